```python
import jax, jax.numpy as jnp
from jax import lax
import numpy as np

D_MODEL = 1024
BATCH = 2
SEQ = 8192
DEPTH = 1
DEC_BATCH = 32
DEC_SEQ = 1
PAST_LEN = 16384
PAGE_SIZE = 128

N_HEADS = 8
HEAD_DIM = 64
N_KV_HEADS = 2
GROUP = N_HEADS // N_KV_HEADS
ROT_DIM = HEAD_DIM // 4
ROPE_THETA = 500000.0
CMP_BLOCK = 32
CMP_STRIDE = 16
CMP_HIDDEN = 64
SEL_BLOCK = 64
N_SEL = 16
SEL_FORCE = 1.0e6
WINDOW = 512
Q_BLOCK = 128
D_RNN = D_MODEL
RNN_BLOCKS = 8
RNN_BW = D_RNN // RNN_BLOCKS
CONV_W = 4
RG_C = 8.0
N_EXPERTS = 64
TOP_K = 6
D_EXPERT = 256
D_SHARED = 256
ROUTED_SCALE = 2.5
MOE_BLOCK = 128
DN_ALPHA = (2.0 * DEPTH) ** 0.25
DN_BETA = (8.0 * DEPTH) ** -0.25
LN_EPS = 1e-5
Q_WIDTH = N_HEADS * HEAD_DIM
KV_WIDTH = 2 * N_KV_HEADS * HEAD_DIM
IN_SPLITS = (Q_WIDTH, KV_WIDTH, KV_WIDTH, KV_WIDTH, 3 * N_HEADS, D_RNN, D_RNN, D_MODEL, D_MODEL)
N_IN = sum(IN_SPLITS)
F32 = jnp.float32

kernel_name = 'hybrid_nsa_rglru_moe_deepnorm_step'


def layer_norm(x, g, b):
    xf = x.astype(F32)
    mu = jnp.mean(xf, axis=-1, keepdims=True)
    var = jnp.mean(jnp.square(xf - mu), axis=-1, keepdims=True)
    return ((xf - mu) * lax.rsqrt(var + LN_EPS) * g + b).astype(x.dtype)


def masked_softmax(s, mask):
    s = jnp.where(mask, s.astype(F32), -jnp.inf)
    m = jnp.max(s, axis=-1, keepdims=True)
    m = jnp.where(jnp.isfinite(m), m, 0.0)
    e = jnp.exp(s - m)
    return e / jnp.maximum(jnp.sum(e, axis=-1, keepdims=True), 1e-30)


def rope_partial(x, pos):
    half = ROT_DIM // 2
    inv = ROPE_THETA ** (-(jnp.arange(half, dtype=F32) * (2.0 / ROT_DIM)))
    ang = pos.astype(F32)[:, None] * inv[None, :]
    cos = jnp.cos(ang)[None, :, None, :]
    sin = jnp.sin(ang)[None, :, None, :]
    xf = x.astype(F32)
    x1, x2 = xf[..., :half], xf[..., half:ROT_DIM]
    out = jnp.concatenate([x1 * cos - x2 * sin, x2 * cos + x1 * sin, xf[..., ROT_DIM:]], axis=-1)
    return out.astype(x.dtype)


def rope_keys(kv, pos):
    return jnp.stack([rope_partial(kv[:, :, 0], pos), kv[:, :, 1]], axis=2)


def mixer_inputs(x, pos, w_in):
    B, T, _ = x.shape
    offs = [int(o) for o in np.cumsum(IN_SPLITS)[:-1]]
    uq, ukc, uks, ukw, ug_nsa, u_rnn, u_gate, g_a, g_b = jnp.split(
        jnp.einsum('btd,dn->btn', x, w_in), offs, axis=-1)
    q = uq.reshape(B, T, N_HEADS, HEAD_DIM)
    q_rot = rope_partial(q, pos).reshape(B, T, N_KV_HEADS, GROUP, HEAD_DIM)
    q = q.reshape(B, T, N_KV_HEADS, GROUP, HEAD_DIM)
    kv_shape = (B, T, 2, N_KV_HEADS, HEAD_DIM)
    kv_c = ukc.reshape(kv_shape)
    kv_s = rope_keys(uks.reshape(kv_shape), pos)
    kv_w = rope_keys(ukw.reshape(kv_shape), pos)
    gates = jax.nn.sigmoid(ug_nsa.astype(F32)).reshape(B, T, N_KV_HEADS, GROUP, 3)
    return q, q_rot, kv_c, kv_s, kv_w, gates, u_rnn, u_gate, g_a, g_b


def compress(rows, pos_emb, w1, w2):
    B, T = rows.shape[:2]
    n_chunk = T // CMP_STRIDE
    parts = CMP_BLOCK // CMP_STRIDE
    n_cmp = n_chunk - parts + 1
    ch = rows[:, :n_chunk * CMP_STRIDE].reshape(B, n_chunk, CMP_STRIDE, N_KV_HEADS, HEAD_DIM)
    w1p = w1.reshape(parts, CMP_STRIDE, HEAD_DIM, CMP_HIDDEN)
    pep = pos_emb.reshape(parts, CMP_STRIDE, HEAD_DIM)
    hid = 0.0
    for m in range(parts):
        part = jnp.einsum('bnshd,sdf->bnhf', ch, w1p[m]) + jnp.einsum('sd,sdf->f', pep[m], w1p[m])
        hid = hid + part[:, m:m + n_cmp]
    return jnp.einsum('bnhf,fd->bnhd', jax.nn.gelu(hid), w2)


def sel_map(n_cmp, n_sel):
    ratio = SEL_BLOCK // CMP_STRIDE
    d = jnp.arange(n_cmp)[:, None] - ratio * jnp.arange(n_sel)[None, :]
    return sum(((d + n >= 0) & (d + n < ratio)).astype(F32) for n in range(CMP_BLOCK // CMP_STRIDE))


def nsa_core(q, q_rot, q_pos, kc, vc, c_end, fetch_sel, n_sel_blocks, kv_w, w_pos, gates):
    B, Tq = q.shape[:2]
    scale = HEAD_DIM ** -0.5
    s_c = jnp.einsum('bqhgd,bchd->bhgqc', q, kc).astype(F32) * scale
    p_c = masked_softmax(s_c, c_end[None, :] <= q_pos[:, None])
    o_c = jnp.einsum('bhgqc,bchd->bqhgd', p_c, vc.astype(F32))
    imp = jnp.einsum('bhgqc,cs->bhqs', p_c, sel_map(kc.shape[1], n_sel_blocks))
    blk = jnp.arange(n_sel_blocks)[None, :]
    cur = (q_pos // SEL_BLOCK)[:, None]
    forced = (blk == 0) | (blk == cur) | (blk == cur - 1)
    score = jnp.where(blk > cur, -jnp.inf, jnp.where(forced, SEL_FORCE, imp))
    n = min(N_SEL, n_sel_blocks)
    _, idx = lax.top_k(score, n)
    idx = idx.transpose(0, 2, 1, 3)
    kv_s = fetch_sel(idx)
    k_s = kv_s[..., 0, :].reshape(B, Tq, N_KV_HEADS, n * SEL_BLOCK, HEAD_DIM)
    v_s = kv_s[..., 1, :].reshape(B, Tq, N_KV_HEADS, n * SEL_BLOCK, HEAD_DIM)
    pos_s = (idx[..., None] * SEL_BLOCK + jnp.arange(SEL_BLOCK)).reshape(B, Tq, N_KV_HEADS, n * SEL_BLOCK)
    m_s = (pos_s <= q_pos[None, :, None, None]).transpose(0, 2, 1, 3)[:, :, None]
    s_s = jnp.einsum('bqhgd,bqhkd->bhgqk', q_rot, k_s).astype(F32) * scale
    o_s = jnp.einsum('bhgqk,bqhkd->bqhgd', masked_softmax(s_s, m_s), v_s.astype(F32))
    diff = q_pos[:, None] - w_pos[None, :]
    m_w = (diff >= 0) & (diff <= WINDOW) & (w_pos[None, :] >= 0)
    s_w = jnp.einsum('bqhgd,bkhd->bhgqk', q_rot, kv_w[:, :, 0]).astype(F32) * scale
    o_w = jnp.einsum('bhgqk,bkhd->bqhgd', masked_softmax(s_w, m_w), kv_w[:, :, 1].astype(F32))
    o = gates[..., 0:1] * o_c + gates[..., 1:2] * o_s + gates[..., 2:3] * o_w
    return o.reshape(B, Tq, Q_WIDTH).astype(q.dtype)


def nsa_prompt(q, q_rot, kv_c, kv_s, kv_w, gates, p):
    B, T = q.shape[:2]
    kc = compress(kv_c[:, :, 0], p['cmp_pos_k'], p['w_cmp_k1'], p['w_cmp_k2'])
    vc = compress(kv_c[:, :, 1], p['cmp_pos_v'], p['w_cmp_v1'], p['w_cmp_v2'])
    c_end = jnp.arange(kc.shape[1], dtype=jnp.int32) * CMP_STRIDE + (CMP_BLOCK - 1)
    n_sel_blocks = T // SEL_BLOCK
    kv_blocks = kv_s.reshape(B, n_sel_blocks, SEL_BLOCK, 2, N_KV_HEADS, HEAD_DIM)
    bb = jnp.arange(B)[:, None, None, None]
    hh = jnp.arange(N_KV_HEADS)[None, None, :, None]

    def fetch(idx):
        return kv_blocks[bb, idx, :, :, hh, :]

    kv_w_pad = jnp.pad(kv_w, ((0, 0), (WINDOW, 0), (0, 0), (0, 0), (0, 0)))

    def one_block(i):
        s0 = i * Q_BLOCK
        take = lambda a: lax.dynamic_slice_in_dim(a, s0, Q_BLOCK, axis=1)
        q_pos = s0 + jnp.arange(Q_BLOCK, dtype=jnp.int32)
        w_rows = lax.dynamic_slice_in_dim(kv_w_pad, s0, WINDOW + Q_BLOCK, axis=1)
        w_pos = s0 - WINDOW + jnp.arange(WINDOW + Q_BLOCK, dtype=jnp.int32)
        return nsa_core(take(q), take(q_rot), q_pos, kc, vc, c_end, fetch, n_sel_blocks,
                        w_rows, w_pos, take(gates))

    o = lax.map(one_block, jnp.arange(T // Q_BLOCK, dtype=jnp.int32))
    return o.transpose(1, 0, 2, 3).reshape(B, T, Q_WIDTH)


def nsa_sample(q, q_rot, kv_c, kv_s, kv_w, gates, pool_c, pool_s, win_buf, page_table, p):
    Bd, S = q.shape[:2]
    q_pos = PAST_LEN + jnp.arange(S, dtype=jnp.int32)
    past_c = pool_c[page_table].reshape(Bd, PAST_LEN, 2, N_KV_HEADS, HEAD_DIM)
    rows_c = jnp.concatenate([past_c, kv_c.astype(past_c.dtype)], axis=1)
    kc = compress(rows_c[:, :, 0], p['cmp_pos_k'], p['w_cmp_k1'], p['w_cmp_k2'])
    vc = compress(rows_c[:, :, 1], p['cmp_pos_v'], p['w_cmp_v1'], p['w_cmp_v2'])
    c_end = jnp.arange(kc.shape[1], dtype=jnp.int32) * CMP_STRIDE + (CMP_BLOCK - 1)
    ns_past = PAST_LEN // SEL_BLOCK
    ns_new = -(-S // SEL_BLOCK)
    per_page = PAGE_SIZE // SEL_BLOCK
    pool_blocks = pool_s.reshape(pool_s.shape[0], per_page, SEL_BLOCK, 2, N_KV_HEADS, HEAD_DIM)
    tail = jnp.pad(kv_s, ((0, 0), (0, ns_new * SEL_BLOCK - S), (0, 0), (0, 0), (0, 0)))
    tail = tail.reshape(Bd, ns_new, SEL_BLOCK, 2, N_KV_HEADS, HEAD_DIM)
    bb = jnp.arange(Bd)[:, None, None, None]
    hh = jnp.arange(N_KV_HEADS)[None, None, :, None]

    def fetch(idx):
        jp = jnp.minimum(idx, ns_past - 1)
        page = page_table[bb, jp // per_page]
        past = pool_blocks[page, jp % per_page, :, :, hh, :]
        new = tail[bb, jnp.clip(idx - ns_past, 0, ns_new - 1), :, :, hh, :]
        return jnp.where((idx < ns_past)[..., None, None, None], past, new.astype(past.dtype))

    rows_w = jnp.concatenate([win_buf, kv_w.astype(win_buf.dtype)], axis=1)
    n_buf = win_buf.shape[1]
    w_pos = PAST_LEN - n_buf + jnp.arange(n_buf + S, dtype=jnp.int32)
    o = nsa_core(q, q_rot, q_pos, kc, vc, c_end, fetch, ns_past + ns_new, rows_w, w_pos, gates)
    return o, rows_w[:, S:]


def rglru_branch(u, u_gate, conv_buf, h0, p):
    B, T, _ = u.shape
    up = jnp.concatenate([conv_buf.astype(u.dtype), u], axis=1)
    xc = p['conv_b'] + sum(up[:, k:k + T] * p['conv_w'][k] for k in range(CONV_W))
    xb = xc.reshape(B, T, RNN_BLOCKS, RNN_BW)
    r = jax.nn.sigmoid((jnp.einsum('btnc,ncd->btnd', xb, p['w_rg_a']).reshape(B, T, D_RNN) + p['b_rg_a']).astype(F32))
    i = jax.nn.sigmoid((jnp.einsum('btnc,ncd->btnd', xb, p['w_rg_i']).reshape(B, T, D_RNN) + p['b_rg_i']).astype(F32))
    log_a = -RG_C * r * jax.nn.softplus(-p['lru_lambda'].astype(F32))
    a = jnp.exp(log_a)
    b = jnp.sqrt(-jnp.expm1(2.0 * log_a)) * (i * xc.astype(F32))

    def step(h, ab):
        h = ab[0] * h + ab[1]
        return h, h

    h_last, hs = lax.scan(step, h0.astype(F32), (a.transpose(1, 0, 2), b.transpose(1, 0, 2)))
    y = hs.transpose(1, 0, 2) * jax.nn.gelu(u_gate.astype(F32))
    return y.astype(u.dtype), up[:, -(CONV_W - 1):], h_last


def swiglu(x, wg, wu, wd):
    return (jax.nn.silu(x @ wg) * (x @ wu)) @ wd


def moe(x, p):
    N, D = x.shape
    scores = jax.nn.sigmoid(jnp.einsum('nd,de->ne', x, p['w_router']).astype(F32))
    _, e_idx = lax.top_k(scores + p['router_bias'].astype(F32), TOP_K)
    w = jnp.take_along_axis(scores, e_idx, axis=1)
    w = w / jnp.sum(w, axis=-1, keepdims=True) * ROUTED_SCALE
    nk = N * TOP_K
    e_flat = e_idx.reshape(nk)
    order = jnp.argsort(e_flat)
    e_sorted = e_flat[order]
    tok_sorted = (order // TOP_K).astype(jnp.int32)
    w_sorted = w.reshape(nk)[order]
    blk = MOE_BLOCK if nk >= MOE_BLOCK * N_EXPERTS else 8
    counts = jnp.bincount(e_flat, length=N_EXPERTS)
    padded = (counts + blk - 1) // blk * blk
    end_pad = jnp.cumsum(padded)
    start_pad = end_pad - padded
    start = jnp.cumsum(counts) - counts
    dest = start_pad[e_sorted] + jnp.arange(nk) - start[e_sorted]
    n_blocks = -(-nk // blk) + N_EXPERTS
    rows = n_blocks * blk
    row_tok = jnp.full((rows,), N, jnp.int32).at[dest].set(tok_sorted)
    row_w = jnp.zeros((rows,), F32).at[dest].set(w_sorted)
    blk_expert = jnp.minimum(jnp.searchsorted(end_pad, jnp.arange(n_blocks) * blk, side='right'), N_EXPERTS - 1)
    x_rows = jnp.concatenate([x, jnp.zeros((1, D), x.dtype)], axis=0)[row_tok].reshape(n_blocks, blk, D)

    def expert_block(args):
        xb, e = args
        return swiglu(xb, p['w_exp_gate'][e], p['w_exp_up'][e], p['w_exp_down'][e])

    y_rows = lax.map(expert_block, (x_rows, blk_expert)).reshape(rows, D)
    routed = jax.ops.segment_sum(y_rows.astype(F32) * row_w[:, None], row_tok, num_segments=N + 1)[:N]
    shared = swiglu(x, p['w_sh_gate'], p['w_sh_up'], p['w_sh_down']).astype(F32)
    return (routed + shared).astype(x.dtype)


def block_tail(x, o_attn, y_rnn, g_a, g_b, p):
    merged = (jax.nn.sigmoid(g_a.astype(F32)) * (o_attn @ p['w_branch_attn'])
              + jax.nn.sigmoid(g_b.astype(F32)) * (y_rnn @ p['w_branch_rnn']))
    mix = merged.astype(x.dtype) @ p['w_out']
    x = layer_norm(DN_ALPHA * x + mix, p['ln1_g'], p['ln1_b'])
    B, T, D = x.shape
    ffn = moe(x.reshape(B * T, D), p).reshape(B, T, D)
    return layer_norm(DN_ALPHA * x + ffn, p['ln2_g'], p['ln2_b'])


def setup_inputs(seed: int = 0) -> dict:
    key = jax.random.key(seed)
    keys = jax.random.split(key, 40)
    cnt = iter(range(40))
    nrm = lambda shape, s: jax.random.normal(keys[next(cnt)], shape, F32) * s
    L = DEPTH
    n_pages = PAST_LEN // PAGE_SIZE
    n_pool = (5 * DEC_BATCH * n_pages + 3) // 4
    w_buf = min(WINDOW, PAST_LEN)
    a0 = jax.random.uniform(keys[next(cnt)], (L, D_RNN), F32, 0.9, 0.999) ** (1.0 / RG_C)
    perm = jax.random.permutation(keys[next(cnt)], n_pool)[:DEC_BATCH * n_pages]
    return {
        'x_prompt': nrm((BATCH, SEQ, D_MODEL), 1.0),
        'x_sample': nrm((DEC_BATCH, DEC_SEQ, D_MODEL), 1.0),
        'cache_cmp_kv': nrm((L, n_pool, PAGE_SIZE, 2, N_KV_HEADS, HEAD_DIM), 1.0),
        'cache_sel_kv': nrm((L, n_pool, PAGE_SIZE, 2, N_KV_HEADS, HEAD_DIM), 1.0),
        'cache_win_kv': nrm((L, DEC_BATCH, w_buf, 2, N_KV_HEADS, HEAD_DIM), 1.0),
        'state_conv': nrm((L, DEC_BATCH, CONV_W - 1, D_RNN), 1.0),
        'state_rnn': nrm((L, DEC_BATCH, D_RNN), 0.5),
        'page_table': perm.reshape(DEC_BATCH, n_pages).astype(jnp.int32),
        'w_in': nrm((L, D_MODEL, N_IN), D_MODEL ** -0.5),
        'conv_w': nrm((L, CONV_W, D_RNN), CONV_W ** -0.5),
        'conv_b': nrm((L, D_RNN), 0.01),
        'w_rg_a': nrm((L, RNN_BLOCKS, RNN_BW, RNN_BW), RNN_BW ** -0.5),
        'b_rg_a': nrm((L, D_RNN), 0.01),
        'w_rg_i': nrm((L, RNN_BLOCKS, RNN_BW, RNN_BW), RNN_BW ** -0.5),
        'b_rg_i': nrm((L, D_RNN), 0.01),
        'lru_lambda': jnp.log(a0) - jnp.log1p(-a0),
        'cmp_pos_k': nrm((L, CMP_BLOCK, HEAD_DIM), 0.1),
        'cmp_pos_v': nrm((L, CMP_BLOCK, HEAD_DIM), 0.1),
        'w_cmp_k1': nrm((L, CMP_BLOCK, HEAD_DIM, CMP_HIDDEN), (CMP_BLOCK * HEAD_DIM) ** -0.5),
        'w_cmp_k2': nrm((L, CMP_HIDDEN, HEAD_DIM), CMP_HIDDEN ** -0.5),
        'w_cmp_v1': nrm((L, CMP_BLOCK, HEAD_DIM, CMP_HIDDEN), (CMP_BLOCK * HEAD_DIM) ** -0.5),
        'w_cmp_v2': nrm((L, CMP_HIDDEN, HEAD_DIM), CMP_HIDDEN ** -0.5),
        'w_branch_attn': nrm((L, Q_WIDTH, D_MODEL), Q_WIDTH ** -0.5),
        'w_branch_rnn': nrm((L, D_RNN, D_MODEL), D_RNN ** -0.5),
        'w_out': nrm((L, D_MODEL, D_MODEL), DN_BETA * D_MODEL ** -0.5),
        'ln1_g': 1.0 + nrm((L, D_MODEL), 0.02),
        'ln1_b': nrm((L, D_MODEL), 0.02),
        'w_router': nrm((L, D_MODEL, N_EXPERTS), D_MODEL ** -0.5),
        'router_bias': nrm((L, N_EXPERTS), 0.01),
        'w_exp_gate': nrm((L, N_EXPERTS, D_MODEL, D_EXPERT), D_MODEL ** -0.5),
        'w_exp_up': nrm((L, N_EXPERTS, D_MODEL, D_EXPERT), D_MODEL ** -0.5),
        'w_exp_down': nrm((L, N_EXPERTS, D_EXPERT, D_MODEL), DN_BETA * D_EXPERT ** -0.5),
        'w_sh_gate': nrm((L, D_MODEL, D_SHARED), D_MODEL ** -0.5),
        'w_sh_up': nrm((L, D_MODEL, D_SHARED), D_MODEL ** -0.5),
        'w_sh_down': nrm((L, D_SHARED, D_MODEL), DN_BETA * D_SHARED ** -0.5),
        'ln2_g': 1.0 + nrm((L, D_MODEL), 0.02),
        'ln2_b': nrm((L, D_MODEL), 0.02),
    }


def reference(x_prompt, x_sample, cache_cmp_kv, cache_sel_kv, cache_win_kv, state_conv, state_rnn,
              page_table, w_in, conv_w, conv_b, w_rg_a, b_rg_a, w_rg_i, b_rg_i, lru_lambda,
              cmp_pos_k, cmp_pos_v, w_cmp_k1, w_cmp_k2, w_cmp_v1, w_cmp_v2, w_branch_attn,
              w_branch_rnn, w_out, ln1_g, ln1_b, w_router, router_bias, w_exp_gate, w_exp_up,
              w_exp_down, w_sh_gate, w_sh_up, w_sh_down, ln2_g, ln2_b):
    B, T, _ = x_prompt.shape
    Bd, S, _ = x_sample.shape
    pos_p = jnp.arange(T, dtype=jnp.int32)
    pos_s = PAST_LEN + jnp.arange(S, dtype=jnp.int32)
    xp, xs = x_prompt, x_sample
    cmp_p, cmp_s, sel_p, sel_s, win_p, win_s = [], [], [], [], [], []
    conv_p, conv_s, rnn_p, rnn_s = [], [], [], []
    for l in range(DEPTH):
        p = dict(w_in=w_in[l], conv_w=conv_w[l], conv_b=conv_b[l], w_rg_a=w_rg_a[l], b_rg_a=b_rg_a[l],
                 w_rg_i=w_rg_i[l], b_rg_i=b_rg_i[l], lru_lambda=lru_lambda[l], cmp_pos_k=cmp_pos_k[l],
                 cmp_pos_v=cmp_pos_v[l], w_cmp_k1=w_cmp_k1[l], w_cmp_k2=w_cmp_k2[l], w_cmp_v1=w_cmp_v1[l],
                 w_cmp_v2=w_cmp_v2[l], w_branch_attn=w_branch_attn[l], w_branch_rnn=w_branch_rnn[l],
                 w_out=w_out[l], ln1_g=ln1_g[l], ln1_b=ln1_b[l], w_router=w_router[l],
                 router_bias=router_bias[l], w_exp_gate=w_exp_gate[l], w_exp_up=w_exp_up[l],
                 w_exp_down=w_exp_down[l], w_sh_gate=w_sh_gate[l], w_sh_up=w_sh_up[l],
                 w_sh_down=w_sh_down[l], ln2_g=ln2_g[l], ln2_b=ln2_b[l])
        q, q_rot, kv_c, kv_s, kv_w, gates, u_rnn, u_gate, g_a, g_b = mixer_inputs(xp, pos_p, p['w_in'])
        o_attn = nsa_prompt(q, q_rot, kv_c, kv_s, kv_w, gates, p)
        y_rnn, conv_new, h_new = rglru_branch(u_rnn, u_gate, jnp.zeros((B, CONV_W - 1, D_RNN), xp.dtype),
                                              jnp.zeros((B, D_RNN), F32), p)
        xp = block_tail(xp, o_attn, y_rnn, g_a, g_b, p)
        cmp_p.append(kv_c)
        sel_p.append(kv_s)
        win_p.append(kv_w[:, -min(WINDOW, T):])
        conv_p.append(conv_new)
        rnn_p.append(h_new)
        q, q_rot, kv_c, kv_s, kv_w, gates, u_rnn, u_gate, g_a, g_b = mixer_inputs(xs, pos_s, p['w_in'])
        o_attn, win_new = nsa_sample(q, q_rot, kv_c, kv_s, kv_w, gates, cache_cmp_kv[l], cache_sel_kv[l],
                                     cache_win_kv[l], page_table, p)
        y_rnn, conv_new, h_new = rglru_branch(u_rnn, u_gate, state_conv[l], state_rnn[l], p)
        xs = block_tail(xs, o_attn, y_rnn, g_a, g_b, p)
        cmp_s.append(kv_c)
        sel_s.append(kv_s)
        win_s.append(win_new)
        conv_s.append(conv_new)
        rnn_s.append(h_new)
    return (xp, xs, jnp.stack(cmp_p), jnp.stack(cmp_s), jnp.stack(sel_p), jnp.stack(sel_s),
            jnp.stack(win_p), jnp.stack(win_s), jnp.stack(conv_p), jnp.stack(conv_s),
            jnp.stack(rnn_p), jnp.stack(rnn_s))
```

```python
import functools

import numpy as np
import jax
import jax.numpy as jnp
from jax import lax
from jax.experimental import pallas as pl
from jax.experimental.pallas import tpu as pltpu

F32 = jnp.float32
BF16 = jnp.bfloat16
I32 = jnp.int32

N_HEADS = 8
HEAD_DIM = 64
N_KV_HEADS = 2
GROUP = N_HEADS // N_KV_HEADS
ROT_DIM = HEAD_DIM // 4
ROPE_THETA = 500000.0
CMP_BLOCK = 32
CMP_STRIDE = 16
CMP_HIDDEN = 64
SEL_BLOCK = 64
N_SEL = 16
SEL_FORCE = 1.0e6
WINDOW = 512
Q_BLOCK = 128
RNN_BLOCKS = 8
CONV_W = 4
RG_C = 8.0
TOP_K = 6
ROUTED_SCALE = 2.5
LN_EPS = 1e-5
PAGE_SIZE = 128

Q_WIDTH = N_HEADS * HEAD_DIM
KV_WIDTH = 2 * N_KV_HEADS * HEAD_DIM
N_GROUPS = 2 * N_KV_HEADS
CHUNK_W = CMP_STRIDE * KV_WIDTH
LANES = 128
NEG = -1.0e30
VMEM_LIMIT = 56 * 1024 * 1024

EXPERT_ROWS = 256
SEL_TK = 512


def _cparams(sem):
    return pltpu.CompilerParams(dimension_semantics=sem, vmem_limit_bytes=VMEM_LIMIT)


def _full(shape):
    n = len(shape)
    return pl.BlockSpec(shape, lambda *a: (0,) * n)


def _dot(a, b):
    return jnp.dot(a, b, preferred_element_type=F32)


def _dot_nt(a, b):
    return lax.dot_general(a, b, (((1,), (1,)), ((), ())), preferred_element_type=F32)


def _gelu(x):
    return 0.5 * x * (1.0 + jnp.tanh(np.sqrt(2.0 / np.pi) * (x + 0.044715 * (x * x * x))))


def _sigmoid(x):
    return 1.0 / (1.0 + jnp.exp(-x))


def _layer_norm(z, g, b):
    mu = jnp.mean(z, axis=-1, keepdims=True)
    d = z - mu
    var = jnp.mean(d * d, axis=-1, keepdims=True)
    return d * lax.rsqrt(var + LN_EPS) * g + b


def _rope_tables(pos):
    half = ROT_DIM // 2
    inv = ROPE_THETA ** (-(jnp.arange(half, dtype=F32) * (2.0 / ROT_DIM)))
    ang = pos.astype(F32)[:, None] * inv[None, :]
    cos, sin = jnp.cos(ang), jnp.sin(ang)
    m = pos.shape[0]
    one = jnp.ones((m, HEAD_DIM - ROT_DIM), F32)
    zero = jnp.zeros((m, HEAD_DIM - ROT_DIM), F32)
    zh = jnp.zeros((m, half), F32)
    c = jnp.concatenate([cos, cos, one], axis=1)
    s1 = jnp.concatenate([-sin, zh, zero], axis=1)
    s2 = jnp.concatenate([zh, sin, zero], axis=1)
    rep = LANES // HEAD_DIM
    return jnp.tile(c, (1, rep)), jnp.tile(s1, (1, rep)), jnp.tile(s2, (1, rep))


def _rope128(x, c, s1, s2):
    half = ROT_DIM // 2
    return x * c + pltpu.roll(x, LANES - half, 1) * s1 + pltpu.roll(x, half, 1) * s2


def _proj_kernel(x_ref, c_ref, s1_ref, s2_ref, wq_ref, wkv_ref, wg_ref, ww_ref,
                 q_ref, qr_ref, kvc_ref, kvs_ref, kvw_ref, ksh_ref, vsh_ref, kwh_ref, vwh_ref,
                 gate_ref, urnn_ref, ugate_ref, ga_ref, gb_ref):
    xb = x_ref[...].astype(BF16)
    c, s1, s2 = c_ref[...], s1_ref[...], s2_ref[...]
    scale = HEAD_DIM ** -0.5
    q = _dot(xb, wq_ref[...]) * scale
    for j in range(Q_WIDTH // LANES):
        ch = q[:, LANES * j:LANES * (j + 1)]
        rot = _rope128(ch, c, s1, s2)
        for hh in range(LANES // HEAD_DIM):
            head = j * (LANES // HEAD_DIM) + hh
            q_ref[head] = ch[:, HEAD_DIM * hh:HEAD_DIM * (hh + 1)].astype(BF16)
            qr_ref[head] = rot[:, HEAD_DIM * hh:HEAD_DIM * (hh + 1)].astype(BF16)
    kv = _dot(xb, wkv_ref[...])
    kvc_ref[...] = kv[:, :KV_WIDTH]
    for base, full_ref, kh_ref, vh_ref in ((KV_WIDTH, kvs_ref, ksh_ref, vsh_ref),
                                           (2 * KV_WIDTH, kvw_ref, kwh_ref, vwh_ref)):
        keys = _rope128(kv[:, base:base + LANES], c, s1, s2)
        vals = kv[:, base + LANES:base + 2 * LANES]
        full_ref[:, :LANES] = keys
        full_ref[:, LANES:] = vals
        for h in range(N_KV_HEADS):
            kh_ref[h] = keys[:, HEAD_DIM * h:HEAD_DIM * (h + 1)].astype(BF16)
            vh_ref[h] = vals[:, HEAD_DIM * h:HEAD_DIM * (h + 1)].astype(BF16)
    gate_ref[...] = _sigmoid(_dot(xb, wg_ref[...]))
    d = urnn_ref.shape[-1]
    for k, ref in enumerate((urnn_ref, ugate_ref, ga_ref, gb_ref)):
        ref[...] = _dot(xb, ww_ref[:, d * k:d * (k + 1)])


def _split_w_in(w_in):
    d = w_in.shape[0]
    o = 0
    wq = w_in[:, o:o + Q_WIDTH]; o += Q_WIDTH
    wkv = w_in[:, o:o + 3 * KV_WIDTH]; o += 3 * KV_WIDTH
    wg = w_in[:, o:o + 3 * N_HEADS]; o += 3 * N_HEADS
    ww = w_in[:, o:]
    per = 3 * GROUP
    wg2 = jnp.zeros((d, N_KV_HEADS * LANES), w_in.dtype)
    for h in range(N_KV_HEADS):
        wg2 = wg2.at[:, h * LANES:h * LANES + per].set(wg[:, h * per:(h + 1) * per])
    return wq.astype(BF16), wkv.astype(BF16), wg2.astype(BF16), ww.astype(BF16)


def _proj(x2d, pos, wparts, tm):
    m, d = x2d.shape
    wq, wkv, wg, ww = wparts
    c, s1, s2 = _rope_tables(pos)
    dw = ww.shape[1] // 4
    row = lambda w: pl.BlockSpec((tm, w), lambda i: (i, 0))
    hm = lambda n: pl.BlockSpec((n, tm, HEAD_DIM), lambda i: (0, i, 0))
    out_shape = (
        jax.ShapeDtypeStruct((N_HEADS, m, HEAD_DIM), BF16),
        jax.ShapeDtypeStruct((N_HEADS, m, HEAD_DIM), BF16),
        jax.ShapeDtypeStruct((m, KV_WIDTH), F32),
        jax.ShapeDtypeStruct((m, KV_WIDTH), F32),
        jax.ShapeDtypeStruct((m, KV_WIDTH), F32),
        jax.ShapeDtypeStruct((N_KV_HEADS, m, HEAD_DIM), BF16),
        jax.ShapeDtypeStruct((N_KV_HEADS, m, HEAD_DIM), BF16),
        jax.ShapeDtypeStruct((N_KV_HEADS, m, HEAD_DIM), BF16),
        jax.ShapeDtypeStruct((N_KV_HEADS, m, HEAD_DIM), BF16),
        jax.ShapeDtypeStruct((m, N_KV_HEADS * LANES), F32),
        jax.ShapeDtypeStruct((m, dw), F32),
        jax.ShapeDtypeStruct((m, dw), F32),
        jax.ShapeDtypeStruct((m, dw), F32),
        jax.ShapeDtypeStruct((m, dw), F32),
    )
    out_specs = (hm(N_HEADS), hm(N_HEADS), row(KV_WIDTH), row(KV_WIDTH), row(KV_WIDTH),
                 hm(N_KV_HEADS), hm(N_KV_HEADS), hm(N_KV_HEADS), hm(N_KV_HEADS),
                 row(N_KV_HEADS * LANES), row(dw), row(dw), row(dw), row(dw))
    return pl.pallas_call(
        _proj_kernel,
        out_shape=out_shape,
        grid=(m // tm,),
        in_specs=[row(d), row(LANES), row(LANES), row(LANES),
                  _full(wq.shape), _full(wkv.shape), _full(wg.shape), _full(ww.shape)],
        out_specs=out_specs,
        compiler_params=_cparams(("parallel",)),
        name="proj",
    )(x2d, c, s1, s2, wq, wkv, wg, ww)


def _compress_weights(p):
    parts = CMP_BLOCK // CMP_STRIDE
    gw = N_GROUPS * CMP_HIDDEN
    w1 = jnp.zeros((CMP_STRIDE, N_GROUPS, HEAD_DIM, parts, N_GROUPS, CMP_HIDDEN), F32)
    pe = jnp.zeros((8, CMP_STRIDE, N_GROUPS, HEAD_DIM), F32)
    w2 = jnp.zeros((N_GROUPS, CMP_HIDDEN, N_GROUPS, HEAD_DIM), F32)
    for g in range(N_GROUPS):
        kv = g // N_KV_HEADS
        w1_src = (p['w_cmp_k1'], p['w_cmp_v1'])[kv].reshape(parts, CMP_STRIDE, HEAD_DIM, CMP_HIDDEN)
        pe_src = (p['cmp_pos_k'], p['cmp_pos_v'])[kv].reshape(parts, CMP_STRIDE, HEAD_DIM)
        w2_src = (p['w_cmp_k2'], p['w_cmp_v2'])[kv]
        for m in range(parts):
            w1 = w1.at[:, g, :, m, g, :].set(w1_src[m])
            pe = pe.at[m, :, g, :].set(pe_src[m])
        w2 = w2.at[g, :, g, :].set(w2_src)
    return (w1.reshape(CHUNK_W, parts * gw).astype(BF16), pe.reshape(8, CHUNK_W).astype(BF16),
            w2.reshape(gw, N_GROUPS * HEAD_DIM).astype(BF16))


def _compress_rows(c, pec, w2):
    gw = N_GROUPS * CMP_HIDDEN
    n = c.shape[0]
    const = pec[0:1, :gw] + pec[1:2, gw:]
    hid = c[:, :gw] + pltpu.roll(c[:, gw:], n - 1, 0) + const
    return _dot(_gelu(hid).astype(BF16), w2)


def _compress_prompt_kernel(ch_ref, w1_ref, pe_ref, w2_ref, kc_ref, vc_ref):
    c = _dot(ch_ref[0].astype(BF16), w1_ref[...])
    pec = _dot(pe_ref[...], w1_ref[...])
    out = _compress_rows(c, pec, w2_ref[...])
    for h in range(N_KV_HEADS):
        kc_ref[0, h] = out[:, HEAD_DIM * h:HEAD_DIM * (h + 1)].astype(BF16)
        vc_ref[0, h] = out[:, HEAD_DIM * (N_KV_HEADS + h):HEAD_DIM * (N_KV_HEADS + h + 1)].astype(BF16)


def _compress_prompt(kvc, b, t, cw):
    w1, pe, w2 = cw
    nch = t // CMP_STRIDE
    chunks = kvc.reshape(b, nch, CHUNK_W)
    out = jax.ShapeDtypeStruct((b, N_KV_HEADS, nch, HEAD_DIM), BF16)
    ospec = pl.BlockSpec((1, N_KV_HEADS, nch, HEAD_DIM), lambda i: (i, 0, 0, 0))
    return pl.pallas_call(
        _compress_prompt_kernel,
        out_shape=(out, out),
        grid=(b,),
        in_specs=[pl.BlockSpec((1, nch, CHUNK_W), lambda i: (i, 0, 0)),
                  _full(w1.shape), _full(pe.shape), _full(w2.shape)],
        out_specs=(ospec, ospec),
        compiler_params=_cparams(("parallel",)),
        name="compress_prompt",
    )(chunks, w1, pe, w2)


def _sel_map(n_cmp_pad, n_cmp, n_sel, n_sel_pad):
    ratio = SEL_BLOCK // CMP_STRIDE
    i = np.arange(n_cmp_pad)[:, None]
    j = np.arange(n_sel_pad)[None, :]
    d = i - ratio * j
    m = sum(((d + n >= 0) & (d + n < ratio)).astype(np.float32) for n in range(CMP_BLOCK // CMP_STRIDE))
    m = m * (i < n_cmp) * (j < n_sel)
    return jnp.asarray(m, BF16)


def _masked_softmax_rows(s, valid):
    s = jnp.where(valid, s, -jnp.inf)
    m = jnp.max(s, axis=-1, keepdims=True)
    m = jnp.where(m == -jnp.inf, 0.0, m)
    e = jnp.exp(s - m)
    return e / jnp.maximum(jnp.sum(e, axis=-1, keepdims=True), 1e-30)


def _top_blocks(score, blk, n):
    width = float(score.shape[-1])
    sel = jnp.zeros(score.shape, F32)
    for _ in range(n):
        m = jnp.max(score, axis=-1, keepdims=True)
        idx = jnp.min(jnp.where(score == m, blk, width), axis=-1, keepdims=True)
        pick = blk == idx
        sel = jnp.where(pick, 1.0, sel)
        score = jnp.where(pick, -jnp.inf, score)
    return sel


def _nsa_prompt_kernel(q_ref, qr_ref, kc_ref, vc_ref, ks_ref, vs_ref, kw_ref, vw_ref, g_ref, smap_ref,
                       o_ref, *, n_sel_blocks):
    qb = pl.program_id(2)
    s0 = qb * Q_BLOCK
    rows = GROUP * Q_BLOCK
    q = q_ref[...].reshape(rows, HEAD_DIM)
    qr = qr_ref[...].reshape(rows, HEAD_DIM)
    pos1 = s0 + lax.broadcasted_iota(I32, (Q_BLOCK, 1), 0)
    pos = s0 + (lax.broadcasted_iota(I32, (rows, 1), 0) & (Q_BLOCK - 1))

    kc = kc_ref[0, 0]
    ncp = kc.shape[0]
    c_end = lax.broadcasted_iota(I32, (1, ncp), 1) * CMP_STRIDE + (CMP_BLOCK - 1)
    p_c = _masked_softmax_rows(_dot_nt(q, kc), c_end <= pos).astype(BF16)
    o_c = _dot(p_c, vc_ref[0, 0])
    imp = _dot(p_c[0:Q_BLOCK], smap_ref[...])
    for g in range(1, GROUP):
        imp = imp + _dot(p_c[g * Q_BLOCK:(g + 1) * Q_BLOCK], smap_ref[...])

    blk_i = lax.broadcasted_iota(I32, (1, LANES), 1)
    blk = blk_i.astype(F32)
    cur = pos1 // SEL_BLOCK
    forced = (blk_i == 0) | (blk_i == cur) | (blk_i == cur - 1)
    score = jnp.where(blk_i > cur, -jnp.inf, jnp.where(forced, SEL_FORCE, imp))
    sel = _top_blocks(score, blk, min(N_SEL, n_sel_blocks))
    sel = jnp.where(blk_i <= cur, sel, 0.0).astype(BF16)

    per_tile = SEL_TK // SEL_BLOCK
    jrow = lax.broadcasted_iota(I32, (LANES, SEL_TK), 0)
    cblk = lax.broadcasted_iota(I32, (LANES, SEL_TK), 1) // SEL_BLOCK
    kcol = lax.broadcasted_iota(I32, (1, SEL_TK), 1)

    def sel_tile(kt, carry, causal):
        m_i, l_i, acc = carry
        start = pl.multiple_of(kt * SEL_TK, SEL_TK)
        k = ks_ref[0, pl.ds(start, SEL_TK), :]
        v = vs_ref[0, pl.ds(start, SEL_TK), :]
        s = _dot_nt(qr, k).reshape(GROUP, Q_BLOCK, SEL_TK)
        expand = jnp.where(jrow - kt * per_tile == cblk, 1.0, 0.0).astype(BF16)
        ok = _dot(sel, expand) > 0.5
        if causal:
            ok = ok & (start + kcol <= pos1)
        s = jnp.where(ok[None], s, NEG)
        m_new = jnp.maximum(m_i, jnp.max(s, axis=-1, keepdims=True))
        alpha = jnp.exp(m_i - m_new)
        p = jnp.exp(s - m_new)
        l_new = alpha * l_i + jnp.sum(p, axis=-1, keepdims=True)
        pv = _dot(p.reshape(rows, SEL_TK).astype(BF16), v).reshape(GROUP, Q_BLOCK, HEAD_DIM)
        return m_new, l_new, alpha * acc + pv

    init = (jnp.full((GROUP, Q_BLOCK, 1), NEG, F32), jnp.zeros((GROUP, Q_BLOCK, 1), F32),
            jnp.zeros((GROUP, Q_BLOCK, HEAD_DIM), F32))
    n_full = s0 // SEL_TK
    carry = lax.fori_loop(0, n_full, lambda kt, cr: sel_tile(kt, cr, False), init)
    _, l_s, acc_s = sel_tile(n_full, carry, True)
    o_s = (acc_s / l_s).reshape(rows, HEAD_DIM)

    wlen = WINDOW + Q_BLOCK
    wstart = pl.multiple_of(jnp.maximum(s0 - WINDOW, 0), Q_BLOCK)
    kpos = wstart + lax.broadcasted_iota(I32, (1, wlen), 1)
    diff = pos - kpos
    p_w = _masked_softmax_rows(_dot_nt(qr, kw_ref[0, pl.ds(wstart, wlen), :]), (diff >= 0) & (diff <= WINDOW))
    o_w = _dot(p_w.astype(BF16), vw_ref[0, pl.ds(wstart, wlen), :])

    gates = g_ref[...]
    outs = []
    for g in range(GROUP):
        r = slice(g * Q_BLOCK, (g + 1) * Q_BLOCK)
        outs.append(gates[:, 3 * g:3 * g + 1] * o_c[r] + gates[:, 3 * g + 1:3 * g + 2] * o_s[r]
                    + gates[:, 3 * g + 2:3 * g + 3] * o_w[r])
    o_ref[...] = jnp.concatenate(outs, axis=1).astype(o_ref.dtype)


def _nsa_prompt(q_hm, qr_hm, kc, vc, ksh, vsh, kwh, vwh, gates, b, t):
    nqb = t // Q_BLOCK
    n_sel_blocks = t // SEL_BLOCK
    ncp = kc.shape[2]
    assert N_SEL <= n_sel_blocks <= LANES and t % SEL_TK == 0 and t >= WINDOW + Q_BLOCK
    smap = _sel_map(ncp, ncp - 1, n_sel_blocks, LANES)
    qspec = pl.BlockSpec((GROUP, Q_BLOCK, HEAD_DIM), lambda bi, h, i: (h, bi * nqb + i, 0))
    cspec = pl.BlockSpec((1, 1, ncp, HEAD_DIM), lambda bi, h, i: (bi, h, 0, 0))
    kspec = pl.BlockSpec((1, t, HEAD_DIM), lambda bi, h, i: (h, bi, 0))
    return pl.pallas_call(
        functools.partial(_nsa_prompt_kernel, n_sel_blocks=n_sel_blocks),
        out_shape=jax.ShapeDtypeStruct((b * t, Q_WIDTH), BF16),
        grid=(b, N_KV_HEADS, nqb),
        in_specs=[qspec, qspec, cspec, cspec, kspec, kspec, kspec, kspec,
                  pl.BlockSpec((Q_BLOCK, LANES), lambda bi, h, i: (bi * nqb + i, h)),
                  _full(smap.shape)],
        out_specs=pl.BlockSpec((Q_BLOCK, GROUP * HEAD_DIM), lambda bi, h, i: (bi * nqb + i, h)),
        compiler_params=_cparams(("parallel", "parallel", "arbitrary")),
        name="nsa_prompt",
    )(q_hm, qr_hm, kc, vc, ksh, vsh, kwh, vwh, gates, smap)


def _rglru_gates(xc, ug, wa_ref, ba, wi_ref, bi, lam):
    xcb = xc.astype(BF16)
    bw = xc.shape[-1] // RNN_BLOCKS
    ra = jnp.concatenate([_dot(xcb[:, bw * n:bw * (n + 1)], wa_ref[n]) for n in range(RNN_BLOCKS)], axis=1)
    ri = jnp.concatenate([_dot(xcb[:, bw * n:bw * (n + 1)], wi_ref[n]) for n in range(RNN_BLOCKS)], axis=1)
    r = _sigmoid(ra + ba)
    i = _sigmoid(ri + bi)
    softplus = jnp.maximum(-lam, 0.0) + jnp.log(1.0 + jnp.exp(-jnp.abs(lam)))
    log_a = -RG_C * r * softplus
    a = jnp.exp(log_a)
    b = jnp.sqrt(1.0 - jnp.exp(2.0 * log_a)) * (i * xc)
    return a, b, _gelu(ug)


def _rglru_prompt_kernel(u_ref, ug_ref, cw_ref, cb_ref, wa_ref, ba_ref, wi_ref, bi_ref, lam_ref,
                         y_ref, h_ref, up_s, a_s, b_s, h_s):
    tt = u_ref.shape[1]
    t = pl.program_id(1)
    halo = CONV_W - 1

    @pl.when(t == 0)
    def _():
        up_s[0:8, :] = jnp.zeros((8, up_s.shape[1]), F32)
        h_s[...] = jnp.zeros(h_s.shape, F32)

    @pl.when(t > 0)
    def _():
        up_s[8 - halo:8, :] = up_s[8 + tt - halo:8 + tt, :]

    up_s[8:8 + tt, :] = u_ref[0]
    xc = cb_ref[...] + up_s[pl.ds(8 - halo, tt), :] * cw_ref[0:1, :]
    for k in range(1, CONV_W):
        xc = xc + up_s[pl.ds(8 - halo + k, tt), :] * cw_ref[k:k + 1, :]
    a, b, gate = _rglru_gates(xc, ug_ref[0], wa_ref, ba_ref[...], wi_ref, bi_ref[...], lam_ref[...])
    a_s[...] = a
    b_s[...] = b

    def step(i, h):
        h = a_s[pl.ds(i, 1), :] * h + b_s[pl.ds(i, 1), :]
        b_s[pl.ds(i, 1), :] = h
        return h

    h = lax.fori_loop(0, tt, step, h_s[...], unroll=8)
    h_s[...] = h
    h_ref[0] = h
    y_ref[0] = (b_s[...] * gate).astype(y_ref.dtype)


def _rglru_weights(p):
    row = lambda v: v.reshape(1, -1).astype(F32)
    return (p['conv_w'].astype(F32), row(p['conv_b']), p['w_rg_a'].astype(BF16), row(p['b_rg_a']),
            p['w_rg_i'].astype(BF16), row(p['b_rg_i']), row(p['lru_lambda']))


def _rglru_prompt(u, ug, rw, b, t, tt):
    d = u.shape[1]
    nt = t // tt
    seq = pl.BlockSpec((1, tt, d), lambda bi, i: (bi, i, 0))
    return pl.pallas_call(
        _rglru_prompt_kernel,
        out_shape=(jax.ShapeDtypeStruct((b, t, d), BF16), jax.ShapeDtypeStruct((b, 1, d), F32)),
        grid=(b, nt),
        in_specs=[seq, seq] + [_full(w.shape) for w in rw],
        out_specs=(seq, pl.BlockSpec((1, 1, d), lambda bi, i: (bi, 0, 0))),
        scratch_shapes=[pltpu.VMEM((tt + 8, d), F32), pltpu.VMEM((tt, d), F32), pltpu.VMEM((tt, d), F32),
                        pltpu.VMEM((1, d), F32)],
        compiler_params=_cparams(("parallel", "arbitrary")),
        name="rglru_prompt",
    )(u.reshape(b, t, d), ug.reshape(b, t, d), *rw)


def _rglru_step_kernel(u_ref, ug_ref, cs_ref, h0_ref, cw_ref, cb_ref, wa_ref, ba_ref, wi_ref, bi_ref, lam_ref,
                       y_ref, h_ref):
    xc = cb_ref[...] + u_ref[...] * cw_ref[CONV_W - 1:CONV_W, :]
    for k in range(CONV_W - 1):
        xc = xc + cs_ref[k] * cw_ref[k:k + 1, :]
    a, b, gate = _rglru_gates(xc, ug_ref[...], wa_ref, ba_ref[...], wi_ref, bi_ref[...], lam_ref[...])
    h = a * h0_ref[...] + b
    h_ref[...] = h
    y_ref[...] = (h * gate).astype(y_ref.dtype)


def _rglru_step(u, ug, conv_state, h0, rw):
    n, d = u.shape
    args = (u, ug, conv_state, h0) + tuple(rw)
    return pl.pallas_call(
        _rglru_step_kernel,
        out_shape=(jax.ShapeDtypeStruct((n, d), BF16), jax.ShapeDtypeStruct((n, d), F32)),
        grid=(1,),
        in_specs=[_full(a.shape) for a in args],
        out_specs=(_full((n, d)), _full((n, d))),
        compiler_params=_cparams(("arbitrary",)),
        name="rglru_step",
    )(*args)


def _row_gather(rows, starts, copy_fn, unroll=8):
    def body(j, c):
        starts(copy_fn(j))
        return c
    lax.fori_loop(0, rows, body, 0, unroll=unroll)


def _nsa_sample_cmp_kernel(pt_ref, pool_ref, q_ref, w1_ref, pe_ref, w2_ref, smap_ref,
                           oc_ref, idx_ref, cbuf, sem, *, n_pages, q_pos, n_sel_blocks):
    b = pl.program_id(0)
    nb = pl.num_programs(0)
    rows_pp = PAGE_SIZE // CMP_STRIDE

    def page_copy(seq, slot, j):
        return pltpu.make_async_copy(pool_ref.at[pt_ref[seq * n_pages + j]],
                                     cbuf.at[slot, pl.ds(j * rows_pp, rows_pp)], sem.at[slot])

    @pl.when(b == 0)
    def _():
        _row_gather(n_pages, lambda c: c.start(), lambda j: page_copy(0, 0, j))

    @pl.when(b + 1 < nb)
    def _():
        _row_gather(n_pages, lambda c: c.start(), lambda j: page_copy(b + 1, (b + 1) % 2, j))

    slot = b % 2
    _row_gather(n_pages, lambda c: c.wait(), lambda j: page_copy(b, slot, j))

    nch = n_pages * rows_pp
    rc = min(256, nch)
    c = jnp.concatenate([_dot(cbuf[slot, pl.ds(r * rc, rc), :].astype(BF16), w1_ref[...])
                         for r in range(nch // rc)], axis=0)
    pec = _dot(pe_ref[...], w1_ref[...])
    kv = _compress_rows(c, pec, w2_ref[...]).astype(BF16)

    q = q_ref[0]
    c_end = lax.broadcasted_iota(I32, (1, nch), 1) * CMP_STRIDE + (CMP_BLOCK - 1)
    row = lax.broadcasted_iota(I32, (N_HEADS, 1), 0)
    width = smap_ref.shape[1]
    blk_i = lax.broadcasted_iota(I32, (1, width), 1)
    cur = q_pos // SEL_BLOCK
    forced = (blk_i == 0) | (blk_i == cur) | (blk_i == cur - 1)
    o_c = jnp.zeros((N_HEADS, HEAD_DIM), F32)
    score = jnp.full((N_HEADS, width), -jnp.inf, F32)
    for h in range(N_KV_HEADS):
        kc = kv[:, HEAD_DIM * h:HEAD_DIM * (h + 1)]
        vc = kv[:, HEAD_DIM * (N_KV_HEADS + h):HEAD_DIM * (N_KV_HEADS + h + 1)]
        p = _masked_softmax_rows(_dot_nt(q, kc), c_end <= q_pos).astype(BF16)
        in_group = (row // GROUP) == h
        o_c = jnp.where(in_group, _dot(p, vc), o_c)
        imp = jnp.sum(jnp.where(in_group, _dot(p, smap_ref[...]), 0.0), axis=0, keepdims=True)
        sc = jnp.where(blk_i > cur, -jnp.inf, jnp.where(forced, SEL_FORCE, imp))
        score = jnp.where(row == h, sc, score)
    oc_ref[0] = o_c

    n = min(N_SEL, n_sel_blocks)
    blk = blk_i.astype(F32)
    lane = lax.broadcasted_iota(I32, (1, LANES), 1)
    idx_out = jnp.zeros((N_HEADS, LANES), I32)
    for r in range(n):
        m = jnp.max(score, axis=-1, keepdims=True)
        idx = jnp.min(jnp.where(score == m, blk, float(width)), axis=-1, keepdims=True)
        score = jnp.where(blk == idx, -jnp.inf, score)
        idx_out = jnp.where(lane == r, idx.astype(I32), idx_out)
    idx_ref[0] = idx_out


def _nsa_sample_cmp(page_table, pool_c, q_seq, cw, past_len):
    bd, n_pages = page_table.shape
    w1, pe, w2 = cw
    n_pool = pool_c.shape[0]
    rows_pp = PAGE_SIZE // CMP_STRIDE
    nch = n_pages * rows_pp
    n_sel_blocks = past_len // SEL_BLOCK + 1
    width = -(-n_sel_blocks // LANES) * LANES
    smap = _sel_map(nch, nch - 1, n_sel_blocks, width)
    pool = pool_c.reshape(n_pool, rows_pp, CHUNK_W)
    grid_spec = pltpu.PrefetchScalarGridSpec(
        num_scalar_prefetch=1,
        grid=(bd,),
        in_specs=[pl.BlockSpec(memory_space=pl.ANY),
                  pl.BlockSpec((1, N_HEADS, HEAD_DIM), lambda i, pt: (i, 0, 0)),
                  pl.BlockSpec(w1.shape, lambda i, pt: (0, 0)),
                  pl.BlockSpec(pe.shape, lambda i, pt: (0, 0)),
                  pl.BlockSpec(w2.shape, lambda i, pt: (0, 0)),
                  pl.BlockSpec(smap.shape, lambda i, pt: (0, 0))],
        out_specs=(pl.BlockSpec((1, N_HEADS, HEAD_DIM), lambda i, pt: (i, 0, 0)),
                   pl.BlockSpec((1, N_HEADS, LANES), lambda i, pt: (i, 0, 0))),
        scratch_shapes=[pltpu.VMEM((2, nch, CHUNK_W), F32), pltpu.SemaphoreType.DMA((2,))],
    )
    return pl.pallas_call(
        functools.partial(_nsa_sample_cmp_kernel, n_pages=n_pages, q_pos=past_len, n_sel_blocks=n_sel_blocks),
        out_shape=(jax.ShapeDtypeStruct((bd, N_HEADS, HEAD_DIM), F32),
                   jax.ShapeDtypeStruct((bd, N_HEADS, LANES), I32)),
        grid_spec=grid_spec,
        compiler_params=_cparams(("arbitrary",)),
        name="nsa_sample_cmp",
    )(page_table.reshape(-1), pool, q_seq, w1, pe, w2, smap)


def _nsa_sample_sel_kernel(pt_ref, idx_s_ref, pool_ref, qr_ref, idx_ref, kvs_ref, kvw_ref, win_ref, oc_ref, g_ref,
                           o_ref, wout_ref, sbuf, sem, *, n_pages, n_sel, ns_past):
    b = pl.program_id(0)
    per_page = PAGE_SIZE // SEL_BLOCK
    nkeys = n_sel * SEL_BLOCK

    def block_copy(j):
        jp = jnp.minimum(idx_s_ref[b * (N_KV_HEADS * n_sel) + j], ns_past - 1)
        page = pt_ref[b * n_pages + jp // per_page]
        return pltpu.make_async_copy(pool_ref.at[page, pl.ds((jp % per_page) * SEL_BLOCK, SEL_BLOCK)],
                                     sbuf.at[j], sem.at[0])

    _row_gather(N_KV_HEADS * n_sel, lambda c: c.start(), block_copy)
    _row_gather(N_KV_HEADS * n_sel, lambda c: c.wait(), block_copy)

    qr = qr_ref[0]
    qf = qr.astype(F32)
    row = lax.broadcasted_iota(I32, (N_HEADS, 1), 0)
    key = lax.broadcasted_iota(I32, (1, nkeys), 1)
    first = (key % SEL_BLOCK) == 0
    expand = jnp.where(lax.broadcasted_iota(I32, (LANES, nkeys), 0)
                       == lax.broadcasted_iota(I32, (LANES, nkeys), 1) // SEL_BLOCK, 1.0, 0.0).astype(BF16)
    is_new = _dot(jnp.where(idx_ref[0] >= ns_past, 1.0, 0.0).astype(BF16), expand) > 0.5
    kvs_new = kvs_ref[0].astype(BF16).astype(F32)
    kvw_new = kvw_ref[0].astype(BF16).astype(F32)
    win = win_ref[0]
    o_s = jnp.zeros((N_HEADS, HEAD_DIM), F32)
    o_w = jnp.zeros((N_HEADS, HEAD_DIM), F32)
    for h in range(N_KV_HEADS):
        in_group = (row // GROUP) == h
        ksl = slice(HEAD_DIM * h, HEAD_DIM * (h + 1))
        vsl = slice(HEAD_DIM * (N_KV_HEADS + h), HEAD_DIM * (N_KV_HEADS + h + 1))
        blocks = sbuf[h * n_sel:(h + 1) * n_sel]
        k = blocks[:, :, ksl].reshape(nkeys, HEAD_DIM).astype(BF16)
        v = blocks[:, :, vsl].reshape(nkeys, HEAD_DIM).astype(BF16)
        newblk = is_new[h:h + 1]
        newkey = newblk & first
        s_new = jnp.sum(qf * kvs_new[:, ksl], axis=-1, keepdims=True)
        s = jnp.where(newkey, s_new, _dot_nt(qr, k))
        p = _masked_softmax_rows(s, jnp.logical_not(newblk) | first)
        p_new = jnp.sum(jnp.where(newkey, p, 0.0), axis=-1, keepdims=True).astype(BF16).astype(F32)
        o_h = _dot(jnp.where(newkey, 0.0, p).astype(BF16), v) + p_new * kvs_new[:, vsl]
        o_s = jnp.where(in_group, o_h, o_s)
        s_buf = _dot_nt(qr, win[:, ksl].astype(BF16))
        s_cur = jnp.sum(qf * kvw_new[:, ksl], axis=-1, keepdims=True)
        m = jnp.maximum(jnp.max(s_buf, axis=-1, keepdims=True), s_cur)
        e_buf = jnp.exp(s_buf - m)
        e_cur = jnp.exp(s_cur - m)
        den = jnp.maximum(jnp.sum(e_buf, axis=-1, keepdims=True) + e_cur, 1e-30)
        o_h = (_dot((e_buf / den).astype(BF16), win[:, vsl].astype(BF16))
               + (e_cur / den).astype(BF16).astype(F32) * kvw_new[:, vsl])
        o_w = jnp.where(in_group, o_h, o_w)

    gw = g_ref.shape[-1]
    gl = lax.broadcasted_iota(I32, (N_HEADS, gw), 1)
    gbase = (row // GROUP) * LANES + (row % GROUP) * 3
    gates = jnp.broadcast_to(g_ref[0], (N_HEADS, gw))
    gate = lambda j: jnp.sum(jnp.where(gl == gbase + j, gates, 0.0), axis=-1, keepdims=True)
    o_ref[0] = gate(0) * oc_ref[0] + gate(1) * o_s + gate(2) * o_w

    n_buf = win.shape[0]
    wout_ref[0, 0:n_buf - 1, :] = win_ref[0, 1:n_buf, :]
    wout_ref[0, n_buf - 1:n_buf, :] = kvw_ref[0]


def _nsa_sample_sel(page_table, idx, pool_s, qr_seq, kvs_new, kvw_new, win_buf, o_c, gates, past_len):
    bd, n_pages = page_table.shape
    n_sel = min(N_SEL, past_len // SEL_BLOCK + 1)
    n_buf = win_buf.shape[1]
    gw = gates.shape[-1]
    seq3 = lambda s1, s2: pl.BlockSpec((1, s1, s2), lambda i, pt, ix: (i, 0, 0))
    grid_spec = pltpu.PrefetchScalarGridSpec(
        num_scalar_prefetch=2,
        grid=(bd,),
        in_specs=[pl.BlockSpec(memory_space=pl.ANY), seq3(N_HEADS, HEAD_DIM), seq3(N_HEADS, LANES),
                  seq3(1, KV_WIDTH), seq3(1, KV_WIDTH), seq3(n_buf, KV_WIDTH), seq3(N_HEADS, HEAD_DIM),
                  seq3(1, gw)],
        out_specs=(seq3(N_HEADS, HEAD_DIM), seq3(n_buf, KV_WIDTH)),
        scratch_shapes=[pltpu.VMEM((N_KV_HEADS * n_sel, SEL_BLOCK, KV_WIDTH), F32), pltpu.SemaphoreType.DMA((1,))],
    )
    idx_flat = idx[:, :N_KV_HEADS, :n_sel].reshape(-1)
    return pl.pallas_call(
        functools.partial(_nsa_sample_sel_kernel, n_pages=n_pages, n_sel=n_sel, ns_past=past_len // SEL_BLOCK),
        out_shape=(jax.ShapeDtypeStruct((bd, N_HEADS, HEAD_DIM), F32),
                   jax.ShapeDtypeStruct((bd, n_buf, KV_WIDTH), F32)),
        grid_spec=grid_spec,
        compiler_params=_cparams(("arbitrary",)),
        name="nsa_sample_sel",
    )(page_table.reshape(-1), idx_flat, pool_s, qr_seq, idx, kvs_new.reshape(bd, 1, KV_WIDTH),
      kvw_new.reshape(bd, 1, KV_WIDTH), win_buf, o_c, gates.reshape(bd, 1, gw))


def _tail_kernel(x_ref, o_ref, y_ref, ga_ref, gb_ref, wa_ref, wb_ref, wo_ref, g1_ref, b1_ref, wr_ref, rb_ref,
                 cin_ref, x1_ref, idx_ref, w_ref, pos_ref, cnt_ref, carry_s, *, alpha):
    i = pl.program_id(0)
    tm = x_ref.shape[0]
    ne = wr_ref.shape[1]

    @pl.when(i == 0)
    def _():
        carry_s[...] = cin_ref[...]

    merged = (_sigmoid(ga_ref[...]) * _dot(o_ref[...], wa_ref[...])
              + _sigmoid(gb_ref[...]) * _dot(y_ref[...], wb_ref[...]))
    mix = _dot(merged.astype(BF16), wo_ref[...])
    x1 = _layer_norm(alpha * x_ref[...] + mix, g1_ref[...], b1_ref[...])
    x1_ref[...] = x1

    scores = _sigmoid(_dot(x1.astype(BF16), wr_ref[...]))
    biased = scores + rb_ref[...]
    lane = lax.broadcasted_iota(I32, (1, ne), 1).astype(F32)
    out_lane = lax.broadcasted_iota(I32, (1, idx_ref.shape[1]), 1)
    hits = jnp.zeros((tm, ne), F32)
    picks, chosen = [], []
    for _ in range(TOP_K):
        m = jnp.max(biased, axis=-1, keepdims=True)
        idx = jnp.min(jnp.where(biased == m, lane, float(ne)), axis=-1, keepdims=True)
        pick = lane == idx
        picks.append(pick)
        chosen.append(jnp.sum(jnp.where(pick, scores, 0.0), axis=-1, keepdims=True))
        hits = jnp.where(pick, 1.0, hits)
        biased = jnp.where(pick, -jnp.inf, biased)
    total = chosen[0]
    for c in chosen[1:]:
        total = total + c

    before = lax.broadcasted_iota(I32, (tm, tm), 1) < lax.broadcasted_iota(I32, (tm, tm), 0)
    prefix = _dot(jnp.where(before, 1.0, 0.0).astype(BF16), hits.astype(BF16)) + carry_s[...]
    idx_out = jnp.zeros(idx_ref.shape, I32)
    w_out = jnp.zeros(w_ref.shape, F32)
    pos_out = jnp.zeros(pos_ref.shape, I32)
    for k in range(TOP_K):
        e_k = jnp.sum(jnp.where(picks[k], lane, 0.0), axis=-1, keepdims=True).astype(I32)
        p_k = jnp.sum(jnp.where(picks[k], prefix, 0.0), axis=-1, keepdims=True).astype(I32)
        idx_out = jnp.where(out_lane == k, e_k, idx_out)
        pos_out = jnp.where(out_lane == k, p_k, pos_out)
        w_out = jnp.where(out_lane == k, chosen[k] / total * ROUTED_SCALE, w_out)
    idx_ref[...] = idx_out
    w_ref[...] = w_out
    pos_ref[...] = pos_out
    carry_s[...] = carry_s[...] + jnp.sum(hits, axis=0, keepdims=True)
    cnt_ref[...] = carry_s[...]


def _tail_weights(p):
    row = lambda v: v.reshape(1, -1).astype(F32)
    return (p['w_branch_attn'].astype(BF16), p['w_branch_rnn'].astype(BF16), p['w_out'].astype(BF16),
            row(p['ln1_g']), row(p['ln1_b']), p['w_router'].astype(BF16), row(p['router_bias']))


def _tail(x2d, o_attn, y_rnn, g_a, g_b, tw, counts_in, alpha, tm):
    m, d = x2d.shape
    ne = tw[5].shape[1]
    row = lambda w: pl.BlockSpec((tm, w), lambda i: (i, 0))
    small = jax.ShapeDtypeStruct((m, 8), I32)
    return pl.pallas_call(
        functools.partial(_tail_kernel, alpha=alpha),
        out_shape=(jax.ShapeDtypeStruct((m, d), F32), small, jax.ShapeDtypeStruct((m, 8), F32), small,
                   jax.ShapeDtypeStruct((1, ne), F32)),
        grid=(m // tm,),
        in_specs=[row(d), row(o_attn.shape[1]), row(y_rnn.shape[1]), row(d), row(d)]
                 + [_full(w.shape) for w in tw] + [_full((1, ne))],
        out_specs=(row(d), row(8), row(8), row(8), _full((1, ne))),
        scratch_shapes=[pltpu.VMEM((1, ne), F32)],
        compiler_params=_cparams(("arbitrary",)),
        name="tail",
    )(x2d, o_attn, y_rnn, g_a, g_b, *tw, counts_in)


def _silu(x):
    return x * _sigmoid(x)


def _experts_kernel(be_ref, nu_ref, rt_ref, x_ref, wg_ref, wu_ref, wd_ref, y_ref, xbuf, wgb, wub, wdb, sem):
    r = pl.program_id(0)
    n_used = nu_ref[0]
    rb = xbuf.shape[1]

    def row_copy(blk, slot, j):
        return pltpu.make_async_copy(x_ref.at[pl.ds(rt_ref[blk * rb + j], 1)], xbuf.at[slot, pl.ds(j, 1)],
                                     sem.at[slot])

    @pl.when(r == 0)
    def _():
        _row_gather(rb, lambda c: c.start(), lambda j: row_copy(0, 0, j))

    @pl.when(r + 1 < n_used)
    def _():
        _row_gather(rb, lambda c: c.start(), lambda j: row_copy(r + 1, (r + 1) % 2, j))

    @pl.when(r < n_used)
    def _():
        slot = r % 2
        _row_gather(rb, lambda c: c.wait(), lambda j: row_copy(r, slot, j))

        @pl.when((r == 0) | (be_ref[r] != be_ref[jnp.maximum(r - 1, 0)]))
        def _():
            wgb[...] = wg_ref[0].astype(BF16)
            wub[...] = wu_ref[0].astype(BF16)
            wdb[...] = wd_ref[0].astype(BF16)

        x = xbuf[slot].astype(BF16)
        hid = _silu(_dot(x, wgb[...])) * _dot(x, wub[...])
        y_ref[...] = _dot(hid.astype(BF16), wdb[...])

    @pl.when(r >= n_used)
    def _():
        y_ref[...] = jnp.zeros(y_ref.shape, F32)


def _experts(x1_all, row_tok, blk_expert, n_used, w_gate, w_up, w_down):
    n_blocks = blk_expert.shape[0]
    d = x1_all.shape[1]
    de = w_gate.shape[2]
    rb = EXPERT_ROWS
    grid_spec = pltpu.PrefetchScalarGridSpec(
        num_scalar_prefetch=3,
        grid=(n_blocks,),
        in_specs=[pl.BlockSpec(memory_space=pl.ANY),
                  pl.BlockSpec((1, d, de), lambda r, be, nu, rt: (be[r], 0, 0)),
                  pl.BlockSpec((1, d, de), lambda r, be, nu, rt: (be[r], 0, 0)),
                  pl.BlockSpec((1, de, d), lambda r, be, nu, rt: (be[r], 0, 0))],
        out_specs=pl.BlockSpec((rb, d), lambda r, be, nu, rt: (r, 0)),
        scratch_shapes=[pltpu.VMEM((2, rb, d), F32), pltpu.VMEM((d, de), BF16), pltpu.VMEM((d, de), BF16),
                        pltpu.VMEM((de, d), BF16), pltpu.SemaphoreType.DMA((2,))],
    )
    return pl.pallas_call(
        _experts_kernel,
        out_shape=jax.ShapeDtypeStruct((n_blocks * rb, d), F32),
        grid_spec=grid_spec,
        compiler_params=_cparams(("arbitrary",)),
        name="experts",
    )(blk_expert, n_used, row_tok, x1_all, w_gate, w_up, w_down)


def _combine_kernel(dest_ref, x1_ref, w_ref, y_ref, wsg_ref, wsu_ref, wsd_ref, g2_ref, b2_ref, out_ref,
                    ybuf, sem, *, tok_off, alpha):
    i = pl.program_id(0)
    n = pl.num_programs(0)
    tc = x1_ref.shape[0]

    def row_copy(tile, slot, k, t):
        src = dest_ref[(tok_off + tile * tc + t) * TOP_K + k]
        return pltpu.make_async_copy(y_ref.at[pl.ds(src, 1)], ybuf.at[slot, k, pl.ds(t, 1)], sem.at[slot])

    def gather(tile, slot, go):
        for k in range(TOP_K):
            _row_gather(tc, go, lambda t: row_copy(tile, slot, k, t))

    @pl.when(i == 0)
    def _():
        gather(0, 0, lambda c: c.start())

    @pl.when(i + 1 < n)
    def _():
        gather(i + 1, (i + 1) % 2, lambda c: c.start())

    slot = i % 2
    gather(i, slot, lambda c: c.wait())
    x1 = x1_ref[...]
    w = w_ref[...]
    routed = w[:, 0:1] * ybuf[slot, 0]
    for k in range(1, TOP_K):
        routed = routed + w[:, k:k + 1] * ybuf[slot, k]
    xb = x1.astype(BF16)
    shared = _dot((_silu(_dot(xb, wsg_ref[...])) * _dot(xb, wsu_ref[...])).astype(BF16), wsd_ref[...])
    out_ref[...] = _layer_norm(alpha * x1 + (routed + shared), g2_ref[...], b2_ref[...])


def _combine_weights(p):
    row = lambda v: v.reshape(1, -1).astype(F32)
    return (p['w_sh_gate'].astype(BF16), p['w_sh_up'].astype(BF16), p['w_sh_down'].astype(BF16),
            row(p['ln2_g']), row(p['ln2_b']))


def _combine(dest, x1, w, y_rows, cw, tok_off, alpha, tc):
    m, d = x1.shape
    row = lambda width: pl.BlockSpec((tc, width), lambda i, ds: (i, 0))
    grid_spec = pltpu.PrefetchScalarGridSpec(
        num_scalar_prefetch=1,
        grid=(m // tc,),
        in_specs=[row(d), row(w.shape[1]), pl.BlockSpec(memory_space=pl.ANY)]
                 + [pl.BlockSpec(a.shape, lambda i, ds: (0, 0)) for a in cw],
        out_specs=row(d),
        scratch_shapes=[pltpu.VMEM((2, TOP_K, tc, d), F32), pltpu.SemaphoreType.DMA((2,))],
    )
    return pl.pallas_call(
        functools.partial(_combine_kernel, tok_off=tok_off, alpha=alpha),
        out_shape=jax.ShapeDtypeStruct((m, d), F32),
        grid_spec=grid_spec,
        compiler_params=_cparams(("arbitrary",)),
        name="combine",
    )(dest, x1, w, y_rows, *cw)


def _route(idx, pos, counts, n_tok):
    ne = counts.shape[0]
    rb = EXPERT_ROWS
    n_blocks = -(-n_tok * TOP_K // rb) + ne
    padded = (counts + rb - 1) // rb * rb
    end_pad = jnp.cumsum(padded)
    start_pad = end_pad - padded
    dest = start_pad[idx] + pos
    tok = jnp.broadcast_to(jnp.arange(n_tok, dtype=I32)[:, None], dest.shape)
    row_tok = jnp.zeros((n_blocks * rb,), I32).at[dest.reshape(-1)].set(tok.reshape(-1))
    blk_expert = jnp.minimum(jnp.searchsorted(end_pad, jnp.arange(n_blocks, dtype=I32) * rb, side='right'),
                             ne - 1).astype(I32)
    n_used = (end_pad[-1] // rb).astype(I32).reshape(1)
    return dest.reshape(-1).astype(I32), row_tok, blk_expert, n_used


def _layer(xp, xs, caches, page_table, p, depth):
    b, t, d = xp.shape
    bd, s, _ = xs.shape
    assert s == 1, "sample group is one new token per sequence"
    pool_c, pool_s, win_buf, state_conv, state_rnn = caches
    past_len = page_table.shape[1] * PAGE_SIZE
    alpha = (2.0 * depth) ** 0.25
    kv6 = lambda a, n, rows: a.reshape(n, rows, 2, N_KV_HEADS, HEAD_DIM)

    wparts = _split_w_in(p['w_in'])
    cw = _compress_weights(p)
    rw = _rglru_weights(p)
    tw = _tail_weights(p)
    mw = _combine_weights(p)

    pos_p = jnp.tile(jnp.arange(t, dtype=I32), b)
    (q, qr, kvc, kvs, kvw, ksh, vsh, kwh, vwh, gates, u_rnn, u_gate, g_a, g_b) = _proj(
        xp.reshape(b * t, d), pos_p, wparts, 256)
    kc, vc = _compress_prompt(kvc, b, t, cw)
    o_attn = _nsa_prompt(q, qr, kc, vc, ksh, vsh, kwh, vwh, gates, b, t)
    y_rnn, h_p = _rglru_prompt(u_rnn, u_gate, rw, b, t, 256)
    ne = p['w_router'].shape[1]
    x1_p, idx_p, w_p, pos_r_p, counts_p = _tail(xp.reshape(b * t, d), o_attn, y_rnn.reshape(b * t, -1), g_a, g_b,
                                                tw, jnp.zeros((1, ne), F32), alpha, 256)
    wn = min(WINDOW, t)
    outs_p = (kv6(kvc, b, t), kv6(kvs, b, t), kv6(kvw, b, t)[:, t - wn:],
              u_rnn.reshape(b, t, -1)[:, t - (CONV_W - 1):], h_p.reshape(b, -1))

    pos_s = jnp.full((bd,), past_len, I32)
    (q, qr, kvc_s, kvs_s, kvw_s, _, _, _, _, gates_s, u_rnn_s, u_gate_s, g_a_s, g_b_s) = _proj(
        xs.reshape(bd, d), pos_s, wparts, bd)
    o_c, sel_idx = _nsa_sample_cmp(page_table, pool_c.reshape(pool_c.shape[0], PAGE_SIZE, KV_WIDTH),
                                   q.transpose(1, 0, 2), cw, past_len)
    o_s, win_new = _nsa_sample_sel(page_table, sel_idx, pool_s.reshape(pool_s.shape[0], PAGE_SIZE, KV_WIDTH),
                                   qr.transpose(1, 0, 2), kvs_s, kvw_s,
                                   win_buf.reshape(bd, win_buf.shape[1], KV_WIDTH), o_c, gates_s, past_len)
    y_rnn_s, h_s = _rglru_step(u_rnn_s, u_gate_s, state_conv.transpose(1, 0, 2), state_rnn, rw)
    x1_s, idx_s, w_s, pos_r_s, counts = _tail(xs.reshape(bd, d), o_s.reshape(bd, Q_WIDTH).astype(BF16), y_rnn_s,
                                              g_a_s, g_b_s, tw, counts_p, alpha, bd)
    conv_s = jnp.concatenate([state_conv[:, 1:], u_rnn_s[:, None, :]], axis=1)
    outs_s = (kv6(kvc_s, bd, 1), kv6(kvs_s, bd, 1), kv6(win_new, bd, win_new.shape[1]), conv_s, h_s)

    n_tok = b * t + bd
    idx_all = jnp.concatenate([idx_p, idx_s], axis=0)[:, :TOP_K]
    pos_all = jnp.concatenate([pos_r_p, pos_r_s], axis=0)[:, :TOP_K]
    dest, row_tok, blk_expert, n_used = _route(idx_all, pos_all, counts.reshape(-1).astype(I32), n_tok)
    x1_all = jnp.concatenate([x1_p, x1_s], axis=0)
    y_rows = _experts(x1_all, row_tok, blk_expert, n_used, p['w_exp_gate'], p['w_exp_up'], p['w_exp_down'])
    yp = _combine(dest, x1_p, w_p, y_rows, mw, 0, alpha, 128)
    ys = _combine(dest, x1_s, w_s, y_rows, mw, b * t, alpha, bd)
    return yp.reshape(b, t, d), ys.reshape(bd, s, d), outs_p, outs_s


def kernel(x_prompt, x_sample, cache_cmp_kv, cache_sel_kv, cache_win_kv, state_conv, state_rnn, page_table,
           w_in, conv_w, conv_b, w_rg_a, b_rg_a, w_rg_i, b_rg_i, lru_lambda, cmp_pos_k, cmp_pos_v,
           w_cmp_k1, w_cmp_k2, w_cmp_v1, w_cmp_v2, w_branch_attn, w_branch_rnn, w_out, ln1_g, ln1_b,
           w_router, router_bias, w_exp_gate, w_exp_up, w_exp_down, w_sh_gate, w_sh_up, w_sh_down, ln2_g, ln2_b):
    weights = dict(w_in=w_in, conv_w=conv_w, conv_b=conv_b, w_rg_a=w_rg_a, b_rg_a=b_rg_a, w_rg_i=w_rg_i,
                   b_rg_i=b_rg_i, lru_lambda=lru_lambda, cmp_pos_k=cmp_pos_k, cmp_pos_v=cmp_pos_v,
                   w_cmp_k1=w_cmp_k1, w_cmp_k2=w_cmp_k2, w_cmp_v1=w_cmp_v1, w_cmp_v2=w_cmp_v2,
                   w_branch_attn=w_branch_attn, w_branch_rnn=w_branch_rnn, w_out=w_out, ln1_g=ln1_g, ln1_b=ln1_b,
                   w_router=w_router, router_bias=router_bias, w_exp_gate=w_exp_gate, w_exp_up=w_exp_up,
                   w_exp_down=w_exp_down, w_sh_gate=w_sh_gate, w_sh_up=w_sh_up, w_sh_down=w_sh_down,
                   ln2_g=ln2_g, ln2_b=ln2_b)
    depth = w_in.shape[0]
    xp, xs = x_prompt, x_sample
    per_layer_p, per_layer_s = [], []
    for l in range(depth):
        p = {k: v[l] for k, v in weights.items()}
        caches = (cache_cmp_kv[l], cache_sel_kv[l], cache_win_kv[l], state_conv[l], state_rnn[l])
        xp, xs, outs_p, outs_s = _layer(xp, xs, caches, page_table, p, depth)
        per_layer_p.append(outs_p)
        per_layer_s.append(outs_s)
    stack = lambda outs, i: jnp.stack([o[i] for o in outs])
    return (xp, xs, stack(per_layer_p, 0), stack(per_layer_s, 0), stack(per_layer_p, 1), stack(per_layer_s, 1),
            stack(per_layer_p, 2), stack(per_layer_s, 2), stack(per_layer_p, 3), stack(per_layer_s, 3),
            stack(per_layer_p, 4), stack(per_layer_s, 4))
```

```python
import functools

import numpy as np
import jax
import jax.numpy as jnp
from jax import lax
from jax.experimental import pallas as pl
from jax.experimental.pallas import tpu as pltpu

F32 = jnp.float32
BF16 = jnp.bfloat16
I32 = jnp.int32

N_HEADS = 8
HEAD_DIM = 64
N_KV_HEADS = 2
GROUP = N_HEADS // N_KV_HEADS
ROT_DIM = HEAD_DIM // 4
ROPE_THETA = 500000.0
CMP_BLOCK = 32
CMP_STRIDE = 16
CMP_HIDDEN = 64
SEL_BLOCK = 64
N_SEL = 16
SEL_FORCE = 1.0e6
WINDOW = 512
Q_BLOCK = 128
RNN_BLOCKS = 8
CONV_W = 4
RG_C = 8.0
TOP_K = 6
ROUTED_SCALE = 2.5
LN_EPS = 1e-5
PAGE_SIZE = 128

Q_WIDTH = N_HEADS * HEAD_DIM
KV_WIDTH = 2 * N_KV_HEADS * HEAD_DIM
N_GROUPS = 2 * N_KV_HEADS
CHUNK_W = CMP_STRIDE * KV_WIDTH
LANES = 128
NEG = -1.0e30
VMEM_LIMIT = 56 * 1024 * 1024

EXPERT_ROWS = 256
SEL_TK = 512


def _cparams(sem):
    return pltpu.CompilerParams(dimension_semantics=sem, vmem_limit_bytes=VMEM_LIMIT)


def _full(shape):
    n = len(shape)
    return pl.BlockSpec(shape, lambda *a: (0,) * n)


def _dot(a, b):
    return jnp.dot(a, b, preferred_element_type=F32)


def _dot_nt(a, b):
    return lax.dot_general(a, b, (((1,), (1,)), ((), ())), preferred_element_type=F32)


def _gelu(x):
    return 0.5 * x * (1.0 + jnp.tanh(np.sqrt(2.0 / np.pi) * (x + 0.044715 * (x * x * x))))


def _sigmoid(x):
    return 1.0 / (1.0 + jnp.exp(-x))


def _layer_norm(z, g, b):
    mu = jnp.mean(z, axis=-1, keepdims=True)
    d = z - mu
    var = jnp.mean(d * d, axis=-1, keepdims=True)
    return d * lax.rsqrt(var + LN_EPS) * g + b


def _rope_tables(pos):
    half = ROT_DIM // 2
    inv = ROPE_THETA ** (-(jnp.arange(half, dtype=F32) * (2.0 / ROT_DIM)))
    ang = pos.astype(F32)[:, None] * inv[None, :]
    cos, sin = jnp.cos(ang), jnp.sin(ang)
    m = pos.shape[0]
    one = jnp.ones((m, HEAD_DIM - ROT_DIM), F32)
    zero = jnp.zeros((m, HEAD_DIM - ROT_DIM), F32)
    zh = jnp.zeros((m, half), F32)
    c = jnp.concatenate([cos, cos, one], axis=1)
    s1 = jnp.concatenate([-sin, zh, zero], axis=1)
    s2 = jnp.concatenate([zh, sin, zero], axis=1)
    rep = LANES // HEAD_DIM
    return jnp.tile(c, (1, rep)), jnp.tile(s1, (1, rep)), jnp.tile(s2, (1, rep))


def _rope128(x, c, s1, s2):
    half = ROT_DIM // 2
    return x * c + pltpu.roll(x, LANES - half, 1) * s1 + pltpu.roll(x, half, 1) * s2


def _proj_kernel(x_ref, c_ref, s1_ref, s2_ref, wq_ref, wkv_ref, wg_ref, ww_ref, *out_refs, transposed):
    if transposed:
        (q_ref, qr_ref, kvc_ref, kvct_ref, kvst_ref, kvwt_ref, ksh_ref, vsh_ref, kwh_ref, vwh_ref,
         gate_ref, urnn_ref, ugate_ref, ga_ref, gb_ref) = out_refs
        kvs_ref = kvw_ref = None
    else:
        (q_ref, qr_ref, kvc_ref, kvs_ref, kvw_ref, ksh_ref, vsh_ref, kwh_ref, vwh_ref,
         gate_ref, urnn_ref, ugate_ref, ga_ref, gb_ref) = out_refs
        kvct_ref = kvst_ref = kvwt_ref = None
    xb = x_ref[...].astype(BF16)
    c, s1, s2 = c_ref[...], s1_ref[...], s2_ref[...]
    scale = HEAD_DIM ** -0.5
    q = _dot(xb, wq_ref[...]) * scale
    for j in range(Q_WIDTH // LANES):
        ch = q[:, LANES * j:LANES * (j + 1)]
        rot = _rope128(ch, c, s1, s2)
        for hh in range(LANES // HEAD_DIM):
            head = j * (LANES // HEAD_DIM) + hh
            q_ref[head] = ch[:, HEAD_DIM * hh:HEAD_DIM * (hh + 1)].astype(BF16)
            qr_ref[head] = rot[:, HEAD_DIM * hh:HEAD_DIM * (hh + 1)].astype(BF16)
    kv = _dot(xb, wkv_ref[...])
    kvc_ref[...] = kv[:, :KV_WIDTH]
    if transposed:
        kvct_ref[0] = kv[:, :KV_WIDTH].T
    for base, full_ref, t_ref, kh_ref, vh_ref in ((KV_WIDTH, kvs_ref, kvst_ref, ksh_ref, vsh_ref),
                                                  (2 * KV_WIDTH, kvw_ref, kvwt_ref, kwh_ref, vwh_ref)):
        keys = _rope128(kv[:, base:base + LANES], c, s1, s2)
        vals = kv[:, base + LANES:base + 2 * LANES]
        if transposed:
            t_ref[0, :LANES, :] = keys.T
            t_ref[0, LANES:, :] = vals.T
        else:
            full_ref[:, :LANES] = keys
            full_ref[:, LANES:] = vals
        for h in range(N_KV_HEADS):
            kh_ref[h] = keys[:, HEAD_DIM * h:HEAD_DIM * (h + 1)].astype(BF16)
            vh_ref[h] = vals[:, HEAD_DIM * h:HEAD_DIM * (h + 1)].astype(BF16)
    gate_ref[...] = _sigmoid(_dot(xb, wg_ref[...]))
    d = urnn_ref.shape[-1]
    for k, ref in enumerate((urnn_ref, ugate_ref, ga_ref, gb_ref)):
        ref[...] = _dot(xb, ww_ref[:, d * k:d * (k + 1)])


def _split_w_in(w_in):
    d = w_in.shape[0]
    o = 0
    wq = w_in[:, o:o + Q_WIDTH]; o += Q_WIDTH
    wkv = w_in[:, o:o + 3 * KV_WIDTH]; o += 3 * KV_WIDTH
    wg = w_in[:, o:o + 3 * N_HEADS]; o += 3 * N_HEADS
    ww = w_in[:, o:]
    per = 3 * GROUP
    wg2 = jnp.zeros((d, N_KV_HEADS * LANES), w_in.dtype)
    for h in range(N_KV_HEADS):
        wg2 = wg2.at[:, h * LANES:h * LANES + per].set(wg[:, h * per:(h + 1) * per])
    return wq.astype(BF16), wkv.astype(BF16), wg2.astype(BF16), ww.astype(BF16)


def _proj(x2d, pos, wparts, tm, seq_len=None):
    m, d = x2d.shape
    wq, wkv, wg, ww = wparts
    c, s1, s2 = _rope_tables(pos)
    dw = ww.shape[1] // 4
    row = lambda w: pl.BlockSpec((tm, w), lambda i: (i, 0))
    hm = lambda n: pl.BlockSpec((n, tm, HEAD_DIM), lambda i: (0, i, 0))
    rows_f32 = jax.ShapeDtypeStruct((m, KV_WIDTH), F32)
    hm_q = jax.ShapeDtypeStruct((N_HEADS, m, HEAD_DIM), BF16)
    hm_kv = jax.ShapeDtypeStruct((N_KV_HEADS, m, HEAD_DIM), BF16)
    wide = jax.ShapeDtypeStruct((m, dw), F32)
    if seq_len is None:
        kv_shapes = (rows_f32, rows_f32, rows_f32)
        kv_specs = (row(KV_WIDTH),) * 3
    else:
        nt = seq_len // tm
        kv_t = jax.ShapeDtypeStruct((m // seq_len, KV_WIDTH, seq_len), F32)
        t_spec = pl.BlockSpec((1, KV_WIDTH, tm), lambda i: (i // nt, 0, i % nt))
        kv_shapes = (rows_f32, kv_t, kv_t, kv_t)
        kv_specs = (row(KV_WIDTH), t_spec, t_spec, t_spec)
    out_shape = (hm_q, hm_q) + kv_shapes + (hm_kv,) * 4 + (
        jax.ShapeDtypeStruct((m, N_KV_HEADS * LANES), F32), wide, wide, wide, wide)
    out_specs = (hm(N_HEADS), hm(N_HEADS)) + kv_specs + (hm(N_KV_HEADS),) * 4 + (
        row(N_KV_HEADS * LANES), row(dw), row(dw), row(dw), row(dw))
    return pl.pallas_call(
        functools.partial(_proj_kernel, transposed=seq_len is not None),
        out_shape=out_shape,
        grid=(m // tm,),
        in_specs=[row(d), row(LANES), row(LANES), row(LANES),
                  _full(wq.shape), _full(wkv.shape), _full(wg.shape), _full(ww.shape)],
        out_specs=out_specs,
        compiler_params=_cparams(("parallel",)),
        name="proj",
    )(x2d, c, s1, s2, wq, wkv, wg, ww)


def _compress_weights(p):
    parts = CMP_BLOCK // CMP_STRIDE
    gw = N_GROUPS * CMP_HIDDEN
    w1 = jnp.zeros((CMP_STRIDE, N_GROUPS, HEAD_DIM, parts, N_GROUPS, CMP_HIDDEN), F32)
    pe = jnp.zeros((8, CMP_STRIDE, N_GROUPS, HEAD_DIM), F32)
    w2 = jnp.zeros((N_GROUPS, CMP_HIDDEN, N_GROUPS, HEAD_DIM), F32)
    for g in range(N_GROUPS):
        kv = g // N_KV_HEADS
        w1_src = (p['w_cmp_k1'], p['w_cmp_v1'])[kv].reshape(parts, CMP_STRIDE, HEAD_DIM, CMP_HIDDEN)
        pe_src = (p['cmp_pos_k'], p['cmp_pos_v'])[kv].reshape(parts, CMP_STRIDE, HEAD_DIM)
        w2_src = (p['w_cmp_k2'], p['w_cmp_v2'])[kv]
        for m in range(parts):
            w1 = w1.at[:, g, :, m, g, :].set(w1_src[m])
            pe = pe.at[m, :, g, :].set(pe_src[m])
        w2 = w2.at[g, :, g, :].set(w2_src)
    return (w1.reshape(CHUNK_W, parts * gw).astype(BF16), pe.reshape(8, CHUNK_W).astype(BF16),
            w2.reshape(gw, N_GROUPS * HEAD_DIM).astype(BF16))


def _compress_rows(c, pec, w2):
    gw = N_GROUPS * CMP_HIDDEN
    n = c.shape[0]
    const = pec[0:1, :gw] + pec[1:2, gw:]
    hid = c[:, :gw] + pltpu.roll(c[:, gw:], n - 1, 0) + const
    return _dot(_gelu(hid).astype(BF16), w2)


def _compress_prompt_kernel(ch_ref, w1_ref, pe_ref, w2_ref, kc_ref, vc_ref):
    c = _dot(ch_ref[0].astype(BF16), w1_ref[...])
    pec = _dot(pe_ref[...], w1_ref[...])
    out = _compress_rows(c, pec, w2_ref[...])
    for h in range(N_KV_HEADS):
        kc_ref[0, h] = out[:, HEAD_DIM * h:HEAD_DIM * (h + 1)].astype(BF16)
        vc_ref[0, h] = out[:, HEAD_DIM * (N_KV_HEADS + h):HEAD_DIM * (N_KV_HEADS + h + 1)].astype(BF16)


def _compress_prompt(kvc, b, t, cw):
    w1, pe, w2 = cw
    nch = t // CMP_STRIDE
    chunks = kvc.reshape(b, nch, CHUNK_W)
    out = jax.ShapeDtypeStruct((b, N_KV_HEADS, nch, HEAD_DIM), BF16)
    ospec = pl.BlockSpec((1, N_KV_HEADS, nch, HEAD_DIM), lambda i: (i, 0, 0, 0))
    return pl.pallas_call(
        _compress_prompt_kernel,
        out_shape=(out, out),
        grid=(b,),
        in_specs=[pl.BlockSpec((1, nch, CHUNK_W), lambda i: (i, 0, 0)),
                  _full(w1.shape), _full(pe.shape), _full(w2.shape)],
        out_specs=(ospec, ospec),
        compiler_params=_cparams(("parallel",)),
        name="compress_prompt",
    )(chunks, w1, pe, w2)


def _sel_map(n_cmp_pad, n_cmp, n_sel, n_sel_pad):
    ratio = SEL_BLOCK // CMP_STRIDE
    i = np.arange(n_cmp_pad)[:, None]
    j = np.arange(n_sel_pad)[None, :]
    d = i - ratio * j
    m = sum(((d + n >= 0) & (d + n < ratio)).astype(np.float32) for n in range(CMP_BLOCK // CMP_STRIDE))
    m = m * (i < n_cmp) * (j < n_sel)
    return jnp.asarray(m, BF16)


def _masked_softmax_rows(s, valid):
    s = jnp.where(valid, s, -jnp.inf)
    m = jnp.max(s, axis=-1, keepdims=True)
    m = jnp.where(m == -jnp.inf, 0.0, m)
    e = jnp.exp(s - m)
    return e / jnp.maximum(jnp.sum(e, axis=-1, keepdims=True), 1e-30)


def _top_blocks(score, blk, n):
    width = float(score.shape[-1])
    sel = jnp.zeros(score.shape, F32)
    for _ in range(n):
        m = jnp.max(score, axis=-1, keepdims=True)
        idx = jnp.min(jnp.where(score == m, blk, width), axis=-1, keepdims=True)
        pick = blk == idx
        sel = jnp.where(pick, 1.0, sel)
        score = jnp.where(pick, -jnp.inf, score)
    return sel


def _nsa_prompt_kernel(q_ref, qr_ref, kc_ref, vc_ref, ks_ref, vs_ref, kw_ref, vw_ref, g_ref, smap_ref,
                       o_ref, *, n_sel_blocks):
    qb = pl.program_id(2)
    s0 = qb * Q_BLOCK
    rows = GROUP * Q_BLOCK
    q = q_ref[...].reshape(rows, HEAD_DIM)
    qr = qr_ref[...].reshape(rows, HEAD_DIM)
    pos1 = s0 + lax.broadcasted_iota(I32, (Q_BLOCK, 1), 0)
    pos = s0 + (lax.broadcasted_iota(I32, (rows, 1), 0) & (Q_BLOCK - 1))

    kc = kc_ref[0, 0]
    ncp = kc.shape[0]
    c_end = lax.broadcasted_iota(I32, (1, ncp), 1) * CMP_STRIDE + (CMP_BLOCK - 1)
    p_c = _masked_softmax_rows(_dot_nt(q, kc), c_end <= pos).astype(BF16)
    o_c = _dot(p_c, vc_ref[0, 0])
    imp = _dot(p_c[0:Q_BLOCK], smap_ref[...])
    for g in range(1, GROUP):
        imp = imp + _dot(p_c[g * Q_BLOCK:(g + 1) * Q_BLOCK], smap_ref[...])

    blk_i = lax.broadcasted_iota(I32, (1, LANES), 1)
    blk = blk_i.astype(F32)
    cur = pos1 // SEL_BLOCK
    forced = (blk_i == 0) | (blk_i == cur) | (blk_i == cur - 1)
    score = jnp.where(blk_i > cur, -jnp.inf, jnp.where(forced, SEL_FORCE, imp))
    sel = _top_blocks(score, blk, min(N_SEL, n_sel_blocks))
    sel = jnp.where(blk_i <= cur, sel, 0.0).astype(BF16)

    per_tile = SEL_TK // SEL_BLOCK
    jrow = lax.broadcasted_iota(I32, (LANES, SEL_TK), 0)
    cblk = lax.broadcasted_iota(I32, (LANES, SEL_TK), 1) // SEL_BLOCK
    kcol = lax.broadcasted_iota(I32, (1, SEL_TK), 1)

    def sel_tile(kt, carry, causal):
        m_i, l_i, acc = carry
        start = pl.multiple_of(kt * SEL_TK, SEL_TK)
        k = ks_ref[0, pl.ds(start, SEL_TK), :]
        v = vs_ref[0, pl.ds(start, SEL_TK), :]
        s = _dot_nt(qr, k).reshape(GROUP, Q_BLOCK, SEL_TK)
        expand = jnp.where(jrow - kt * per_tile == cblk, 1.0, 0.0).astype(BF16)
        ok = _dot(sel, expand) > 0.5
        if causal:
            ok = ok & (start + kcol <= pos1)
        s = jnp.where(ok[None], s, NEG)
        m_new = jnp.maximum(m_i, jnp.max(s, axis=-1, keepdims=True))
        alpha = jnp.exp(m_i - m_new)
        p = jnp.exp(s - m_new)
        l_new = alpha * l_i + jnp.sum(p, axis=-1, keepdims=True)
        pv = _dot(p.reshape(rows, SEL_TK).astype(BF16), v).reshape(GROUP, Q_BLOCK, HEAD_DIM)
        return m_new, l_new, alpha * acc + pv

    init = (jnp.full((GROUP, Q_BLOCK, 1), NEG, F32), jnp.zeros((GROUP, Q_BLOCK, 1), F32),
            jnp.zeros((GROUP, Q_BLOCK, HEAD_DIM), F32))
    n_full = s0 // SEL_TK
    carry = lax.fori_loop(0, n_full, lambda kt, cr: sel_tile(kt, cr, False), init)
    _, l_s, acc_s = sel_tile(n_full, carry, True)
    o_s = (acc_s / l_s).reshape(rows, HEAD_DIM)

    wlen = WINDOW + Q_BLOCK
    wstart = pl.multiple_of(jnp.maximum(s0 - WINDOW, 0), Q_BLOCK)
    kpos = wstart + lax.broadcasted_iota(I32, (1, wlen), 1)
    diff = pos - kpos
    p_w = _masked_softmax_rows(_dot_nt(qr, kw_ref[0, pl.ds(wstart, wlen), :]), (diff >= 0) & (diff <= WINDOW))
    o_w = _dot(p_w.astype(BF16), vw_ref[0, pl.ds(wstart, wlen), :])

    gates = g_ref[...]
    outs = []
    for g in range(GROUP):
        r = slice(g * Q_BLOCK, (g + 1) * Q_BLOCK)
        outs.append(gates[:, 3 * g:3 * g + 1] * o_c[r] + gates[:, 3 * g + 1:3 * g + 2] * o_s[r]
                    + gates[:, 3 * g + 2:3 * g + 3] * o_w[r])
    o_ref[...] = jnp.concatenate(outs, axis=1).astype(o_ref.dtype)


def _nsa_prompt(q_hm, qr_hm, kc, vc, ksh, vsh, kwh, vwh, gates, b, t):
    nqb = t // Q_BLOCK
    n_sel_blocks = t // SEL_BLOCK
    ncp = kc.shape[2]
    assert N_SEL <= n_sel_blocks <= LANES and t % SEL_TK == 0 and t >= WINDOW + Q_BLOCK
    smap = _sel_map(ncp, ncp - 1, n_sel_blocks, LANES)
    qspec = pl.BlockSpec((GROUP, Q_BLOCK, HEAD_DIM), lambda bi, h, i: (h, bi * nqb + i, 0))
    cspec = pl.BlockSpec((1, 1, ncp, HEAD_DIM), lambda bi, h, i: (bi, h, 0, 0))
    kspec = pl.BlockSpec((1, t, HEAD_DIM), lambda bi, h, i: (h, bi, 0))
    return pl.pallas_call(
        functools.partial(_nsa_prompt_kernel, n_sel_blocks=n_sel_blocks),
        out_shape=jax.ShapeDtypeStruct((b * t, Q_WIDTH), BF16),
        grid=(b, N_KV_HEADS, nqb),
        in_specs=[qspec, qspec, cspec, cspec, kspec, kspec, kspec, kspec,
                  pl.BlockSpec((Q_BLOCK, LANES), lambda bi, h, i: (bi * nqb + i, h)),
                  _full(smap.shape)],
        out_specs=pl.BlockSpec((Q_BLOCK, GROUP * HEAD_DIM), lambda bi, h, i: (bi * nqb + i, h)),
        compiler_params=_cparams(("parallel", "parallel", "arbitrary")),
        name="nsa_prompt",
    )(q_hm, qr_hm, kc, vc, ksh, vsh, kwh, vwh, gates, smap)


def _rglru_gates(xc, ug, wa_ref, ba, wi_ref, bi, lam):
    xcb = xc.astype(BF16)
    bw = xc.shape[-1] // RNN_BLOCKS
    ra = jnp.concatenate([_dot(xcb[:, bw * n:bw * (n + 1)], wa_ref[n]) for n in range(RNN_BLOCKS)], axis=1)
    ri = jnp.concatenate([_dot(xcb[:, bw * n:bw * (n + 1)], wi_ref[n]) for n in range(RNN_BLOCKS)], axis=1)
    r = _sigmoid(ra + ba)
    i = _sigmoid(ri + bi)
    softplus = jnp.maximum(-lam, 0.0) + jnp.log(1.0 + jnp.exp(-jnp.abs(lam)))
    log_a = -RG_C * r * softplus
    a = jnp.exp(log_a)
    b = jnp.sqrt(1.0 - jnp.exp(2.0 * log_a)) * (i * xc)
    return a, b, _gelu(ug)


def _rglru_prompt_kernel(u_ref, ug_ref, cw_ref, cb_ref, wa_ref, ba_ref, wi_ref, bi_ref, lam_ref,
                         y_ref, h_ref, up_s, a_s, b_s, h_s):
    tt = u_ref.shape[1]
    t = pl.program_id(1)
    halo = CONV_W - 1

    @pl.when(t == 0)
    def _():
        up_s[0:8, :] = jnp.zeros((8, up_s.shape[1]), F32)
        h_s[...] = jnp.zeros(h_s.shape, F32)

    @pl.when(t > 0)
    def _():
        up_s[8 - halo:8, :] = up_s[8 + tt - halo:8 + tt, :]

    up_s[8:8 + tt, :] = u_ref[0]
    xc = cb_ref[...] + up_s[pl.ds(8 - halo, tt), :] * cw_ref[0:1, :]
    for k in range(1, CONV_W):
        xc = xc + up_s[pl.ds(8 - halo + k, tt), :] * cw_ref[k:k + 1, :]
    a, b, gate = _rglru_gates(xc, ug_ref[0], wa_ref, ba_ref[...], wi_ref, bi_ref[...], lam_ref[...])
    a_s[...] = a
    b_s[...] = b

    def step(i, h):
        h = a_s[pl.ds(i, 1), :] * h + b_s[pl.ds(i, 1), :]
        b_s[pl.ds(i, 1), :] = h
        return h

    h = lax.fori_loop(0, tt, step, h_s[...], unroll=8)
    h_s[...] = h
    h_ref[0] = h
    y_ref[0] = (b_s[...] * gate).astype(y_ref.dtype)


def _rglru_weights(p):
    row = lambda v: v.reshape(1, -1).astype(F32)
    return (p['conv_w'].astype(F32), row(p['conv_b']), p['w_rg_a'].astype(BF16), row(p['b_rg_a']),
            p['w_rg_i'].astype(BF16), row(p['b_rg_i']), row(p['lru_lambda']))


def _rglru_prompt(u, ug, rw, b, t, tt):
    d = u.shape[1]
    nt = t // tt
    seq = pl.BlockSpec((1, tt, d), lambda bi, i: (bi, i, 0))
    return pl.pallas_call(
        _rglru_prompt_kernel,
        out_shape=(jax.ShapeDtypeStruct((b, t, d), BF16), jax.ShapeDtypeStruct((b, 1, d), F32)),
        grid=(b, nt),
        in_specs=[seq, seq] + [_full(w.shape) for w in rw],
        out_specs=(seq, pl.BlockSpec((1, 1, d), lambda bi, i: (bi, 0, 0))),
        scratch_shapes=[pltpu.VMEM((tt + 8, d), F32), pltpu.VMEM((tt, d), F32), pltpu.VMEM((tt, d), F32),
                        pltpu.VMEM((1, d), F32)],
        compiler_params=_cparams(("parallel", "arbitrary")),
        name="rglru_prompt",
    )(u.reshape(b, t, d), ug.reshape(b, t, d), *rw)


def _rglru_step_kernel(u_ref, ug_ref, cs_ref, h0_ref, cw_ref, cb_ref, wa_ref, ba_ref, wi_ref, bi_ref, lam_ref,
                       y_ref, h_ref):
    xc = cb_ref[...] + u_ref[...] * cw_ref[CONV_W - 1:CONV_W, :]
    for k in range(CONV_W - 1):
        xc = xc + cs_ref[k] * cw_ref[k:k + 1, :]
    a, b, gate = _rglru_gates(xc, ug_ref[...], wa_ref, ba_ref[...], wi_ref, bi_ref[...], lam_ref[...])
    h = a * h0_ref[...] + b
    h_ref[...] = h
    y_ref[...] = (h * gate).astype(y_ref.dtype)


def _rglru_step(u, ug, conv_state, h0, rw):
    n, d = u.shape
    args = (u, ug, conv_state, h0) + tuple(rw)
    return pl.pallas_call(
        _rglru_step_kernel,
        out_shape=(jax.ShapeDtypeStruct((n, d), BF16), jax.ShapeDtypeStruct((n, d), F32)),
        grid=(1,),
        in_specs=[_full(a.shape) for a in args],
        out_specs=(_full((n, d)), _full((n, d))),
        compiler_params=_cparams(("arbitrary",)),
        name="rglru_step",
    )(*args)


def _row_gather(rows, starts, copy_fn, unroll=8):
    def body(j, c):
        starts(copy_fn(j))
        return c
    lax.fori_loop(0, rows, body, 0, unroll=unroll)


def _nsa_sample_cmp_kernel(pt_ref, pool_ref, q_ref, w1_ref, pe_ref, w2_ref, smap_ref,
                           oc_ref, idx_ref, pbuf, rows_s, c_s, sem, *, n_seq, n_pages, q_pos, n_sel_blocks):
    b = pl.program_id(0)
    nb = n_seq
    n_slots, pg = pbuf.shape[0], pbuf.shape[1]
    groups = n_pages // pg
    ahead = n_slots - 1
    rows_pp = PAGE_SIZE // CMP_STRIDE

    def page_copy(seq, grp, slot, j):
        return pltpu.make_async_copy(pool_ref.at[pt_ref[seq * n_pages + grp * pg + j]], pbuf.at[slot, j],
                                     sem.at[slot])

    def issue(seq, grp, slot):
        _row_gather(pg, lambda c: c.start(), lambda j: page_copy(seq, grp, slot, j))

    @pl.when(b == 0)
    def _():
        for g in range(ahead):
            if g // groups < n_seq:
                issue(g // groups, g % groups, g % n_slots)

    for k in range(groups):
        gi = b * groups + k
        slot = lax.rem(gi, n_slots)
        nxt = k + ahead
        seq_off, grp_nxt = nxt // groups, nxt % groups

        @pl.when(b + seq_off < nb)
        def _():
            issue(b + seq_off, grp_nxt, lax.rem(gi + ahead, n_slots))

        _row_gather(pg, lambda c: c.wait(), lambda j: page_copy(b, k, slot, j))
        halves = KV_WIDTH // LANES
        for j in range(pg):
            for f in range(halves):
                rows_s[f, j * PAGE_SIZE:(j + 1) * PAGE_SIZE, :] = pbuf[slot, j, f * LANES:(f + 1) * LANES, :].T

        def chunk_rows(s):
            return jnp.concatenate([rows_s[f, pl.ds(s, pg * rows_pp, stride=CMP_STRIDE), :]
                                    for f in range(halves)], axis=1).astype(BF16)

        acc = _dot(chunk_rows(0), w1_ref[0])
        for s in range(1, CMP_STRIDE):
            acc = acc + _dot(chunk_rows(s), w1_ref[s])
        c_s[k * pg * rows_pp:(k + 1) * pg * rows_pp, :] = acc

    nch = n_pages * rows_pp
    pec = _dot(pe_ref[...], w1_ref[...].reshape(CHUNK_W, w1_ref.shape[2]))
    kv = _compress_rows(c_s[...], pec, w2_ref[...]).astype(BF16)

    q = q_ref[0]
    c_end = lax.broadcasted_iota(I32, (1, nch), 1) * CMP_STRIDE + (CMP_BLOCK - 1)
    row = lax.broadcasted_iota(I32, (N_HEADS, 1), 0)
    width = smap_ref.shape[1]
    blk_i = lax.broadcasted_iota(I32, (1, width), 1)
    cur = q_pos // SEL_BLOCK
    forced = (blk_i == 0) | (blk_i == cur) | (blk_i == cur - 1)
    o_c = jnp.zeros((N_HEADS, HEAD_DIM), F32)
    score = jnp.full((N_HEADS, width), -jnp.inf, F32)
    for h in range(N_KV_HEADS):
        kc = kv[:, HEAD_DIM * h:HEAD_DIM * (h + 1)]
        vc = kv[:, HEAD_DIM * (N_KV_HEADS + h):HEAD_DIM * (N_KV_HEADS + h + 1)]
        p = _masked_softmax_rows(_dot_nt(q, kc), c_end <= q_pos).astype(BF16)
        in_group = (row // GROUP) == h
        o_c = jnp.where(in_group, _dot(p, vc), o_c)
        imp = jnp.sum(jnp.where(in_group, _dot(p, smap_ref[...]), 0.0), axis=0, keepdims=True)
        sc = jnp.where(blk_i > cur, -jnp.inf, jnp.where(forced, SEL_FORCE, imp))
        score = jnp.where(row == h, sc, score)
    oc_ref[0] = o_c

    n = min(N_SEL, n_sel_blocks)
    blk = blk_i.astype(F32)
    lane = lax.broadcasted_iota(I32, (1, LANES), 1)
    idx_out = jnp.zeros((N_HEADS, LANES), I32)
    for r in range(n):
        m = jnp.max(score, axis=-1, keepdims=True)
        idx = jnp.min(jnp.where(score == m, blk, float(width)), axis=-1, keepdims=True)
        score = jnp.where(blk == idx, -jnp.inf, score)
        idx_out = jnp.where(lane == r, idx.astype(I32), idx_out)
    idx_ref[0] = idx_out


CMP_PAGE_GROUP = 32
CMP_GROUP_SLOTS = 3


def _pages_feature_major(pool):
    n_pool, rows = pool.shape[0], pool.shape[1]
    return jnp.transpose(pool, (0, 2, 3, 4, 1)).reshape(n_pool, KV_WIDTH, rows)


def _nsa_sample_cmp(page_table, pool_t, q_seq, cw, past_len):
    bd, n_pages = page_table.shape
    w1, pe, w2 = cw
    w1 = w1.reshape(CMP_STRIDE, KV_WIDTH, w1.shape[1])
    rows_pp = PAGE_SIZE // CMP_STRIDE
    nch = n_pages * rows_pp
    pg = min(CMP_PAGE_GROUP, n_pages)
    assert n_pages % pg == 0
    n_sel_blocks = past_len // SEL_BLOCK + 1
    width = -(-n_sel_blocks // LANES) * LANES
    smap = _sel_map(nch, nch - 1, n_sel_blocks, width)
    grid_spec = pltpu.PrefetchScalarGridSpec(
        num_scalar_prefetch=1,
        grid=(bd,),
        in_specs=[pl.BlockSpec(memory_space=pl.ANY),
                  pl.BlockSpec((1, N_HEADS, HEAD_DIM), lambda i, pt: (i, 0, 0)),
                  pl.BlockSpec(w1.shape, lambda i, pt: (0, 0, 0)),
                  pl.BlockSpec(pe.shape, lambda i, pt: (0, 0)),
                  pl.BlockSpec(w2.shape, lambda i, pt: (0, 0)),
                  pl.BlockSpec(smap.shape, lambda i, pt: (0, 0))],
        out_specs=(pl.BlockSpec((1, N_HEADS, HEAD_DIM), lambda i, pt: (i, 0, 0)),
                   pl.BlockSpec((1, N_HEADS, LANES), lambda i, pt: (i, 0, 0))),
        scratch_shapes=[pltpu.VMEM((CMP_GROUP_SLOTS, pg, KV_WIDTH, PAGE_SIZE), F32),
                        pltpu.VMEM((KV_WIDTH // LANES, pg * PAGE_SIZE, LANES), F32),
                        pltpu.VMEM((nch, w1.shape[2]), F32),
                        pltpu.SemaphoreType.DMA((CMP_GROUP_SLOTS,))],
    )
    return pl.pallas_call(
        functools.partial(_nsa_sample_cmp_kernel, n_seq=bd, n_pages=n_pages, q_pos=past_len,
                          n_sel_blocks=n_sel_blocks),
        out_shape=(jax.ShapeDtypeStruct((bd, N_HEADS, HEAD_DIM), F32),
                   jax.ShapeDtypeStruct((bd, N_HEADS, LANES), I32)),
        grid_spec=grid_spec,
        compiler_params=_cparams(("arbitrary",)),
        name="nsa_sample_cmp",
    )(page_table.reshape(-1), pool_t, q_seq, w1, pe, w2, smap)


def _nsa_sample_sel_kernel(pt_ref, idx_s_ref, pool_ref, qr_ref, idx_ref, kvs_ref, kvw_ref, win_ref, oc_ref, g_ref,
                           o_ref, wout_ref, sbuf, sem, *, n_seq, n_pages, n_sel, ns_past):
    b = pl.program_id(0)
    per_page = PAGE_SIZE // SEL_BLOCK
    nblk = N_KV_HEADS * n_sel
    nkeys = n_sel * PAGE_SIZE

    def slab_copy(seq, slot, j):
        blk, kv = j // 2, j % 2
        jp = jnp.minimum(idx_s_ref[seq * nblk + blk], ns_past - 1)
        page = pt_ref[seq * n_pages + jp // per_page]
        feat = pl.multiple_of((kv * N_KV_HEADS + blk // n_sel) * HEAD_DIM, HEAD_DIM)
        return pltpu.make_async_copy(pool_ref.at[page, pl.ds(feat, HEAD_DIM), :], sbuf.at[slot, blk, kv],
                                     sem.at[slot])

    @pl.when(b == 0)
    def _():
        _row_gather(2 * nblk, lambda c: c.start(), lambda j: slab_copy(0, 0, j))

    @pl.when(b + 1 < n_seq)
    def _():
        _row_gather(2 * nblk, lambda c: c.start(), lambda j: slab_copy(b + 1, (b + 1) % 2, j))

    slot = b % 2
    _row_gather(2 * nblk, lambda c: c.wait(), lambda j: slab_copy(b, slot, j))

    qr = qr_ref[0]
    qf = qr.astype(F32)
    row = lax.broadcasted_iota(I32, (N_HEADS, 1), 0)
    key = lax.broadcasted_iota(I32, (1, nkeys), 1)
    key_blk = ((key % PAGE_SIZE) // SEL_BLOCK).astype(F32)
    first = (key % SEL_BLOCK) == 0
    expand = jnp.where(lax.broadcasted_iota(I32, (LANES, nkeys), 0)
                       == lax.broadcasted_iota(I32, (LANES, nkeys), 1) // PAGE_SIZE, 1.0, 0.0).astype(BF16)
    chosen = idx_ref[0]
    in_page = (jnp.minimum(chosen, ns_past - 1) % per_page).astype(F32).astype(BF16)
    want_blk = _dot(in_page, expand)
    is_new = _dot(jnp.where(chosen >= ns_past, 1.0, 0.0).astype(BF16), expand) > 0.5
    kvs_new = kvs_ref[0].astype(BF16).astype(F32)
    kvw_new = kvw_ref[0].astype(BF16).astype(F32)
    o_s = jnp.zeros((N_HEADS, HEAD_DIM), F32)
    o_w = jnp.zeros((N_HEADS, HEAD_DIM), F32)
    for h in range(N_KV_HEADS):
        in_group = (row // GROUP) == h
        ksl = slice(HEAD_DIM * h, HEAD_DIM * (h + 1))
        vsl = slice(HEAD_DIM * (N_KV_HEADS + h), HEAD_DIM * (N_KV_HEADS + h + 1))
        k_t = jnp.concatenate([sbuf[slot, h * n_sel + n, 0] for n in range(n_sel)], axis=1).astype(BF16)
        v_t = jnp.concatenate([sbuf[slot, h * n_sel + n, 1] for n in range(n_sel)], axis=1).astype(BF16)
        in_blk = want_blk[h:h + 1] == key_blk
        newblk = is_new[h:h + 1]
        newkey = in_blk & newblk & first
        s_new = jnp.sum(qf * kvs_new[:, ksl], axis=-1, keepdims=True)
        s = jnp.where(newkey, s_new, _dot(qr, k_t))
        p = _masked_softmax_rows(s, in_blk & (jnp.logical_not(newblk) | first))
        p_new = jnp.sum(jnp.where(newkey, p, 0.0), axis=-1, keepdims=True).astype(BF16).astype(F32)
        o_h = _dot_nt(jnp.where(newkey, 0.0, p).astype(BF16), v_t) + p_new * kvs_new[:, vsl]
        o_s = jnp.where(in_group, o_h, o_s)
        s_buf = _dot(qr, win_ref[0, ksl, :].astype(BF16))
        s_cur = jnp.sum(qf * kvw_new[:, ksl], axis=-1, keepdims=True)
        m = jnp.maximum(jnp.max(s_buf, axis=-1, keepdims=True), s_cur)
        e_buf = jnp.exp(s_buf - m)
        e_cur = jnp.exp(s_cur - m)
        den = jnp.maximum(jnp.sum(e_buf, axis=-1, keepdims=True) + e_cur, 1e-30)
        o_h = (_dot_nt((e_buf / den).astype(BF16), win_ref[0, vsl, :].astype(BF16))
               + (e_cur / den).astype(BF16).astype(F32) * kvw_new[:, vsl])
        o_w = jnp.where(in_group, o_h, o_w)

    gw = g_ref.shape[-1]
    gl = lax.broadcasted_iota(I32, (N_HEADS, gw), 1)
    gbase = (row // GROUP) * LANES + (row % GROUP) * 3
    gates = jnp.broadcast_to(g_ref[0], (N_HEADS, gw))
    gate = lambda j: jnp.sum(jnp.where(gl == gbase + j, gates, 0.0), axis=-1, keepdims=True)
    o_ref[0] = gate(0) * oc_ref[0] + gate(1) * o_s + gate(2) * o_w

    n_chunks = win_ref.shape[2] // LANES
    new_col = jnp.broadcast_to(kvw_ref[0], (LANES, KV_WIDTH)).T
    last_lane = lax.broadcasted_iota(I32, (1, LANES), 1) == LANES - 1
    for c in range(n_chunks):
        cur = win_ref[0, :, c * LANES:(c + 1) * LANES]
        nxt = win_ref[0, :, (c + 1) * LANES:(c + 2) * LANES] if c + 1 < n_chunks else new_col
        wout_ref[0, :, c * LANES:(c + 1) * LANES] = jnp.where(last_lane, pltpu.roll(nxt, LANES - 1, 1),
                                                             pltpu.roll(cur, LANES - 1, 1))


def _nsa_sample_sel(page_table, idx, pool_t, qr_seq, kvs_new, kvw_new, win_t, o_c, gates, past_len):
    bd, n_pages = page_table.shape
    n_sel = min(N_SEL, past_len // SEL_BLOCK + 1)
    n_buf = win_t.shape[2]
    assert n_buf % LANES == 0
    gw = gates.shape[-1]
    seq3 = lambda s1, s2: pl.BlockSpec((1, s1, s2), lambda i, pt, ix: (i, 0, 0))
    grid_spec = pltpu.PrefetchScalarGridSpec(
        num_scalar_prefetch=2,
        grid=(bd,),
        in_specs=[pl.BlockSpec(memory_space=pl.ANY), seq3(N_HEADS, HEAD_DIM), seq3(N_HEADS, LANES),
                  seq3(1, KV_WIDTH), seq3(1, KV_WIDTH), seq3(KV_WIDTH, n_buf), seq3(N_HEADS, HEAD_DIM),
                  seq3(1, gw)],
        out_specs=(seq3(N_HEADS, HEAD_DIM), seq3(KV_WIDTH, n_buf)),
        scratch_shapes=[pltpu.VMEM((2, N_KV_HEADS * n_sel, 2, HEAD_DIM, PAGE_SIZE), F32),
                        pltpu.SemaphoreType.DMA((2,))],
    )
    idx_flat = idx[:, :N_KV_HEADS, :n_sel].reshape(-1)
    return pl.pallas_call(
        functools.partial(_nsa_sample_sel_kernel, n_seq=bd, n_pages=n_pages, n_sel=n_sel,
                          ns_past=past_len // SEL_BLOCK),
        out_shape=(jax.ShapeDtypeStruct((bd, N_HEADS, HEAD_DIM), F32),
                   jax.ShapeDtypeStruct((bd, KV_WIDTH, n_buf), F32)),
        grid_spec=grid_spec,
        compiler_params=_cparams(("arbitrary",)),
        name="nsa_sample_sel",
    )(page_table.reshape(-1), idx_flat, pool_t, qr_seq, idx, kvs_new.reshape(bd, 1, KV_WIDTH),
      kvw_new.reshape(bd, 1, KV_WIDTH), win_t, o_c, gates.reshape(bd, 1, gw))


def _tail_kernel(x_ref, o_ref, y_ref, ga_ref, gb_ref, wa_ref, wb_ref, wo_ref, g1_ref, b1_ref, wr_ref, rb_ref,
                 cin_ref, x1_ref, idx_ref, w_ref, pos_ref, cnt_ref, carry_s, *, alpha):
    i = pl.program_id(0)
    tm = x_ref.shape[0]
    ne = wr_ref.shape[1]

    @pl.when(i == 0)
    def _():
        carry_s[...] = cin_ref[...]

    merged = (_sigmoid(ga_ref[...]) * _dot(o_ref[...], wa_ref[...])
              + _sigmoid(gb_ref[...]) * _dot(y_ref[...], wb_ref[...]))
    mix = _dot(merged.astype(BF16), wo_ref[...])
    x1 = _layer_norm(alpha * x_ref[...] + mix, g1_ref[...], b1_ref[...])
    x1_ref[...] = x1

    scores = _sigmoid(_dot(x1.astype(BF16), wr_ref[...]))
    biased = scores + rb_ref[...]
    lane = lax.broadcasted_iota(I32, (1, ne), 1).astype(F32)
    out_lane = lax.broadcasted_iota(I32, (1, idx_ref.shape[1]), 1)
    hits = jnp.zeros((tm, ne), F32)
    picks, chosen = [], []
    for _ in range(TOP_K):
        m = jnp.max(biased, axis=-1, keepdims=True)
        idx = jnp.min(jnp.where(biased == m, lane, float(ne)), axis=-1, keepdims=True)
        pick = lane == idx
        picks.append(pick)
        chosen.append(jnp.sum(jnp.where(pick, scores, 0.0), axis=-1, keepdims=True))
        hits = jnp.where(pick, 1.0, hits)
        biased = jnp.where(pick, -jnp.inf, biased)
    total = chosen[0]
    for c in chosen[1:]:
        total = total + c

    before = lax.broadcasted_iota(I32, (tm, tm), 1) < lax.broadcasted_iota(I32, (tm, tm), 0)
    prefix = _dot(jnp.where(before, 1.0, 0.0).astype(BF16), hits.astype(BF16)) + carry_s[...]
    idx_out = jnp.zeros(idx_ref.shape, I32)
    w_out = jnp.zeros(w_ref.shape, F32)
    pos_out = jnp.zeros(pos_ref.shape, I32)
    for k in range(TOP_K):
        e_k = jnp.sum(jnp.where(picks[k], lane, 0.0), axis=-1, keepdims=True).astype(I32)
        p_k = jnp.sum(jnp.where(picks[k], prefix, 0.0), axis=-1, keepdims=True).astype(I32)
        idx_out = jnp.where(out_lane == k, e_k, idx_out)
        pos_out = jnp.where(out_lane == k, p_k, pos_out)
        w_out = jnp.where(out_lane == k, chosen[k] / total * ROUTED_SCALE, w_out)
    idx_ref[...] = idx_out
    w_ref[...] = w_out
    pos_ref[...] = pos_out
    carry_s[...] = carry_s[...] + jnp.sum(hits, axis=0, keepdims=True)
    cnt_ref[...] = carry_s[...]


def _tail_weights(p):
    row = lambda v: v.reshape(1, -1).astype(F32)
    return (p['w_branch_attn'].astype(BF16), p['w_branch_rnn'].astype(BF16), p['w_out'].astype(BF16),
            row(p['ln1_g']), row(p['ln1_b']), p['w_router'].astype(BF16), row(p['router_bias']))


def _tail(x2d, o_attn, y_rnn, g_a, g_b, tw, counts_in, alpha, tm):
    m, d = x2d.shape
    ne = tw[5].shape[1]
    row = lambda w: pl.BlockSpec((tm, w), lambda i: (i, 0))
    small = jax.ShapeDtypeStruct((m, 8), I32)
    return pl.pallas_call(
        functools.partial(_tail_kernel, alpha=alpha),
        out_shape=(jax.ShapeDtypeStruct((m, d), F32), small, jax.ShapeDtypeStruct((m, 8), F32), small,
                   jax.ShapeDtypeStruct((1, ne), F32)),
        grid=(m // tm,),
        in_specs=[row(d), row(o_attn.shape[1]), row(y_rnn.shape[1]), row(d), row(d)]
                 + [_full(w.shape) for w in tw] + [_full((1, ne))],
        out_specs=(row(d), row(8), row(8), row(8), _full((1, ne))),
        scratch_shapes=[pltpu.VMEM((1, ne), F32)],
        compiler_params=_cparams(("arbitrary",)),
        name="tail",
    )(x2d, o_attn, y_rnn, g_a, g_b, *tw, counts_in)


def _silu(x):
    return x * _sigmoid(x)


def _experts_kernel(be_ref, nu_ref, rt_ref, x_ref, wg_ref, wu_ref, wd_ref, y_ref, xbuf, wgb, wub, wdb, sem):
    r = pl.program_id(0)
    n_used = nu_ref[0]
    rb = xbuf.shape[1]

    def row_copy(blk, slot, j):
        return pltpu.make_async_copy(x_ref.at[pl.ds(rt_ref[blk * rb + j], 1)], xbuf.at[slot, pl.ds(j, 1)],
                                     sem.at[slot])

    @pl.when(r == 0)
    def _():
        _row_gather(rb, lambda c: c.start(), lambda j: row_copy(0, 0, j))

    @pl.when(r + 1 < n_used)
    def _():
        _row_gather(rb, lambda c: c.start(), lambda j: row_copy(r + 1, (r + 1) % 2, j))

    @pl.when(r < n_used)
    def _():
        slot = r % 2
        _row_gather(rb, lambda c: c.wait(), lambda j: row_copy(r, slot, j))

        @pl.when((r == 0) | (be_ref[r] != be_ref[jnp.maximum(r - 1, 0)]))
        def _():
            wgb[...] = wg_ref[0].astype(BF16)
            wub[...] = wu_ref[0].astype(BF16)
            wdb[...] = wd_ref[0].astype(BF16)

        x = xbuf[slot].astype(BF16)
        hid = _silu(_dot(x, wgb[...])) * _dot(x, wub[...])
        y_ref[...] = _dot(hid.astype(BF16), wdb[...])

    @pl.when(r >= n_used)
    def _():
        y_ref[...] = jnp.zeros(y_ref.shape, F32)


def _experts(x1_all, row_tok, blk_expert, n_used, w_gate, w_up, w_down):
    n_blocks = blk_expert.shape[0]
    d = x1_all.shape[1]
    de = w_gate.shape[2]
    rb = EXPERT_ROWS
    grid_spec = pltpu.PrefetchScalarGridSpec(
        num_scalar_prefetch=3,
        grid=(n_blocks,),
        in_specs=[pl.BlockSpec(memory_space=pl.ANY),
                  pl.BlockSpec((1, d, de), lambda r, be, nu, rt: (be[r], 0, 0)),
                  pl.BlockSpec((1, d, de), lambda r, be, nu, rt: (be[r], 0, 0)),
                  pl.BlockSpec((1, de, d), lambda r, be, nu, rt: (be[r], 0, 0))],
        out_specs=pl.BlockSpec((rb, d), lambda r, be, nu, rt: (r, 0)),
        scratch_shapes=[pltpu.VMEM((2, rb, d), F32), pltpu.VMEM((d, de), BF16), pltpu.VMEM((d, de), BF16),
                        pltpu.VMEM((de, d), BF16), pltpu.SemaphoreType.DMA((2,))],
    )
    return pl.pallas_call(
        _experts_kernel,
        out_shape=jax.ShapeDtypeStruct((n_blocks * rb, d), F32),
        grid_spec=grid_spec,
        compiler_params=_cparams(("arbitrary",)),
        name="experts",
    )(blk_expert, n_used, row_tok, x1_all, w_gate, w_up, w_down)


def _combine_kernel(dest_ref, x1_ref, w_ref, y_ref, wsg_ref, wsu_ref, wsd_ref, g2_ref, b2_ref, out_ref,
                    ybuf, sem, *, tok_off, alpha):
    i = pl.program_id(0)
    n = pl.num_programs(0)
    tc = x1_ref.shape[0]

    def row_copy(tile, slot, k, t):
        src = dest_ref[(tok_off + tile * tc + t) * TOP_K + k]
        return pltpu.make_async_copy(y_ref.at[pl.ds(src, 1)], ybuf.at[slot, k, pl.ds(t, 1)], sem.at[slot])

    def gather(tile, slot, go):
        for k in range(TOP_K):
            _row_gather(tc, go, lambda t: row_copy(tile, slot, k, t))

    @pl.when(i == 0)
    def _():
        gather(0, 0, lambda c: c.start())

    @pl.when(i + 1 < n)
    def _():
        gather(i + 1, (i + 1) % 2, lambda c: c.start())

    slot = i % 2
    gather(i, slot, lambda c: c.wait())
    x1 = x1_ref[...]
    w = w_ref[...]
    routed = w[:, 0:1] * ybuf[slot, 0]
    for k in range(1, TOP_K):
        routed = routed + w[:, k:k + 1] * ybuf[slot, k]
    xb = x1.astype(BF16)
    shared = _dot((_silu(_dot(xb, wsg_ref[...])) * _dot(xb, wsu_ref[...])).astype(BF16), wsd_ref[...])
    out_ref[...] = _layer_norm(alpha * x1 + (routed + shared), g2_ref[...], b2_ref[...])


def _combine_weights(p):
    row = lambda v: v.reshape(1, -1).astype(F32)
    return (p['w_sh_gate'].astype(BF16), p['w_sh_up'].astype(BF16), p['w_sh_down'].astype(BF16),
            row(p['ln2_g']), row(p['ln2_b']))


def _combine(dest, x1, w, y_rows, cw, tok_off, alpha, tc):
    m, d = x1.shape
    row = lambda width: pl.BlockSpec((tc, width), lambda i, ds: (i, 0))
    grid_spec = pltpu.PrefetchScalarGridSpec(
        num_scalar_prefetch=1,
        grid=(m // tc,),
        in_specs=[row(d), row(w.shape[1]), pl.BlockSpec(memory_space=pl.ANY)]
                 + [pl.BlockSpec(a.shape, lambda i, ds: (0, 0)) for a in cw],
        out_specs=row(d),
        scratch_shapes=[pltpu.VMEM((2, TOP_K, tc, d), F32), pltpu.SemaphoreType.DMA((2,))],
    )
    return pl.pallas_call(
        functools.partial(_combine_kernel, tok_off=tok_off, alpha=alpha),
        out_shape=jax.ShapeDtypeStruct((m, d), F32),
        grid_spec=grid_spec,
        compiler_params=_cparams(("arbitrary",)),
        name="combine",
    )(dest, x1, w, y_rows, *cw)


def _route(idx, pos, counts, n_tok):
    ne = counts.shape[0]
    rb = EXPERT_ROWS
    n_blocks = -(-n_tok * TOP_K // rb) + ne
    padded = (counts + rb - 1) // rb * rb
    end_pad = jnp.cumsum(padded)
    start_pad = end_pad - padded
    dest = start_pad[idx] + pos
    tok = jnp.broadcast_to(jnp.arange(n_tok, dtype=I32)[:, None], dest.shape)
    row_tok = jnp.zeros((n_blocks * rb,), I32).at[dest.reshape(-1)].set(tok.reshape(-1))
    first_row = jnp.arange(n_blocks, dtype=I32) * rb
    blk_expert = jnp.minimum(jnp.sum((end_pad[None, :] <= first_row[:, None]).astype(I32), axis=1), ne - 1)
    n_used = (end_pad[-1] // rb).astype(I32).reshape(1)
    return dest.reshape(-1).astype(I32), row_tok, blk_expert, n_used


def _layer(xp, xs, caches, page_table, p, depth):
    b, t, d = xp.shape
    bd, s, _ = xs.shape
    assert s == 1, "sample group is one new token per sequence"
    pool_c, pool_s, win_buf, state_conv, state_rnn = caches
    past_len = page_table.shape[1] * PAGE_SIZE
    alpha = (2.0 * depth) ** 0.25
    kv6 = lambda a, n, rows: a.reshape(n, rows, 2, N_KV_HEADS, HEAD_DIM)
    rows_major = lambda a: jnp.transpose(a.reshape(a.shape[0], 2, N_KV_HEADS, HEAD_DIM, a.shape[2]), (0, 4, 1, 2, 3))

    wparts = _split_w_in(p['w_in'])
    cw = _compress_weights(p)
    rw = _rglru_weights(p)
    tw = _tail_weights(p)
    mw = _combine_weights(p)

    pos_p = jnp.tile(jnp.arange(t, dtype=I32), b)
    (q, qr, kvc, kvc_t, kvs_t, kvw_t, ksh, vsh, kwh, vwh, gates, u_rnn, u_gate, g_a, g_b) = _proj(
        xp.reshape(b * t, d), pos_p, wparts, 256, seq_len=t)
    kc, vc = _compress_prompt(kvc, b, t, cw)
    o_attn = _nsa_prompt(q, qr, kc, vc, ksh, vsh, kwh, vwh, gates, b, t)
    y_rnn, h_p = _rglru_prompt(u_rnn, u_gate, rw, b, t, 256)
    ne = p['w_router'].shape[1]
    x1_p, idx_p, w_p, pos_r_p, counts_p = _tail(xp.reshape(b * t, d), o_attn, y_rnn.reshape(b * t, -1), g_a, g_b,
                                                tw, jnp.zeros((1, ne), F32), alpha, 256)
    wn = min(WINDOW, t)
    outs_p = (rows_major(kvc_t), rows_major(kvs_t), rows_major(kvw_t[:, :, t - wn:]),
              u_rnn.reshape(b, t, -1)[:, t - (CONV_W - 1):], h_p.reshape(b, -1))

    pos_s = jnp.full((bd,), past_len, I32)
    (q, qr, kvc_s, kvs_s, kvw_s, _, _, _, _, gates_s, u_rnn_s, u_gate_s, g_a_s, g_b_s) = _proj(
        xs.reshape(bd, d), pos_s, wparts, bd)
    o_c, sel_idx = _nsa_sample_cmp(page_table, _pages_feature_major(pool_c), q.transpose(1, 0, 2), cw, past_len)
    o_s, win_new_t = _nsa_sample_sel(page_table, sel_idx, _pages_feature_major(pool_s), qr.transpose(1, 0, 2),
                                     kvs_s, kvw_s, _pages_feature_major(win_buf), o_c, gates_s, past_len)
    y_rnn_s, h_s = _rglru_step(u_rnn_s, u_gate_s, state_conv.transpose(1, 0, 2), state_rnn, rw)
    x1_s, idx_s, w_s, pos_r_s, counts = _tail(xs.reshape(bd, d), o_s.reshape(bd, Q_WIDTH).astype(BF16), y_rnn_s,
                                              g_a_s, g_b_s, tw, counts_p, alpha, bd)
    conv_s = jnp.concatenate([state_conv[:, 1:], u_rnn_s[:, None, :]], axis=1)
    outs_s = (kv6(kvc_s, bd, 1), kv6(kvs_s, bd, 1), rows_major(win_new_t), conv_s, h_s)

    n_tok = b * t + bd
    idx_all = jnp.concatenate([idx_p, idx_s], axis=0)[:, :TOP_K]
    pos_all = jnp.concatenate([pos_r_p, pos_r_s], axis=0)[:, :TOP_K]
    dest, row_tok, blk_expert, n_used = _route(idx_all, pos_all, counts.reshape(-1).astype(I32), n_tok)
    x1_all = jnp.concatenate([x1_p, x1_s], axis=0)
    y_rows = _experts(x1_all, row_tok, blk_expert, n_used, p['w_exp_gate'], p['w_exp_up'], p['w_exp_down'])
    yp = _combine(dest, x1_p, w_p, y_rows, mw, 0, alpha, 128)
    ys = _combine(dest, x1_s, w_s, y_rows, mw, b * t, alpha, bd)
    return yp.reshape(b, t, d), ys.reshape(bd, s, d), outs_p, outs_s


def kernel(x_prompt, x_sample, cache_cmp_kv, cache_sel_kv, cache_win_kv, state_conv, state_rnn, page_table,
           w_in, conv_w, conv_b, w_rg_a, b_rg_a, w_rg_i, b_rg_i, lru_lambda, cmp_pos_k, cmp_pos_v,
           w_cmp_k1, w_cmp_k2, w_cmp_v1, w_cmp_v2, w_branch_attn, w_branch_rnn, w_out, ln1_g, ln1_b,
           w_router, router_bias, w_exp_gate, w_exp_up, w_exp_down, w_sh_gate, w_sh_up, w_sh_down, ln2_g, ln2_b):
    weights = dict(w_in=w_in, conv_w=conv_w, conv_b=conv_b, w_rg_a=w_rg_a, b_rg_a=b_rg_a, w_rg_i=w_rg_i,
                   b_rg_i=b_rg_i, lru_lambda=lru_lambda, cmp_pos_k=cmp_pos_k, cmp_pos_v=cmp_pos_v,
                   w_cmp_k1=w_cmp_k1, w_cmp_k2=w_cmp_k2, w_cmp_v1=w_cmp_v1, w_cmp_v2=w_cmp_v2,
                   w_branch_attn=w_branch_attn, w_branch_rnn=w_branch_rnn, w_out=w_out, ln1_g=ln1_g, ln1_b=ln1_b,
                   w_router=w_router, router_bias=router_bias, w_exp_gate=w_exp_gate, w_exp_up=w_exp_up,
                   w_exp_down=w_exp_down, w_sh_gate=w_sh_gate, w_sh_up=w_sh_up, w_sh_down=w_sh_down,
                   ln2_g=ln2_g, ln2_b=ln2_b)
    depth = w_in.shape[0]
    xp, xs = x_prompt, x_sample
    per_layer_p, per_layer_s = [], []
    for l in range(depth):
        p = {k: v[l] for k, v in weights.items()}
        caches = (cache_cmp_kv[l], cache_sel_kv[l], cache_win_kv[l], state_conv[l], state_rnn[l])
        xp, xs, outs_p, outs_s = _layer(xp, xs, caches, page_table, p, depth)
        per_layer_p.append(outs_p)
        per_layer_s.append(outs_s)
    stack = lambda outs, i: jnp.stack([o[i] for o in outs])
    return (xp, xs, stack(per_layer_p, 0), stack(per_layer_s, 0), stack(per_layer_p, 1), stack(per_layer_s, 1),
            stack(per_layer_p, 2), stack(per_layer_s, 2), stack(per_layer_p, 3), stack(per_layer_s, 3),
            stack(per_layer_p, 4), stack(per_layer_s, 4))
```

```python
import functools

import numpy as np
import jax
import jax.numpy as jnp
from jax import lax
from jax.experimental import pallas as pl
from jax.experimental.pallas import tpu as pltpu

F32 = jnp.float32
BF16 = jnp.bfloat16
I32 = jnp.int32

N_HEADS = 8
HEAD_DIM = 64
N_KV_HEADS = 2
GROUP = N_HEADS // N_KV_HEADS
ROT_DIM = HEAD_DIM // 4
ROPE_THETA = 500000.0
CMP_BLOCK = 32
CMP_STRIDE = 16
CMP_HIDDEN = 64
SEL_BLOCK = 64
N_SEL = 16
SEL_FORCE = 1.0e6
WINDOW = 512
Q_BLOCK = 128
RNN_BLOCKS = 8
CONV_W = 4
RG_C = 8.0
TOP_K = 6
ROUTED_SCALE = 2.5
LN_EPS = 1e-5
PAGE_SIZE = 128

Q_WIDTH = N_HEADS * HEAD_DIM
KV_WIDTH = 2 * N_KV_HEADS * HEAD_DIM
N_GROUPS = 2 * N_KV_HEADS
CHUNK_W = CMP_STRIDE * KV_WIDTH
LANES = 128
NEG = -1.0e30
VMEM_LIMIT = 56 * 1024 * 1024

EXPERT_ROWS = 256
SEL_TK = 512


def _cparams(sem):
    return pltpu.CompilerParams(dimension_semantics=sem, vmem_limit_bytes=VMEM_LIMIT)


def _full(shape):
    n = len(shape)
    return pl.BlockSpec(shape, lambda *a: (0,) * n)


def _dot(a, b):
    return jnp.dot(a, b, preferred_element_type=F32)


def _dot_nt(a, b):
    return lax.dot_general(a, b, (((1,), (1,)), ((), ())), preferred_element_type=F32)


def _gelu(x):
    return 0.5 * x * (1.0 + jnp.tanh(np.sqrt(2.0 / np.pi) * (x + 0.044715 * (x * x * x))))


def _sigmoid(x):
    return 1.0 / (1.0 + jnp.exp(-x))


def _layer_norm(z, g, b):
    mu = jnp.mean(z, axis=-1, keepdims=True)
    d = z - mu
    var = jnp.mean(d * d, axis=-1, keepdims=True)
    return d * lax.rsqrt(var + LN_EPS) * g + b


def _rope_tables(pos):
    half = ROT_DIM // 2
    inv = ROPE_THETA ** (-(jnp.arange(half, dtype=F32) * (2.0 / ROT_DIM)))
    ang = pos.astype(F32)[:, None] * inv[None, :]
    cos, sin = jnp.cos(ang), jnp.sin(ang)
    m = pos.shape[0]
    one = jnp.ones((m, HEAD_DIM - ROT_DIM), F32)
    zero = jnp.zeros((m, HEAD_DIM - ROT_DIM), F32)
    zh = jnp.zeros((m, half), F32)
    c = jnp.concatenate([cos, cos, one], axis=1)
    s1 = jnp.concatenate([-sin, zh, zero], axis=1)
    s2 = jnp.concatenate([zh, sin, zero], axis=1)
    rep = LANES // HEAD_DIM
    return jnp.tile(c, (1, rep)), jnp.tile(s1, (1, rep)), jnp.tile(s2, (1, rep))


def _rope128(x, c, s1, s2):
    half = ROT_DIM // 2
    return x * c + pltpu.roll(x, LANES - half, 1) * s1 + pltpu.roll(x, half, 1) * s2


def _proj_kernel(x_ref, c_ref, s1_ref, s2_ref, wq_ref, wkv_ref, wg_ref, ww_ref, *out_refs, transposed):
    if transposed:
        (q_ref, qr_ref, kvc_ref, kvct_ref, kvst_ref, kvwt_ref, ksh_ref, vsh_ref, kwh_ref, vwh_ref,
         gate_ref, urnn_ref, ugate_ref, ga_ref, gb_ref) = out_refs
        kvs_ref = kvw_ref = None
    else:
        (q_ref, qr_ref, kvc_ref, kvs_ref, kvw_ref, ksh_ref, vsh_ref, kwh_ref, vwh_ref,
         gate_ref, urnn_ref, ugate_ref, ga_ref, gb_ref) = out_refs
        kvct_ref = kvst_ref = kvwt_ref = None
    xb = x_ref[...].astype(BF16)
    c, s1, s2 = c_ref[...], s1_ref[...], s2_ref[...]
    scale = HEAD_DIM ** -0.5
    q = _dot(xb, wq_ref[...]) * scale
    for j in range(Q_WIDTH // LANES):
        ch = q[:, LANES * j:LANES * (j + 1)]
        rot = _rope128(ch, c, s1, s2)
        for hh in range(LANES // HEAD_DIM):
            head = j * (LANES // HEAD_DIM) + hh
            q_ref[head] = ch[:, HEAD_DIM * hh:HEAD_DIM * (hh + 1)].astype(BF16)
            qr_ref[head] = rot[:, HEAD_DIM * hh:HEAD_DIM * (hh + 1)].astype(BF16)
    kv = _dot(xb, wkv_ref[...])
    kvc_ref[...] = kv[:, :KV_WIDTH]
    if transposed:
        kvct_ref[0] = kv[:, :KV_WIDTH].T
    for base, full_ref, t_ref, kh_ref, vh_ref in ((KV_WIDTH, kvs_ref, kvst_ref, ksh_ref, vsh_ref),
                                                  (2 * KV_WIDTH, kvw_ref, kvwt_ref, kwh_ref, vwh_ref)):
        keys = _rope128(kv[:, base:base + LANES], c, s1, s2)
        vals = kv[:, base + LANES:base + 2 * LANES]
        if transposed:
            t_ref[0, :LANES, :] = keys.T
            t_ref[0, LANES:, :] = vals.T
        else:
            full_ref[:, :LANES] = keys
            full_ref[:, LANES:] = vals
        for h in range(N_KV_HEADS):
            kh_ref[h] = keys[:, HEAD_DIM * h:HEAD_DIM * (h + 1)].astype(BF16)
            vh_ref[h] = vals[:, HEAD_DIM * h:HEAD_DIM * (h + 1)].astype(BF16)
    gate_ref[...] = _sigmoid(_dot(xb, wg_ref[...]))
    d = urnn_ref.shape[-1]
    for k, ref in enumerate((urnn_ref, ugate_ref, ga_ref, gb_ref)):
        ref[...] = _dot(xb, ww_ref[:, d * k:d * (k + 1)])


def _split_w_in(w_in):
    d = w_in.shape[0]
    o = 0
    wq = w_in[:, o:o + Q_WIDTH]; o += Q_WIDTH
    wkv = w_in[:, o:o + 3 * KV_WIDTH]; o += 3 * KV_WIDTH
    wg = w_in[:, o:o + 3 * N_HEADS]; o += 3 * N_HEADS
    ww = w_in[:, o:]
    per = 3 * GROUP
    wg2 = jnp.zeros((d, N_KV_HEADS * LANES), w_in.dtype)
    for h in range(N_KV_HEADS):
        wg2 = wg2.at[:, h * LANES:h * LANES + per].set(wg[:, h * per:(h + 1) * per])
    return wq.astype(BF16), wkv.astype(BF16), wg2.astype(BF16), ww.astype(BF16)


def _proj(x2d, pos, wparts, tm, seq_len=None):
    m, d = x2d.shape
    wq, wkv, wg, ww = wparts
    c, s1, s2 = _rope_tables(pos)
    dw = ww.shape[1] // 4
    row = lambda w: pl.BlockSpec((tm, w), lambda i: (i, 0))
    hm = lambda n: pl.BlockSpec((n, tm, HEAD_DIM), lambda i: (0, i, 0))
    rows_f32 = jax.ShapeDtypeStruct((m, KV_WIDTH), F32)
    hm_q = jax.ShapeDtypeStruct((N_HEADS, m, HEAD_DIM), BF16)
    hm_kv = jax.ShapeDtypeStruct((N_KV_HEADS, m, HEAD_DIM), BF16)
    wide = jax.ShapeDtypeStruct((m, dw), F32)
    if seq_len is None:
        kv_shapes = (rows_f32, rows_f32, rows_f32)
        kv_specs = (row(KV_WIDTH),) * 3
    else:
        nt = seq_len // tm
        kv_t = jax.ShapeDtypeStruct((m // seq_len, KV_WIDTH, seq_len), F32)
        t_spec = pl.BlockSpec((1, KV_WIDTH, tm), lambda i: (i // nt, 0, i % nt))
        kv_shapes = (rows_f32, kv_t, kv_t, kv_t)
        kv_specs = (row(KV_WIDTH), t_spec, t_spec, t_spec)
    out_shape = (hm_q, hm_q) + kv_shapes + (hm_kv,) * 4 + (
        jax.ShapeDtypeStruct((m, N_KV_HEADS * LANES), F32), wide, wide, wide, wide)
    out_specs = (hm(N_HEADS), hm(N_HEADS)) + kv_specs + (hm(N_KV_HEADS),) * 4 + (
        row(N_KV_HEADS * LANES), row(dw), row(dw), row(dw), row(dw))
    return pl.pallas_call(
        functools.partial(_proj_kernel, transposed=seq_len is not None),
        out_shape=out_shape,
        grid=(m // tm,),
        in_specs=[row(d), row(LANES), row(LANES), row(LANES),
                  _full(wq.shape), _full(wkv.shape), _full(wg.shape), _full(ww.shape)],
        out_specs=out_specs,
        compiler_params=_cparams(("parallel",)),
        name="proj",
    )(x2d, c, s1, s2, wq, wkv, wg, ww)


def _compress_weights(p):
    parts = CMP_BLOCK // CMP_STRIDE
    gw = N_GROUPS * CMP_HIDDEN
    w1 = jnp.zeros((CMP_STRIDE, N_GROUPS, HEAD_DIM, parts, N_GROUPS, CMP_HIDDEN), F32)
    pe = jnp.zeros((8, CMP_STRIDE, N_GROUPS, HEAD_DIM), F32)
    w2 = jnp.zeros((N_GROUPS, CMP_HIDDEN, N_GROUPS, HEAD_DIM), F32)
    for g in range(N_GROUPS):
        kv = g // N_KV_HEADS
        w1_src = (p['w_cmp_k1'], p['w_cmp_v1'])[kv].reshape(parts, CMP_STRIDE, HEAD_DIM, CMP_HIDDEN)
        pe_src = (p['cmp_pos_k'], p['cmp_pos_v'])[kv].reshape(parts, CMP_STRIDE, HEAD_DIM)
        w2_src = (p['w_cmp_k2'], p['w_cmp_v2'])[kv]
        for m in range(parts):
            w1 = w1.at[:, g, :, m, g, :].set(w1_src[m])
            pe = pe.at[m, :, g, :].set(pe_src[m])
        w2 = w2.at[g, :, g, :].set(w2_src)
    return (w1.reshape(CHUNK_W, parts * gw).astype(BF16), pe.reshape(8, CHUNK_W).astype(BF16),
            w2.reshape(gw, N_GROUPS * HEAD_DIM).astype(BF16))


def _compress_rows(c, pec, w2):
    gw = N_GROUPS * CMP_HIDDEN
    n = c.shape[0]
    const = pec[0:1, :gw] + pec[1:2, gw:]
    hid = c[:, :gw] + pltpu.roll(c[:, gw:], n - 1, 0) + const
    return _dot(_gelu(hid).astype(BF16), w2)


def _compress_prompt_kernel(ch_ref, w1_ref, pe_ref, w2_ref, kc_ref, vc_ref):
    c = _dot(ch_ref[0].astype(BF16), w1_ref[...])
    pec = _dot(pe_ref[...], w1_ref[...])
    out = _compress_rows(c, pec, w2_ref[...])
    for h in range(N_KV_HEADS):
        kc_ref[0, h] = out[:, HEAD_DIM * h:HEAD_DIM * (h + 1)].astype(BF16)
        vc_ref[0, h] = out[:, HEAD_DIM * (N_KV_HEADS + h):HEAD_DIM * (N_KV_HEADS + h + 1)].astype(BF16)


def _compress_prompt(kvc, b, t, cw):
    w1, pe, w2 = cw
    nch = t // CMP_STRIDE
    chunks = kvc.reshape(b, nch, CHUNK_W)
    out = jax.ShapeDtypeStruct((b, N_KV_HEADS, nch, HEAD_DIM), BF16)
    ospec = pl.BlockSpec((1, N_KV_HEADS, nch, HEAD_DIM), lambda i: (i, 0, 0, 0))
    return pl.pallas_call(
        _compress_prompt_kernel,
        out_shape=(out, out),
        grid=(b,),
        in_specs=[pl.BlockSpec((1, nch, CHUNK_W), lambda i: (i, 0, 0)),
                  _full(w1.shape), _full(pe.shape), _full(w2.shape)],
        out_specs=(ospec, ospec),
        compiler_params=_cparams(("parallel",)),
        name="compress_prompt",
    )(chunks, w1, pe, w2)


def _sel_map(n_cmp_pad, n_cmp, n_sel, n_sel_pad):
    ratio = SEL_BLOCK // CMP_STRIDE
    i = np.arange(n_cmp_pad)[:, None]
    j = np.arange(n_sel_pad)[None, :]
    d = i - ratio * j
    m = sum(((d + n >= 0) & (d + n < ratio)).astype(np.float32) for n in range(CMP_BLOCK // CMP_STRIDE))
    m = m * (i < n_cmp) * (j < n_sel)
    return jnp.asarray(m, BF16)


def _masked_softmax_rows(s, valid):
    s = jnp.where(valid, s, -jnp.inf)
    m = jnp.max(s, axis=-1, keepdims=True)
    m = jnp.where(m == -jnp.inf, 0.0, m)
    e = jnp.exp(s - m)
    return e / jnp.maximum(jnp.sum(e, axis=-1, keepdims=True), 1e-30)


def _top_rows(score, n):
    rows = score.shape[0]
    row = lax.broadcasted_iota(I32, score.shape, 0).astype(F32)
    sel = jnp.zeros(score.shape, F32)
    picked = []
    for _ in range(n):
        m = jnp.max(score, axis=0, keepdims=True)
        idx = jnp.min(jnp.where(score == m, row, float(rows)), axis=0, keepdims=True)
        pick = row == idx
        picked.append(idx)
        sel = jnp.where(pick, 1.0, sel)
        score = jnp.where(pick, -jnp.inf, score)
    return picked, sel


def _nsa_prompt_kernel(q_ref, qr_ref, kc_ref, vc_ref, ks_ref, vs_ref, kw_ref, vw_ref, g_ref, smap_ref,
                       o_ref, *, n_sel_blocks):
    qb = pl.program_id(2)
    s0 = qb * Q_BLOCK
    rows = GROUP * Q_BLOCK
    q = q_ref[...].reshape(rows, HEAD_DIM)
    qr = qr_ref[...].reshape(rows, HEAD_DIM)
    pos1 = s0 + lax.broadcasted_iota(I32, (Q_BLOCK, 1), 0)
    pos = s0 + (lax.broadcasted_iota(I32, (rows, 1), 0) & (Q_BLOCK - 1))

    kc = kc_ref[0, 0]
    ncp = kc.shape[0]
    c_end = lax.broadcasted_iota(I32, (1, ncp), 1) * CMP_STRIDE + (CMP_BLOCK - 1)
    p_c = _masked_softmax_rows(_dot_nt(q, kc), c_end <= pos).astype(BF16)
    o_c = _dot(p_c, vc_ref[0, 0])
    imp_t = _dot_nt(smap_ref[...], p_c[0:Q_BLOCK])
    for g in range(1, GROUP):
        imp_t = imp_t + _dot_nt(smap_ref[...], p_c[g * Q_BLOCK:(g + 1) * Q_BLOCK])

    wlen = WINDOW + Q_BLOCK
    wstart = pl.multiple_of(jnp.maximum(s0 - WINDOW, 0), Q_BLOCK)
    kpos = wstart + lax.broadcasted_iota(I32, (1, wlen), 1)
    diff = pos - kpos
    p_w = _masked_softmax_rows(_dot_nt(qr, kw_ref[0, pl.ds(wstart, wlen), :]), (diff >= 0) & (diff <= WINDOW))
    o_w = _dot(p_w.astype(BF16), vw_ref[0, pl.ds(wstart, wlen), :])

    blk_t = lax.broadcasted_iota(I32, (LANES, Q_BLOCK), 0)
    cur_t = (s0 + lax.broadcasted_iota(I32, (1, Q_BLOCK), 1)) // SEL_BLOCK
    causal_t = blk_t <= cur_t
    forced_t = ((blk_t == 0) | (blk_t == cur_t) | (blk_t == cur_t - 1)) & causal_t
    n_forced = 3
    _, ranked_t = _top_rows(jnp.where(causal_t & jnp.logical_not(forced_t), imp_t, -jnp.inf),
                            min(N_SEL, n_sel_blocks) - n_forced)
    sel_t = jnp.where(forced_t | (causal_t & (ranked_t > 0.5)), 1.0, 0.0)
    sel = sel_t.T.astype(BF16)

    per_tile = SEL_TK // SEL_BLOCK
    jrow = lax.broadcasted_iota(I32, (LANES, SEL_TK), 0)
    cblk = lax.broadcasted_iota(I32, (LANES, SEL_TK), 1) // SEL_BLOCK
    kcol = lax.broadcasted_iota(I32, (1, SEL_TK), 1)

    def sel_tile(kt, carry, causal):
        m_i, l_i, acc = carry
        start = pl.multiple_of(kt * SEL_TK, SEL_TK)
        s = _dot_nt(qr, ks_ref[0, pl.ds(start, SEL_TK), :]).reshape(GROUP, Q_BLOCK, SEL_TK)
        expand = jnp.where(jrow - kt * per_tile == cblk, 1.0, 0.0).astype(BF16)
        ok = _dot(sel, expand) > 0.5
        if causal:
            ok = ok & (start + kcol <= pos1)
        s = jnp.where(ok[None], s, NEG)
        m_new = jnp.maximum(m_i, jnp.max(s, axis=-1, keepdims=True))
        alpha = jnp.exp(m_i - m_new)
        p = jnp.exp(s - m_new)
        l_new = alpha * l_i + jnp.sum(p, axis=-1, keepdims=True)
        pv = _dot(p.reshape(rows, SEL_TK).astype(BF16), vs_ref[0, pl.ds(start, SEL_TK), :])
        return m_new, l_new, alpha * acc + pv.reshape(GROUP, Q_BLOCK, HEAD_DIM)

    init = (jnp.full((GROUP, Q_BLOCK, 1), NEG, F32), jnp.zeros((GROUP, Q_BLOCK, 1), F32),
            jnp.zeros((GROUP, Q_BLOCK, HEAD_DIM), F32))
    n_full = s0 // SEL_TK
    carry = lax.fori_loop(0, n_full, lambda kt, cr: sel_tile(kt, cr, False), init)
    _, l_s, acc_s = sel_tile(n_full, carry, True)
    o_s = (acc_s / l_s).reshape(rows, HEAD_DIM)

    gates = g_ref[...]
    outs = []
    for g in range(GROUP):
        r = slice(g * Q_BLOCK, (g + 1) * Q_BLOCK)
        outs.append(gates[:, 3 * g:3 * g + 1] * o_c[r] + gates[:, 3 * g + 1:3 * g + 2] * o_s[r]
                    + gates[:, 3 * g + 2:3 * g + 3] * o_w[r])
    o_ref[...] = jnp.concatenate(outs, axis=1).astype(o_ref.dtype)


def _nsa_prompt(q_hm, qr_hm, kc, vc, ksh, vsh, kwh, vwh, gates, b, t):
    nqb = t // Q_BLOCK
    n_sel_blocks = t // SEL_BLOCK
    ncp = kc.shape[2]
    assert N_SEL <= n_sel_blocks <= LANES and t % SEL_TK == 0 and t >= WINDOW + Q_BLOCK
    smap = _sel_map(ncp, ncp - 1, n_sel_blocks, LANES).T
    qspec = pl.BlockSpec((GROUP, Q_BLOCK, HEAD_DIM), lambda bi, h, i: (h, bi * nqb + i, 0))
    cspec = pl.BlockSpec((1, 1, ncp, HEAD_DIM), lambda bi, h, i: (bi, h, 0, 0))
    kspec = pl.BlockSpec((1, t, HEAD_DIM), lambda bi, h, i: (h, bi, 0))
    return pl.pallas_call(
        functools.partial(_nsa_prompt_kernel, n_sel_blocks=n_sel_blocks),
        out_shape=jax.ShapeDtypeStruct((b * t, Q_WIDTH), BF16),
        grid=(b, N_KV_HEADS, nqb),
        in_specs=[qspec, qspec, cspec, cspec, kspec, kspec, kspec, kspec,
                  pl.BlockSpec((Q_BLOCK, LANES), lambda bi, h, i: (bi * nqb + i, h)),
                  _full(smap.shape)],
        out_specs=pl.BlockSpec((Q_BLOCK, GROUP * HEAD_DIM), lambda bi, h, i: (bi * nqb + i, h)),
        compiler_params=_cparams(("parallel", "parallel", "arbitrary")),
        name="nsa_prompt",
    )(q_hm, qr_hm, kc, vc, ksh, vsh, kwh, vwh, gates, smap)


def _expm1(x):
    series = x * (1.0 + x * (1.0 / 2 + x * (1.0 / 6 + x * (1.0 / 24 + x * (1.0 / 120 + x * (1.0 / 720))))))
    return jnp.where(jnp.abs(x) < 0.1, series, jnp.exp(x) - 1.0)


def _log1p(z):
    series = z * (1.0 - z * (1.0 / 2 - z * (1.0 / 3 - z * (1.0 / 4 - z * (1.0 / 5 - z * (1.0 / 6 - z * (
        1.0 / 7 - z * (1.0 / 8))))))))
    return jnp.where(z < 0.1, series, jnp.log(1.0 + z))


def _rglru_gates(xc, ug, wa_ref, ba, wi_ref, bi, lam):
    xcb = xc.astype(BF16)
    bw = xc.shape[-1] // RNN_BLOCKS
    ra = jnp.concatenate([_dot(xcb[:, bw * n:bw * (n + 1)], wa_ref[n]) for n in range(RNN_BLOCKS)], axis=1)
    ri = jnp.concatenate([_dot(xcb[:, bw * n:bw * (n + 1)], wi_ref[n]) for n in range(RNN_BLOCKS)], axis=1)
    r = _sigmoid(ra + ba)
    i = _sigmoid(ri + bi)
    softplus = jnp.maximum(-lam, 0.0) + _log1p(jnp.exp(-jnp.abs(lam)))
    log_a = -RG_C * r * softplus
    a = jnp.exp(log_a)
    b = jnp.sqrt(-_expm1(2.0 * log_a)) * (i * xc)
    return a, b, _gelu(ug)


def _rglru_prompt_kernel(u_ref, ug_ref, cw_ref, cb_ref, wa_ref, ba_ref, wi_ref, bi_ref, lam_ref,
                         y_ref, h_ref, up_s, a_s, b_s, h_s):
    tt = u_ref.shape[1]
    t = pl.program_id(1)
    halo = CONV_W - 1

    @pl.when(t == 0)
    def _():
        up_s[0:8, :] = jnp.zeros((8, up_s.shape[1]), F32)
        h_s[...] = jnp.zeros(h_s.shape, F32)

    @pl.when(t > 0)
    def _():
        up_s[8 - halo:8, :] = up_s[8 + tt - halo:8 + tt, :]

    up_s[8:8 + tt, :] = u_ref[0]
    xc = cb_ref[...] + up_s[pl.ds(8 - halo, tt), :] * cw_ref[0:1, :]
    for k in range(1, CONV_W):
        xc = xc + up_s[pl.ds(8 - halo + k, tt), :] * cw_ref[k:k + 1, :]
    a, b, gate = _rglru_gates(xc, ug_ref[0], wa_ref, ba_ref[...], wi_ref, bi_ref[...], lam_ref[...])
    a_s[...] = a
    b_s[...] = b

    def step(i, h):
        h = a_s[pl.ds(i, 1), :] * h + b_s[pl.ds(i, 1), :]
        b_s[pl.ds(i, 1), :] = h
        return h

    h = lax.fori_loop(0, tt, step, h_s[...], unroll=8)
    h_s[...] = h
    h_ref[0] = h
    y_ref[0] = (b_s[...] * gate).astype(y_ref.dtype)


def _rglru_weights(p):
    row = lambda v: v.reshape(1, -1).astype(F32)
    return (p['conv_w'].astype(F32), row(p['conv_b']), p['w_rg_a'].astype(BF16), row(p['b_rg_a']),
            p['w_rg_i'].astype(BF16), row(p['b_rg_i']), row(p['lru_lambda']))


def _rglru_prompt(u, ug, rw, b, t, tt):
    d = u.shape[1]
    nt = t // tt
    seq = pl.BlockSpec((1, tt, d), lambda bi, i: (bi, i, 0))
    return pl.pallas_call(
        _rglru_prompt_kernel,
        out_shape=(jax.ShapeDtypeStruct((b, t, d), BF16), jax.ShapeDtypeStruct((b, 1, d), F32)),
        grid=(b, nt),
        in_specs=[seq, seq] + [_full(w.shape) for w in rw],
        out_specs=(seq, pl.BlockSpec((1, 1, d), lambda bi, i: (bi, 0, 0))),
        scratch_shapes=[pltpu.VMEM((tt + 8, d), F32), pltpu.VMEM((tt, d), F32), pltpu.VMEM((tt, d), F32),
                        pltpu.VMEM((1, d), F32)],
        compiler_params=_cparams(("parallel", "arbitrary")),
        name="rglru_prompt",
    )(u.reshape(b, t, d), ug.reshape(b, t, d), *rw)


def _rglru_step_kernel(u_ref, ug_ref, cs_ref, h0_ref, cw_ref, cb_ref, wa_ref, ba_ref, wi_ref, bi_ref, lam_ref,
                       y_ref, h_ref):
    xc = cb_ref[...] + u_ref[...] * cw_ref[CONV_W - 1:CONV_W, :]
    for k in range(CONV_W - 1):
        xc = xc + cs_ref[k] * cw_ref[k:k + 1, :]
    a, b, gate = _rglru_gates(xc, ug_ref[...], wa_ref, ba_ref[...], wi_ref, bi_ref[...], lam_ref[...])
    h = a * h0_ref[...] + b
    h_ref[...] = h
    y_ref[...] = (h * gate).astype(y_ref.dtype)


def _rglru_step(u, ug, conv_state, h0, rw):
    n, d = u.shape
    args = (u, ug, conv_state, h0) + tuple(rw)
    return pl.pallas_call(
        _rglru_step_kernel,
        out_shape=(jax.ShapeDtypeStruct((n, d), BF16), jax.ShapeDtypeStruct((n, d), F32)),
        grid=(1,),
        in_specs=[_full(a.shape) for a in args],
        out_specs=(_full((n, d)), _full((n, d))),
        compiler_params=_cparams(("arbitrary",)),
        name="rglru_step",
    )(*args)


def _row_gather(rows, starts, copy_fn, unroll=8):
    def body(j, c):
        starts(copy_fn(j))
        return c
    lax.fori_loop(0, rows, body, 0, unroll=unroll)


def _nsa_sample_cmp_kernel(pt_ref, pool_ref, q_ref, w1_ref, pe_ref, w2_ref, smap_ref,
                           oc_ref, idx_ref, pbuf, rows_s, c_s, sem, *, n_seq, n_pages, q_pos, n_sel_blocks):
    b = pl.program_id(0)
    nb = n_seq
    n_slots, pg = pbuf.shape[0], pbuf.shape[1]
    groups = n_pages // pg
    ahead = n_slots - 1
    rows_pp = PAGE_SIZE // CMP_STRIDE

    def page_copy(seq, grp, slot, j):
        return pltpu.make_async_copy(pool_ref.at[pt_ref[seq * n_pages + grp * pg + j]], pbuf.at[slot, j],
                                     sem.at[slot])

    def issue(seq, grp, slot):
        _row_gather(pg, lambda c: c.start(), lambda j: page_copy(seq, grp, slot, j))

    @pl.when(b == 0)
    def _():
        for g in range(ahead):
            if g // groups < n_seq:
                issue(g // groups, g % groups, g % n_slots)

    for k in range(groups):
        gi = b * groups + k
        slot = lax.rem(gi, n_slots)
        nxt = k + ahead
        seq_off, grp_nxt = nxt // groups, nxt % groups

        @pl.when(b + seq_off < nb)
        def _():
            issue(b + seq_off, grp_nxt, lax.rem(gi + ahead, n_slots))

        _row_gather(pg, lambda c: c.wait(), lambda j: page_copy(b, k, slot, j))
        halves = KV_WIDTH // LANES
        for j in range(pg):
            for f in range(halves):
                rows_s[f, j * PAGE_SIZE:(j + 1) * PAGE_SIZE, :] = pbuf[slot, j, f * LANES:(f + 1) * LANES, :].T

        def chunk_rows(s):
            return jnp.concatenate([rows_s[f, pl.ds(s, pg * rows_pp, stride=CMP_STRIDE), :]
                                    for f in range(halves)], axis=1).astype(BF16)

        acc = _dot(chunk_rows(0), w1_ref[0])
        for s in range(1, CMP_STRIDE):
            acc = acc + _dot(chunk_rows(s), w1_ref[s])
        c_s[k * pg * rows_pp:(k + 1) * pg * rows_pp, :] = acc

    nch = n_pages * rows_pp
    pec = _dot(pe_ref[...], w1_ref[...].reshape(CHUNK_W, w1_ref.shape[2]))
    kv = _compress_rows(c_s[...], pec, w2_ref[...]).astype(BF16)

    q = q_ref[0]
    c_end = lax.broadcasted_iota(I32, (1, nch), 1) * CMP_STRIDE + (CMP_BLOCK - 1)
    row = lax.broadcasted_iota(I32, (N_HEADS, 1), 0)
    width = smap_ref.shape[1]
    blk_i = lax.broadcasted_iota(I32, (1, width), 1)
    cur = q_pos // SEL_BLOCK
    forced = (blk_i == 0) | (blk_i == cur) | (blk_i == cur - 1)
    o_c = jnp.zeros((N_HEADS, HEAD_DIM), F32)
    score = jnp.full((N_HEADS, width), -jnp.inf, F32)
    for h in range(N_KV_HEADS):
        kc = kv[:, HEAD_DIM * h:HEAD_DIM * (h + 1)]
        vc = kv[:, HEAD_DIM * (N_KV_HEADS + h):HEAD_DIM * (N_KV_HEADS + h + 1)]
        p = _masked_softmax_rows(_dot_nt(q, kc), c_end <= q_pos).astype(BF16)
        in_group = (row // GROUP) == h
        o_c = jnp.where(in_group, _dot(p, vc), o_c)
        imp = jnp.sum(jnp.where(in_group, _dot(p, smap_ref[...]), 0.0), axis=0, keepdims=True)
        sc = jnp.where(blk_i > cur, -jnp.inf, jnp.where(forced, SEL_FORCE, imp))
        score = jnp.where(row == h, sc, score)
    oc_ref[0] = o_c

    n = min(N_SEL, n_sel_blocks)
    lane = lax.broadcasted_iota(I32, (1, LANES), 1)
    idx_out = jnp.zeros((N_HEADS, LANES), I32)
    blk = blk_i.astype(F32)
    for r in range(n):
        m = jnp.max(score, axis=-1, keepdims=True)
        idx = jnp.min(jnp.where(score == m, blk, float(width)), axis=-1, keepdims=True)
        score = jnp.where(blk == idx, -jnp.inf, score)
        idx_out = jnp.where(lane == r, idx.astype(I32), idx_out)
    idx_ref[0] = idx_out


CMP_PAGE_GROUP = 32
CMP_GROUP_SLOTS = 3


def _pages_feature_major(pool):
    n_pool, rows = pool.shape[0], pool.shape[1]
    return jnp.transpose(pool, (0, 2, 3, 4, 1)).reshape(n_pool, KV_WIDTH, rows)


def _nsa_sample_cmp(page_table, pool_t, q_seq, cw, past_len):
    bd, n_pages = page_table.shape
    w1, pe, w2 = cw
    w1 = w1.reshape(CMP_STRIDE, KV_WIDTH, w1.shape[1])
    rows_pp = PAGE_SIZE // CMP_STRIDE
    nch = n_pages * rows_pp
    pg = min(CMP_PAGE_GROUP, n_pages)
    assert n_pages % pg == 0
    n_sel_blocks = past_len // SEL_BLOCK + 1
    width = -(-n_sel_blocks // LANES) * LANES
    smap = _sel_map(nch, nch - 1, n_sel_blocks, width)
    grid_spec = pltpu.PrefetchScalarGridSpec(
        num_scalar_prefetch=1,
        grid=(bd,),
        in_specs=[pl.BlockSpec(memory_space=pl.ANY),
                  pl.BlockSpec((1, N_HEADS, HEAD_DIM), lambda i, pt: (i, 0, 0)),
                  pl.BlockSpec(w1.shape, lambda i, pt: (0, 0, 0)),
                  pl.BlockSpec(pe.shape, lambda i, pt: (0, 0)),
                  pl.BlockSpec(w2.shape, lambda i, pt: (0, 0)),
                  pl.BlockSpec(smap.shape, lambda i, pt: (0, 0))],
        out_specs=(pl.BlockSpec((1, N_HEADS, HEAD_DIM), lambda i, pt: (i, 0, 0)),
                   pl.BlockSpec((1, N_HEADS, LANES), lambda i, pt: (i, 0, 0))),
        scratch_shapes=[pltpu.VMEM((CMP_GROUP_SLOTS, pg, KV_WIDTH, PAGE_SIZE), F32),
                        pltpu.VMEM((KV_WIDTH // LANES, pg * PAGE_SIZE, LANES), F32),
                        pltpu.VMEM((nch, w1.shape[2]), F32),
                        pltpu.SemaphoreType.DMA((CMP_GROUP_SLOTS,))],
    )
    return pl.pallas_call(
        functools.partial(_nsa_sample_cmp_kernel, n_seq=bd, n_pages=n_pages, q_pos=past_len,
                          n_sel_blocks=n_sel_blocks),
        out_shape=(jax.ShapeDtypeStruct((bd, N_HEADS, HEAD_DIM), F32),
                   jax.ShapeDtypeStruct((bd, N_HEADS, LANES), I32)),
        grid_spec=grid_spec,
        compiler_params=_cparams(("arbitrary",)),
        name="nsa_sample_cmp",
    )(page_table.reshape(-1), pool_t, q_seq, w1, pe, w2, smap)


def _nsa_sample_sel_kernel(pt_ref, idx_s_ref, pool_ref, qr_ref, idx_ref, kvs_ref, kvw_ref, win_ref, oc_ref, g_ref,
                           o_ref, wout_ref, sbuf, sem, *, n_seq, n_pages, n_sel, ns_past):
    b = pl.program_id(0)
    per_page = PAGE_SIZE // SEL_BLOCK
    nblk = N_KV_HEADS * n_sel
    nkeys = n_sel * PAGE_SIZE

    def slab_copy(seq, slot, j):
        blk, kv = j // 2, j % 2
        jp = jnp.minimum(idx_s_ref[seq * nblk + blk], ns_past - 1)
        page = pt_ref[seq * n_pages + jp // per_page]
        feat = pl.multiple_of((kv * N_KV_HEADS + blk // n_sel) * HEAD_DIM, HEAD_DIM)
        return pltpu.make_async_copy(pool_ref.at[page, pl.ds(feat, HEAD_DIM), :], sbuf.at[slot, blk, kv],
                                     sem.at[slot])

    @pl.when(b == 0)
    def _():
        _row_gather(2 * nblk, lambda c: c.start(), lambda j: slab_copy(0, 0, j))

    @pl.when(b + 1 < n_seq)
    def _():
        _row_gather(2 * nblk, lambda c: c.start(), lambda j: slab_copy(b + 1, (b + 1) % 2, j))

    slot = b % 2
    _row_gather(2 * nblk, lambda c: c.wait(), lambda j: slab_copy(b, slot, j))

    qr = qr_ref[0]
    qf = qr.astype(F32)
    row = lax.broadcasted_iota(I32, (N_HEADS, 1), 0)
    key = lax.broadcasted_iota(I32, (1, nkeys), 1)
    key_blk = ((key % PAGE_SIZE) // SEL_BLOCK).astype(F32)
    first = (key % SEL_BLOCK) == 0
    expand = jnp.where(lax.broadcasted_iota(I32, (LANES, nkeys), 0)
                       == lax.broadcasted_iota(I32, (LANES, nkeys), 1) // PAGE_SIZE, 1.0, 0.0).astype(BF16)
    chosen = idx_ref[0]
    in_page = (jnp.minimum(chosen, ns_past - 1) % per_page).astype(F32).astype(BF16)
    want_blk = _dot(in_page, expand)
    is_new = _dot(jnp.where(chosen >= ns_past, 1.0, 0.0).astype(BF16), expand) > 0.5
    kvs_new = kvs_ref[0].astype(BF16).astype(F32)
    kvw_new = kvw_ref[0].astype(BF16).astype(F32)
    o_s = jnp.zeros((N_HEADS, HEAD_DIM), F32)
    o_w = jnp.zeros((N_HEADS, HEAD_DIM), F32)
    for h in range(N_KV_HEADS):
        in_group = (row // GROUP) == h
        ksl = slice(HEAD_DIM * h, HEAD_DIM * (h + 1))
        vsl = slice(HEAD_DIM * (N_KV_HEADS + h), HEAD_DIM * (N_KV_HEADS + h + 1))
        k_t = jnp.concatenate([sbuf[slot, h * n_sel + n, 0] for n in range(n_sel)], axis=1).astype(BF16)
        v_t = jnp.concatenate([sbuf[slot, h * n_sel + n, 1] for n in range(n_sel)], axis=1).astype(BF16)
        in_blk = want_blk[h:h + 1] == key_blk
        newblk = is_new[h:h + 1]
        newkey = in_blk & newblk & first
        s_new = jnp.sum(qf * kvs_new[:, ksl], axis=-1, keepdims=True)
        s = jnp.where(newkey, s_new, _dot(qr, k_t))
        p = _masked_softmax_rows(s, in_blk & (jnp.logical_not(newblk) | first))
        p_new = jnp.sum(jnp.where(newkey, p, 0.0), axis=-1, keepdims=True).astype(BF16).astype(F32)
        o_h = _dot_nt(jnp.where(newkey, 0.0, p).astype(BF16), v_t) + p_new * kvs_new[:, vsl]
        o_s = jnp.where(in_group, o_h, o_s)
        s_buf = _dot(qr, win_ref[0, ksl, :].astype(BF16))
        s_cur = jnp.sum(qf * kvw_new[:, ksl], axis=-1, keepdims=True)
        m = jnp.maximum(jnp.max(s_buf, axis=-1, keepdims=True), s_cur)
        e_buf = jnp.exp(s_buf - m)
        e_cur = jnp.exp(s_cur - m)
        den = jnp.maximum(jnp.sum(e_buf, axis=-1, keepdims=True) + e_cur, 1e-30)
        o_h = (_dot_nt((e_buf / den).astype(BF16), win_ref[0, vsl, :].astype(BF16))
               + (e_cur / den).astype(BF16).astype(F32) * kvw_new[:, vsl])
        o_w = jnp.where(in_group, o_h, o_w)

    gw = g_ref.shape[-1]
    gl = lax.broadcasted_iota(I32, (N_HEADS, gw), 1)
    gbase = (row // GROUP) * LANES + (row % GROUP) * 3
    gates = jnp.broadcast_to(g_ref[0], (N_HEADS, gw))
    gate = lambda j: jnp.sum(jnp.where(gl == gbase + j, gates, 0.0), axis=-1, keepdims=True)
    o_ref[0] = gate(0) * oc_ref[0] + gate(1) * o_s + gate(2) * o_w

    n_chunks = win_ref.shape[2] // LANES
    new_col = jnp.broadcast_to(kvw_ref[0], (LANES, KV_WIDTH)).T
    last_lane = lax.broadcasted_iota(I32, (1, LANES), 1) == LANES - 1
    for c in range(n_chunks):
        cur = win_ref[0, :, c * LANES:(c + 1) * LANES]
        nxt = win_ref[0, :, (c + 1) * LANES:(c + 2) * LANES] if c + 1 < n_chunks else new_col
        wout_ref[0, :, c * LANES:(c + 1) * LANES] = jnp.where(last_lane, pltpu.roll(nxt, LANES - 1, 1),
                                                             pltpu.roll(cur, LANES - 1, 1))


def _nsa_sample_sel(page_table, idx, pool_t, qr_seq, kvs_new, kvw_new, win_t, o_c, gates, past_len):
    bd, n_pages = page_table.shape
    n_sel = min(N_SEL, past_len // SEL_BLOCK + 1)
    n_buf = win_t.shape[2]
    assert n_buf % LANES == 0
    gw = gates.shape[-1]
    seq3 = lambda s1, s2: pl.BlockSpec((1, s1, s2), lambda i, pt, ix: (i, 0, 0))
    grid_spec = pltpu.PrefetchScalarGridSpec(
        num_scalar_prefetch=2,
        grid=(bd,),
        in_specs=[pl.BlockSpec(memory_space=pl.ANY), seq3(N_HEADS, HEAD_DIM), seq3(N_HEADS, LANES),
                  seq3(1, KV_WIDTH), seq3(1, KV_WIDTH), seq3(KV_WIDTH, n_buf), seq3(N_HEADS, HEAD_DIM),
                  seq3(1, gw)],
        out_specs=(seq3(N_HEADS, HEAD_DIM), seq3(KV_WIDTH, n_buf)),
        scratch_shapes=[pltpu.VMEM((2, N_KV_HEADS * n_sel, 2, HEAD_DIM, PAGE_SIZE), F32),
                        pltpu.SemaphoreType.DMA((2,))],
    )
    idx_flat = idx[:, :N_KV_HEADS, :n_sel].reshape(-1)
    return pl.pallas_call(
        functools.partial(_nsa_sample_sel_kernel, n_seq=bd, n_pages=n_pages, n_sel=n_sel,
                          ns_past=past_len // SEL_BLOCK),
        out_shape=(jax.ShapeDtypeStruct((bd, N_HEADS, HEAD_DIM), F32),
                   jax.ShapeDtypeStruct((bd, KV_WIDTH, n_buf), F32)),
        grid_spec=grid_spec,
        compiler_params=_cparams(("arbitrary",)),
        name="nsa_sample_sel",
    )(page_table.reshape(-1), idx_flat, pool_t, qr_seq, idx, kvs_new.reshape(bd, 1, KV_WIDTH),
      kvw_new.reshape(bd, 1, KV_WIDTH), win_t, o_c, gates.reshape(bd, 1, gw))


def _tail_kernel(x_ref, o_ref, y_ref, ga_ref, gb_ref, wa_ref, wb_ref, wo_ref, g1_ref, b1_ref, wr_ref, rb_ref,
                 cin_ref, x1_ref, idx_ref, w_ref, pos_ref, cnt_ref, carry_s, *, alpha):
    i = pl.program_id(0)
    tm = x_ref.shape[0]
    ne = wr_ref.shape[0]

    @pl.when(i == 0)
    def _():
        carry_s[...] = cin_ref[...]

    merged = (_sigmoid(ga_ref[...]) * _dot(o_ref[...], wa_ref[...])
              + _sigmoid(gb_ref[...]) * _dot(y_ref[...], wb_ref[...]))
    mix = _dot(merged.astype(BF16), wo_ref[...])
    x1 = _layer_norm(alpha * x_ref[...] + mix, g1_ref[...], b1_ref[...])
    x1_ref[...] = x1

    scores = _sigmoid(_dot_nt(wr_ref[...], x1.astype(BF16)))
    picked, hits = _top_rows(scores + rb_ref[...], TOP_K)
    expert = lax.broadcasted_iota(I32, (ne, tm), 0).astype(F32)
    chosen = [jnp.sum(jnp.where(expert == e_k, scores, 0.0), axis=0, keepdims=True) for e_k in picked]
    total = chosen[0]
    for c in chosen[1:]:
        total = total + c

    earlier = lax.broadcasted_iota(I32, (tm, tm), 0) < lax.broadcasted_iota(I32, (tm, tm), 1)
    prefix = _dot(hits.astype(BF16), jnp.where(earlier, 1.0, 0.0).astype(BF16)) + carry_s[...]
    out_row = lax.broadcasted_iota(I32, (idx_ref.shape[0], 1), 0)
    idx_out = jnp.zeros(idx_ref.shape, I32)
    w_out = jnp.zeros(w_ref.shape, F32)
    pos_out = jnp.zeros(pos_ref.shape, I32)
    for k in range(TOP_K):
        p_k = jnp.sum(jnp.where(expert == picked[k], prefix, 0.0), axis=0, keepdims=True)
        idx_out = jnp.where(out_row == k, picked[k].astype(I32), idx_out)
        pos_out = jnp.where(out_row == k, p_k.astype(I32), pos_out)
        w_out = jnp.where(out_row == k, chosen[k] / total * ROUTED_SCALE, w_out)
    idx_ref[...] = idx_out
    w_ref[...] = w_out
    pos_ref[...] = pos_out
    carry_s[...] = carry_s[...] + jnp.sum(hits, axis=1, keepdims=True)
    cnt_ref[...] = carry_s[...]


def _tail_weights(p):
    row = lambda v: v.reshape(1, -1).astype(F32)
    return (p['w_branch_attn'].astype(BF16), p['w_branch_rnn'].astype(BF16), p['w_out'].astype(BF16),
            row(p['ln1_g']), row(p['ln1_b']), p['w_router'].T.astype(BF16),
            p['router_bias'].reshape(-1, 1).astype(F32))


def _tail(x2d, o_attn, y_rnn, g_a, g_b, tw, counts_in, alpha, tm):
    m, d = x2d.shape
    ne = tw[5].shape[0]
    row = lambda w: pl.BlockSpec((tm, w), lambda i: (i, 0))
    slots = pl.BlockSpec((8, tm), lambda i: (0, i))
    small = jax.ShapeDtypeStruct((8, m), I32)
    return pl.pallas_call(
        functools.partial(_tail_kernel, alpha=alpha),
        out_shape=(jax.ShapeDtypeStruct((m, d), F32), small, jax.ShapeDtypeStruct((8, m), F32), small,
                   jax.ShapeDtypeStruct((ne, 1), F32)),
        grid=(m // tm,),
        in_specs=[row(d), row(o_attn.shape[1]), row(y_rnn.shape[1]), row(d), row(d)]
                 + [_full(w.shape) for w in tw] + [_full((ne, 1))],
        out_specs=(row(d), slots, slots, slots, _full((ne, 1))),
        scratch_shapes=[pltpu.VMEM((ne, 1), F32)],
        compiler_params=_cparams(("arbitrary",)),
        name="tail",
    )(x2d, o_attn, y_rnn, g_a, g_b, *tw, counts_in)


def _silu(x):
    return x * _sigmoid(x)


def _pack_bf16_pairs(x):
    half = x.shape[1] // 2
    bits = lax.bitcast_convert_type(x.astype(BF16).astype(F32), jnp.uint32)
    return bits[:, :half] | (bits[:, half:] >> 16)


def _dispatch_kernel(dest_ref, xp_ref, xs_ref, out_ref, wbuf, zbuf, sem, *, n_tok, tail):
    i = pl.program_id(0)
    n = pl.num_programs(0)
    tm = xp_ref.shape[0]
    n_sample = xs_ref.shape[0]
    slot = i % 2

    def scatter(tile, rows, slot_, go):
        def row_copy(k, t):
            dst = dest_ref[k * n_tok + tile * tm + t]
            return pltpu.make_async_copy(wbuf.at[slot_, pl.ds(t, 1)], out_ref.at[pl.ds(dst, 1)], sem.at[slot_])
        for k in range(TOP_K):
            _row_gather(rows, go, functools.partial(row_copy, k))

    if tail is not None:
        tail_copy = pltpu.make_async_copy(zbuf, out_ref.at[pl.ds(tail[0], tail[1])], sem.at[2])

        @pl.when(i == 0)
        def _():
            zbuf[...] = jnp.zeros(zbuf.shape, zbuf.dtype)
            tail_copy.start()

    @pl.when(i >= 2)
    def _():
        scatter(i - 2, tm, slot, lambda c: c.wait())

    @pl.when(i < n - 1)
    def _():
        wbuf[slot] = _pack_bf16_pairs(xp_ref[...])
        scatter(i, tm, slot, lambda c: c.start())

    @pl.when(i == n - 1)
    def _():
        wbuf[slot, 0:n_sample, :] = _pack_bf16_pairs(xs_ref[...])
        scatter(i, n_sample, slot, lambda c: c.start())
        scatter(i, n_sample, slot, lambda c: c.wait())
        if tail is not None:
            tail_copy.wait()

    @pl.when((i == n - 1) & (i >= 1))
    def _():
        scatter(i - 1, tm, 1 - slot, lambda c: c.wait())


def _dispatch(dest, x1_p, x1_s, n_rows_pad, tm):
    m, d = x1_p.shape
    n_rows = dest.shape[0]
    n_tiles = m // tm
    assert n_rows == (m + x1_s.shape[0]) * TOP_K and x1_s.shape[0] <= tm
    tail = None if n_rows_pad == n_rows else (n_rows, n_rows_pad - n_rows)
    grid_spec = pltpu.PrefetchScalarGridSpec(
        num_scalar_prefetch=1,
        grid=(n_tiles + 1,),
        in_specs=[pl.BlockSpec((tm, d), lambda i, ds: (jnp.minimum(i, n_tiles - 1), 0)),
                  pl.BlockSpec(x1_s.shape, lambda i, ds: (0, 0))],
        out_specs=pl.BlockSpec(memory_space=pl.ANY),
        scratch_shapes=[pltpu.VMEM((2, tm, d // 2), jnp.uint32),
                        pltpu.VMEM((8 if tail is None else tail[1], d // 2), jnp.uint32),
                        pltpu.SemaphoreType.DMA((3,))],
    )
    return pl.pallas_call(
        functools.partial(_dispatch_kernel, n_tok=n_rows // TOP_K, tail=tail),
        out_shape=jax.ShapeDtypeStruct((n_rows_pad, d // 2), jnp.uint32),
        grid_spec=grid_spec,
        compiler_params=_cparams(("arbitrary",)),
        name="dispatch",
    )(dest, x1_p, x1_s)


def _experts_kernel(vb_ref, ve_ref, lo_ref, hi_ref, xs_ref, wg_ref, wu_ref, wd_ref, y_ref, wgb, wub, wdb):
    v = pl.program_id(0)
    lo, hi = lo_ref[v], hi_ref[v]
    prev = jnp.maximum(v - 1, 0)
    first_of_block = (v == 0) | (vb_ref[v] != vb_ref[prev])

    @pl.when((v == 0) | (ve_ref[v] != ve_ref[prev]))
    def _():
        wgb[...] = wg_ref[0].astype(BF16)
        wub[...] = wu_ref[0].astype(BF16)
        wdb[...] = wd_ref[0].astype(BF16)

    @pl.when(hi > lo)
    def _():
        words = xs_ref[...]
        x = jnp.concatenate(
            [lax.bitcast_convert_type(words & jnp.uint32(0xFFFF0000), F32).astype(BF16),
             lax.bitcast_convert_type(words << 16, F32).astype(BF16)], axis=1)
        hid = _silu(_dot(x, wgb[...])) * _dot(x, wub[...])
        val = _dot(hid.astype(BF16), wdb[...])
        row = lax.broadcasted_iota(I32, (val.shape[0], 1), 0)
        mine = (row >= lo) & (row < hi)

        @pl.when(first_of_block)
        def _():
            y_ref[...] = jnp.where(mine, val, 0.0)

        @pl.when(jnp.logical_not(first_of_block))
        def _():
            y_ref[...] = jnp.where(mine, val, y_ref[...])


def _experts(xs, visits, w_gate, w_up, w_down):
    vblk, vexp, vlo, vhi = visits
    d, de = w_gate.shape[1], w_gate.shape[2]
    rb = EXPERT_ROWS
    grid_spec = pltpu.PrefetchScalarGridSpec(
        num_scalar_prefetch=4,
        grid=(vblk.shape[0],),
        in_specs=[pl.BlockSpec((rb, d // 2), lambda v, vb, ve, lo, hi: (vb[v], 0)),
                  pl.BlockSpec((1, d, de), lambda v, vb, ve, lo, hi: (ve[v], 0, 0)),
                  pl.BlockSpec((1, d, de), lambda v, vb, ve, lo, hi: (ve[v], 0, 0)),
                  pl.BlockSpec((1, de, d), lambda v, vb, ve, lo, hi: (ve[v], 0, 0))],
        out_specs=pl.BlockSpec((rb, d), lambda v, vb, ve, lo, hi: (vb[v], 0)),
        scratch_shapes=[pltpu.VMEM((d, de), BF16), pltpu.VMEM((d, de), BF16), pltpu.VMEM((de, d), BF16)],
    )
    return pl.pallas_call(
        _experts_kernel,
        out_shape=jax.ShapeDtypeStruct((xs.shape[0], d), F32),
        grid_spec=grid_spec,
        compiler_params=_cparams(("arbitrary",)),
        name="experts",
    )(vblk, vexp, vlo, vhi, xs, w_gate, w_up, w_down)


def _combine_kernel(dest_ref, x1_ref, w_ref, y_ref, wsg_ref, wsu_ref, wsd_ref, g2_ref, b2_ref, out_ref,
                    ybuf, sem, *, n_tok, tok_off, alpha):
    i = pl.program_id(0)
    n = pl.num_programs(0)
    tc = x1_ref.shape[0]

    def row_copy(tile, slot, k, t):
        src = dest_ref[k * n_tok + tok_off + tile * tc + t]
        return pltpu.make_async_copy(y_ref.at[pl.ds(src, 1)], ybuf.at[slot, k, pl.ds(t, 1)], sem.at[slot])

    def gather(tile, slot, go):
        for k in range(TOP_K):
            _row_gather(tc, go, lambda t: row_copy(tile, slot, k, t))

    @pl.when(i == 0)
    def _():
        gather(0, 0, lambda c: c.start())

    @pl.when(i + 1 < n)
    def _():
        gather(i + 1, (i + 1) % 2, lambda c: c.start())

    slot = i % 2
    gather(i, slot, lambda c: c.wait())
    x1 = x1_ref[...]
    w = w_ref[...]
    routed = w[:, 0:1] * ybuf[slot, 0]
    for k in range(1, TOP_K):
        routed = routed + w[:, k:k + 1] * ybuf[slot, k]
    xb = x1.astype(BF16)
    shared = _dot((_silu(_dot(xb, wsg_ref[...])) * _dot(xb, wsu_ref[...])).astype(BF16), wsd_ref[...])
    out_ref[...] = _layer_norm(alpha * x1 + (routed + shared), g2_ref[...], b2_ref[...])


def _combine_weights(p):
    row = lambda v: v.reshape(1, -1).astype(F32)
    return (p['w_sh_gate'].astype(BF16), p['w_sh_up'].astype(BF16), p['w_sh_down'].astype(BF16),
            row(p['ln2_g']), row(p['ln2_b']))


def _combine(dest, x1, w, y_rows, cw, tok_off, alpha, tc):
    m, d = x1.shape
    row = lambda width: pl.BlockSpec((tc, width), lambda i, ds: (i, 0))
    grid_spec = pltpu.PrefetchScalarGridSpec(
        num_scalar_prefetch=1,
        grid=(m // tc,),
        in_specs=[row(d), row(w.shape[1]), pl.BlockSpec(memory_space=pl.ANY)]
                 + [pl.BlockSpec(a.shape, lambda i, ds: (0, 0)) for a in cw],
        out_specs=row(d),
        scratch_shapes=[pltpu.VMEM((2, TOP_K, tc, d), F32), pltpu.SemaphoreType.DMA((2,))],
    )
    return pl.pallas_call(
        functools.partial(_combine_kernel, n_tok=dest.shape[0] // TOP_K, tok_off=tok_off, alpha=alpha),
        out_shape=jax.ShapeDtypeStruct((m, d), F32),
        grid_spec=grid_spec,
        compiler_params=_cparams(("arbitrary",)),
        name="combine",
    )(dest, x1, w, y_rows, *cw)


def _route(idx, pos, counts):
    n_tok = idx.shape[1]
    ne = counts.shape[0]
    rb = EXPERT_ROWS
    n_blocks = -(-n_tok * TOP_K // rb)
    end = jnp.cumsum(counts)
    start = end - counts
    dest = start[idx] + pos
    first_blk = start // rb
    n_vis = jnp.where(counts > 0, (end - 1) // rb - first_blk + 1, 0)
    vend = jnp.cumsum(n_vis)
    vstart = vend - n_vis
    v = jnp.arange(n_blocks + ne - 1, dtype=I32)
    valid = v < vend[-1]
    v_c = jnp.minimum(v, vend[-1] - 1)
    vexp = jnp.minimum(jnp.sum((vend[None, :] <= v_c[:, None]).astype(I32), axis=1), ne - 1)
    vblk = first_blk[vexp] + (v_c - vstart[vexp])
    vlo = jnp.where(valid, jnp.clip(start[vexp] - vblk * rb, 0, rb), 0)
    vhi = jnp.where(valid, jnp.clip(end[vexp] - vblk * rb, 0, rb), 0)
    visits = tuple(a.astype(I32) for a in (vblk, vexp, vlo, vhi))
    return dest.reshape(-1).astype(I32), visits, n_blocks * rb


def _layer(xp, xs, caches, page_table, p, depth):
    b, t, d = xp.shape
    bd, s, _ = xs.shape
    assert s == 1, "sample group is one new token per sequence"
    pool_c, pool_s, win_buf, state_conv, state_rnn = caches
    past_len = page_table.shape[1] * PAGE_SIZE
    alpha = (2.0 * depth) ** 0.25
    kv6 = lambda a, n, rows: a.reshape(n, rows, 2, N_KV_HEADS, HEAD_DIM)
    rows_major = lambda a: jnp.transpose(a.reshape(a.shape[0], 2, N_KV_HEADS, HEAD_DIM, a.shape[2]), (0, 4, 1, 2, 3))

    wparts = _split_w_in(p['w_in'])
    cw = _compress_weights(p)
    rw = _rglru_weights(p)
    tw = _tail_weights(p)
    mw = _combine_weights(p)

    pos_p = jnp.tile(jnp.arange(t, dtype=I32), b)
    (q, qr, kvc, kvc_t, kvs_t, kvw_t, ksh, vsh, kwh, vwh, gates, u_rnn, u_gate, g_a, g_b) = _proj(
        xp.reshape(b * t, d), pos_p, wparts, 256, seq_len=t)
    kc, vc = _compress_prompt(kvc, b, t, cw)
    o_attn = _nsa_prompt(q, qr, kc, vc, ksh, vsh, kwh, vwh, gates, b, t)
    y_rnn, h_p = _rglru_prompt(u_rnn, u_gate, rw, b, t, 256)
    ne = p['w_router'].shape[1]
    x1_p, idx_p, w_p, pos_r_p, counts_p = _tail(xp.reshape(b * t, d), o_attn, y_rnn.reshape(b * t, -1), g_a, g_b,
                                                tw, jnp.zeros((ne, 1), F32), alpha, 256)
    wn = min(WINDOW, t)
    outs_p = (rows_major(kvc_t), rows_major(kvs_t), rows_major(kvw_t[:, :, t - wn:]),
              u_rnn.reshape(b, t, -1)[:, t - (CONV_W - 1):], h_p.reshape(b, -1))

    pos_s = jnp.full((bd,), past_len, I32)
    (q, qr, kvc_s, kvs_s, kvw_s, _, _, _, _, gates_s, u_rnn_s, u_gate_s, g_a_s, g_b_s) = _proj(
        xs.reshape(bd, d), pos_s, wparts, bd)
    o_c, sel_idx = _nsa_sample_cmp(page_table, _pages_feature_major(pool_c), q.transpose(1, 0, 2), cw, past_len)
    o_s, win_new_t = _nsa_sample_sel(page_table, sel_idx, _pages_feature_major(pool_s), qr.transpose(1, 0, 2),
                                     kvs_s, kvw_s, _pages_feature_major(win_buf), o_c, gates_s, past_len)
    y_rnn_s, h_s = _rglru_step(u_rnn_s, u_gate_s, state_conv.transpose(1, 0, 2), state_rnn, rw)
    x1_s, idx_s, w_s, pos_r_s, counts = _tail(xs.reshape(bd, d), o_s.reshape(bd, Q_WIDTH).astype(BF16), y_rnn_s,
                                              g_a_s, g_b_s, tw, counts_p, alpha, bd)
    conv_s = jnp.concatenate([state_conv[:, 1:], u_rnn_s[:, None, :]], axis=1)
    outs_s = (kv6(kvc_s, bd, 1), kv6(kvs_s, bd, 1), rows_major(win_new_t), conv_s, h_s)

    idx_all = jnp.concatenate([idx_p, idx_s], axis=1)[:TOP_K]
    pos_all = jnp.concatenate([pos_r_p, pos_r_s], axis=1)[:TOP_K]
    dest, visits, n_rows_pad = _route(idx_all, pos_all, counts.reshape(-1).astype(I32))
    xs = _dispatch(dest, x1_p, x1_s, n_rows_pad, 256)
    y_rows = _experts(xs, visits, p['w_exp_gate'], p['w_exp_up'], p['w_exp_down'])
    yp = _combine(dest, x1_p, w_p.T, y_rows, mw, 0, alpha, 128)
    ys = _combine(dest, x1_s, w_s.T, y_rows, mw, b * t, alpha, bd)
    return yp.reshape(b, t, d), ys.reshape(bd, s, d), outs_p, outs_s


def kernel(x_prompt, x_sample, cache_cmp_kv, cache_sel_kv, cache_win_kv, state_conv, state_rnn, page_table,
           w_in, conv_w, conv_b, w_rg_a, b_rg_a, w_rg_i, b_rg_i, lru_lambda, cmp_pos_k, cmp_pos_v,
           w_cmp_k1, w_cmp_k2, w_cmp_v1, w_cmp_v2, w_branch_attn, w_branch_rnn, w_out, ln1_g, ln1_b,
           w_router, router_bias, w_exp_gate, w_exp_up, w_exp_down, w_sh_gate, w_sh_up, w_sh_down, ln2_g, ln2_b):
    weights = dict(w_in=w_in, conv_w=conv_w, conv_b=conv_b, w_rg_a=w_rg_a, b_rg_a=b_rg_a, w_rg_i=w_rg_i,
                   b_rg_i=b_rg_i, lru_lambda=lru_lambda, cmp_pos_k=cmp_pos_k, cmp_pos_v=cmp_pos_v,
                   w_cmp_k1=w_cmp_k1, w_cmp_k2=w_cmp_k2, w_cmp_v1=w_cmp_v1, w_cmp_v2=w_cmp_v2,
                   w_branch_attn=w_branch_attn, w_branch_rnn=w_branch_rnn, w_out=w_out, ln1_g=ln1_g, ln1_b=ln1_b,
                   w_router=w_router, router_bias=router_bias, w_exp_gate=w_exp_gate, w_exp_up=w_exp_up,
                   w_exp_down=w_exp_down, w_sh_gate=w_sh_gate, w_sh_up=w_sh_up, w_sh_down=w_sh_down,
                   ln2_g=ln2_g, ln2_b=ln2_b)
    depth = w_in.shape[0]
    xp, xs = x_prompt, x_sample
    per_layer_p, per_layer_s = [], []
    for l in range(depth):
        p = {k: v[l] for k, v in weights.items()}
        caches = (cache_cmp_kv[l], cache_sel_kv[l], cache_win_kv[l], state_conv[l], state_rnn[l])
        xp, xs, outs_p, outs_s = _layer(xp, xs, caches, page_table, p, depth)
        per_layer_p.append(outs_p)
        per_layer_s.append(outs_s)
    stack = lambda outs, i: jnp.stack([o[i] for o in outs])
    return (xp, xs, stack(per_layer_p, 0), stack(per_layer_s, 0), stack(per_layer_p, 1), stack(per_layer_s, 1),
            stack(per_layer_p, 2), stack(per_layer_s, 2), stack(per_layer_p, 3), stack(per_layer_s, 3),
            stack(per_layer_p, 4), stack(per_layer_s, 4))
```

```python
import functools

import numpy as np
import jax
import jax.numpy as jnp
from jax import lax
from jax.experimental import pallas as pl
from jax.experimental.pallas import tpu as pltpu

F32 = jnp.float32
BF16 = jnp.bfloat16
I32 = jnp.int32

N_HEADS = 8
HEAD_DIM = 64
N_KV_HEADS = 2
GROUP = N_HEADS // N_KV_HEADS
ROT_DIM = HEAD_DIM // 4
ROPE_THETA = 500000.0
CMP_BLOCK = 32
CMP_STRIDE = 16
CMP_HIDDEN = 64
SEL_BLOCK = 64
N_SEL = 16
SEL_FORCE = 1.0e6
WINDOW = 512
Q_BLOCK = 128
RNN_BLOCKS = 8
CONV_W = 4
RG_C = 8.0
TOP_K = 6
ROUTED_SCALE = 2.5
LN_EPS = 1e-5
PAGE_SIZE = 128

Q_WIDTH = N_HEADS * HEAD_DIM
KV_WIDTH = 2 * N_KV_HEADS * HEAD_DIM
N_GROUPS = 2 * N_KV_HEADS
CHUNK_W = CMP_STRIDE * KV_WIDTH
LANES = 128
NEG = -1.0e30
VMEM_LIMIT = 56 * 1024 * 1024

EXPERT_ROWS = 512
SEL_TK = 512


def _cparams(sem):
    return pltpu.CompilerParams(dimension_semantics=sem, vmem_limit_bytes=VMEM_LIMIT)


def _full(shape):
    n = len(shape)
    return pl.BlockSpec(shape, lambda *a: (0,) * n)


def _dot(a, b):
    return jnp.dot(a, b, preferred_element_type=F32)


def _dot_nt(a, b):
    return lax.dot_general(a, b, (((1,), (1,)), ((), ())), preferred_element_type=F32)


def _gelu(x):
    return 0.5 * x * (1.0 + jnp.tanh(np.sqrt(2.0 / np.pi) * (x + 0.044715 * (x * x * x))))


def _sigmoid(x):
    return 1.0 / (1.0 + jnp.exp(-x))


def _layer_norm(z, g, b):
    mu = jnp.mean(z, axis=-1, keepdims=True)
    d = z - mu
    var = jnp.mean(d * d, axis=-1, keepdims=True)
    return d * lax.rsqrt(var + LN_EPS) * g + b


def _rope_tables(pos):
    half = ROT_DIM // 2
    inv = ROPE_THETA ** (-(jnp.arange(half, dtype=F32) * (2.0 / ROT_DIM)))
    ang = pos.astype(F32)[:, None] * inv[None, :]
    cos, sin = jnp.cos(ang), jnp.sin(ang)
    m = pos.shape[0]
    one = jnp.ones((m, HEAD_DIM - ROT_DIM), F32)
    zero = jnp.zeros((m, HEAD_DIM - ROT_DIM), F32)
    zh = jnp.zeros((m, half), F32)
    c = jnp.concatenate([cos, cos, one], axis=1)
    s1 = jnp.concatenate([-sin, zh, zero], axis=1)
    s2 = jnp.concatenate([zh, sin, zero], axis=1)
    rep = LANES // HEAD_DIM
    return jnp.tile(c, (1, rep)), jnp.tile(s1, (1, rep)), jnp.tile(s2, (1, rep))


def _rope128(x, c, s1, s2):
    half = ROT_DIM // 2
    return x * c + pltpu.roll(x, LANES - half, 1) * s1 + pltpu.roll(x, half, 1) * s2


def _proj_kernel(x_ref, c_ref, s1_ref, s2_ref, wq_ref, wkv_ref, wg_ref, ww_ref, *out_refs, transposed):
    if transposed:
        (q_ref, qr_ref, kvc_ref, kvct_ref, kvst_ref, kvwt_ref, ksh_ref, vsh_ref, kwh_ref, vwh_ref,
         gate_ref, urnn_ref, ugate_ref, ga_ref, gb_ref) = out_refs
        kvs_ref = kvw_ref = None
    else:
        (q_ref, qr_ref, kvc_ref, kvs_ref, kvw_ref, ksh_ref, vsh_ref, kwh_ref, vwh_ref,
         gate_ref, urnn_ref, ugate_ref, ga_ref, gb_ref) = out_refs
        kvct_ref = kvst_ref = kvwt_ref = None
    xb = x_ref[...].astype(BF16)
    c, s1, s2 = c_ref[...], s1_ref[...], s2_ref[...]
    scale = HEAD_DIM ** -0.5
    q = _dot(xb, wq_ref[...]) * scale
    for j in range(Q_WIDTH // LANES):
        ch = q[:, LANES * j:LANES * (j + 1)]
        rot = _rope128(ch, c, s1, s2)
        for hh in range(LANES // HEAD_DIM):
            head = j * (LANES // HEAD_DIM) + hh
            q_ref[head] = ch[:, HEAD_DIM * hh:HEAD_DIM * (hh + 1)].astype(BF16)
            qr_ref[head] = rot[:, HEAD_DIM * hh:HEAD_DIM * (hh + 1)].astype(BF16)
    kv = _dot(xb, wkv_ref[...])
    kvc_ref[...] = kv[:, :KV_WIDTH]
    if transposed:
        kvct_ref[0] = kv[:, :KV_WIDTH].T
    for base, full_ref, t_ref, kh_ref, vh_ref in ((KV_WIDTH, kvs_ref, kvst_ref, ksh_ref, vsh_ref),
                                                  (2 * KV_WIDTH, kvw_ref, kvwt_ref, kwh_ref, vwh_ref)):
        keys = _rope128(kv[:, base:base + LANES], c, s1, s2)
        vals = kv[:, base + LANES:base + 2 * LANES]
        if transposed:
            t_ref[0, :LANES, :] = keys.T
            t_ref[0, LANES:, :] = vals.T
        else:
            full_ref[:, :LANES] = keys
            full_ref[:, LANES:] = vals
        for h in range(N_KV_HEADS):
            kh_ref[h] = keys[:, HEAD_DIM * h:HEAD_DIM * (h + 1)].astype(BF16)
            vh_ref[h] = vals[:, HEAD_DIM * h:HEAD_DIM * (h + 1)].astype(BF16)
    gate_ref[...] = _sigmoid(_dot(xb, wg_ref[...]))
    d = urnn_ref.shape[-1]
    for k, ref in enumerate((urnn_ref, ugate_ref, ga_ref, gb_ref)):
        ref[...] = _dot(xb, ww_ref[:, d * k:d * (k + 1)])


def _split_w_in(w_in):
    d = w_in.shape[0]
    o = 0
    wq = w_in[:, o:o + Q_WIDTH]; o += Q_WIDTH
    wkv = w_in[:, o:o + 3 * KV_WIDTH]; o += 3 * KV_WIDTH
    wg = w_in[:, o:o + 3 * N_HEADS]; o += 3 * N_HEADS
    ww = w_in[:, o:]
    per = 3 * GROUP
    wg2 = jnp.zeros((d, N_KV_HEADS * LANES), w_in.dtype)
    for h in range(N_KV_HEADS):
        wg2 = wg2.at[:, h * LANES:h * LANES + per].set(wg[:, h * per:(h + 1) * per])
    return wq.astype(BF16), wkv.astype(BF16), wg2.astype(BF16), ww.astype(BF16)


def _proj(x2d, pos, wparts, tm, seq_len=None):
    m, d = x2d.shape
    wq, wkv, wg, ww = wparts
    c, s1, s2 = _rope_tables(pos)
    dw = ww.shape[1] // 4
    row = lambda w: pl.BlockSpec((tm, w), lambda i: (i, 0))
    hm = lambda n: pl.BlockSpec((n, tm, HEAD_DIM), lambda i: (0, i, 0))
    rows_f32 = jax.ShapeDtypeStruct((m, KV_WIDTH), F32)
    hm_q = jax.ShapeDtypeStruct((N_HEADS, m, HEAD_DIM), BF16)
    hm_kv = jax.ShapeDtypeStruct((N_KV_HEADS, m, HEAD_DIM), BF16)
    wide = jax.ShapeDtypeStruct((m, dw), F32)
    if seq_len is None:
        kv_shapes = (rows_f32, rows_f32, rows_f32)
        kv_specs = (row(KV_WIDTH),) * 3
    else:
        nt = seq_len // tm
        kv_t = jax.ShapeDtypeStruct((m // seq_len, KV_WIDTH, seq_len), F32)
        t_spec = pl.BlockSpec((1, KV_WIDTH, tm), lambda i: (i // nt, 0, i % nt))
        kv_shapes = (rows_f32, kv_t, kv_t, kv_t)
        kv_specs = (row(KV_WIDTH), t_spec, t_spec, t_spec)
    out_shape = (hm_q, hm_q) + kv_shapes + (hm_kv,) * 4 + (
        jax.ShapeDtypeStruct((m, N_KV_HEADS * LANES), F32), wide, wide, wide, wide)
    out_specs = (hm(N_HEADS), hm(N_HEADS)) + kv_specs + (hm(N_KV_HEADS),) * 4 + (
        row(N_KV_HEADS * LANES), row(dw), row(dw), row(dw), row(dw))
    return pl.pallas_call(
        functools.partial(_proj_kernel, transposed=seq_len is not None),
        out_shape=out_shape,
        grid=(m // tm,),
        in_specs=[row(d), row(LANES), row(LANES), row(LANES),
                  _full(wq.shape), _full(wkv.shape), _full(wg.shape), _full(ww.shape)],
        out_specs=out_specs,
        compiler_params=_cparams(("parallel",)),
        name="proj",
    )(x2d, c, s1, s2, wq, wkv, wg, ww)


def _compress_weights(p):
    parts = CMP_BLOCK // CMP_STRIDE
    gw = N_GROUPS * CMP_HIDDEN
    w1 = jnp.zeros((CMP_STRIDE, N_GROUPS, HEAD_DIM, parts, N_GROUPS, CMP_HIDDEN), F32)
    pe = jnp.zeros((8, CMP_STRIDE, N_GROUPS, HEAD_DIM), F32)
    w2 = jnp.zeros((N_GROUPS, CMP_HIDDEN, N_GROUPS, HEAD_DIM), F32)
    for g in range(N_GROUPS):
        kv = g // N_KV_HEADS
        w1_src = (p['w_cmp_k1'], p['w_cmp_v1'])[kv].reshape(parts, CMP_STRIDE, HEAD_DIM, CMP_HIDDEN)
        pe_src = (p['cmp_pos_k'], p['cmp_pos_v'])[kv].reshape(parts, CMP_STRIDE, HEAD_DIM)
        w2_src = (p['w_cmp_k2'], p['w_cmp_v2'])[kv]
        for m in range(parts):
            w1 = w1.at[:, g, :, m, g, :].set(w1_src[m])
            pe = pe.at[m, :, g, :].set(pe_src[m])
        w2 = w2.at[g, :, g, :].set(w2_src)
    return (w1.reshape(CHUNK_W, parts * gw).astype(BF16), pe.reshape(8, CHUNK_W).astype(BF16),
            w2.reshape(gw, N_GROUPS * HEAD_DIM).astype(BF16))


def _compress_rows(c, pec, w2):
    gw = N_GROUPS * CMP_HIDDEN
    n = c.shape[0]
    const = pec[0:1, :gw] + pec[1:2, gw:]
    hid = c[:, :gw] + pltpu.roll(c[:, gw:], n - 1, 0) + const
    return _dot(_gelu(hid).astype(BF16), w2)


def _compress_prompt_kernel(ch_ref, w1_ref, pe_ref, w2_ref, kc_ref, vc_ref):
    c = _dot(ch_ref[0].astype(BF16), w1_ref[...])
    pec = _dot(pe_ref[...], w1_ref[...])
    out = _compress_rows(c, pec, w2_ref[...])
    for h in range(N_KV_HEADS):
        kc_ref[0, h] = out[:, HEAD_DIM * h:HEAD_DIM * (h + 1)].astype(BF16)
        vc_ref[0, h] = out[:, HEAD_DIM * (N_KV_HEADS + h):HEAD_DIM * (N_KV_HEADS + h + 1)].astype(BF16)


def _compress_prompt(kvc, b, t, cw):
    w1, pe, w2 = cw
    nch = t // CMP_STRIDE
    chunks = kvc.reshape(b, nch, CHUNK_W)
    out = jax.ShapeDtypeStruct((b, N_KV_HEADS, nch, HEAD_DIM), BF16)
    ospec = pl.BlockSpec((1, N_KV_HEADS, nch, HEAD_DIM), lambda i: (i, 0, 0, 0))
    return pl.pallas_call(
        _compress_prompt_kernel,
        out_shape=(out, out),
        grid=(b,),
        in_specs=[pl.BlockSpec((1, nch, CHUNK_W), lambda i: (i, 0, 0)),
                  _full(w1.shape), _full(pe.shape), _full(w2.shape)],
        out_specs=(ospec, ospec),
        compiler_params=_cparams(("parallel",)),
        name="compress_prompt",
    )(chunks, w1, pe, w2)


def _sel_map(n_cmp_pad, n_cmp, n_sel, n_sel_pad):
    ratio = SEL_BLOCK // CMP_STRIDE
    i = np.arange(n_cmp_pad)[:, None]
    j = np.arange(n_sel_pad)[None, :]
    d = i - ratio * j
    m = sum(((d + n >= 0) & (d + n < ratio)).astype(np.float32) for n in range(CMP_BLOCK // CMP_STRIDE))
    m = m * (i < n_cmp) * (j < n_sel)
    return jnp.asarray(m, BF16)


def _masked_softmax_rows(s, valid):
    s = jnp.where(valid, s, -jnp.inf)
    m = jnp.max(s, axis=-1, keepdims=True)
    m = jnp.where(m == -jnp.inf, 0.0, m)
    e = jnp.exp(s - m)
    return e / jnp.maximum(jnp.sum(e, axis=-1, keepdims=True), 1e-30)


def _top_rows(score, n):
    rows = score.shape[0]
    row = lax.broadcasted_iota(I32, score.shape, 0).astype(F32)
    sel = jnp.zeros(score.shape, F32)
    picked = []
    for _ in range(n):
        m = jnp.max(score, axis=0, keepdims=True)
        idx = jnp.min(jnp.where(score == m, row, float(rows)), axis=0, keepdims=True)
        pick = row == idx
        picked.append(idx)
        sel = jnp.where(pick, 1.0, sel)
        score = jnp.where(pick, -jnp.inf, score)
    return picked, sel


def _nsa_prompt_kernel(q_ref, qr_ref, kc_ref, vc_ref, ks_ref, vs_ref, kw_ref, vw_ref, g_ref, smap_ref,
                       o_ref, *, n_sel_blocks):
    qb = pl.program_id(2)
    s0 = qb * Q_BLOCK
    rows = GROUP * Q_BLOCK
    q = q_ref[...].reshape(rows, HEAD_DIM)
    qr = qr_ref[...].reshape(rows, HEAD_DIM)
    pos1 = s0 + lax.broadcasted_iota(I32, (Q_BLOCK, 1), 0)
    pos = s0 + (lax.broadcasted_iota(I32, (rows, 1), 0) & (Q_BLOCK - 1))

    kc = kc_ref[0, 0]
    ncp = kc.shape[0]
    c_end = lax.broadcasted_iota(I32, (1, ncp), 1) * CMP_STRIDE + (CMP_BLOCK - 1)
    p_c = _masked_softmax_rows(_dot_nt(q, kc), c_end <= pos).astype(BF16)
    o_c = _dot(p_c, vc_ref[0, 0])
    imp_t = _dot_nt(smap_ref[...], p_c[0:Q_BLOCK])
    for g in range(1, GROUP):
        imp_t = imp_t + _dot_nt(smap_ref[...], p_c[g * Q_BLOCK:(g + 1) * Q_BLOCK])

    wlen = WINDOW + Q_BLOCK
    wstart = pl.multiple_of(jnp.maximum(s0 - WINDOW, 0), Q_BLOCK)
    kpos = wstart + lax.broadcasted_iota(I32, (1, wlen), 1)
    diff = pos - kpos
    p_w = _masked_softmax_rows(_dot_nt(qr, kw_ref[0, pl.ds(wstart, wlen), :]), (diff >= 0) & (diff <= WINDOW))
    o_w = _dot(p_w.astype(BF16), vw_ref[0, pl.ds(wstart, wlen), :])

    blk_t = lax.broadcasted_iota(I32, (LANES, Q_BLOCK), 0)
    cur_t = (s0 + lax.broadcasted_iota(I32, (1, Q_BLOCK), 1)) // SEL_BLOCK
    causal_t = blk_t <= cur_t
    forced_t = ((blk_t == 0) | (blk_t == cur_t) | (blk_t == cur_t - 1)) & causal_t
    n_forced = 3
    _, ranked_t = _top_rows(jnp.where(causal_t & jnp.logical_not(forced_t), imp_t, -jnp.inf),
                            min(N_SEL, n_sel_blocks) - n_forced)
    sel_t = jnp.where(forced_t | (causal_t & (ranked_t > 0.5)), 1.0, 0.0)
    sel = sel_t.T.astype(BF16)

    per_tile = SEL_TK // SEL_BLOCK
    jrow = lax.broadcasted_iota(I32, (LANES, SEL_TK), 0)
    cblk = lax.broadcasted_iota(I32, (LANES, SEL_TK), 1) // SEL_BLOCK
    kcol = lax.broadcasted_iota(I32, (1, SEL_TK), 1)

    def sel_tile(kt, carry, causal):
        m_i, l_i, acc = carry
        start = pl.multiple_of(kt * SEL_TK, SEL_TK)
        s = _dot_nt(qr, ks_ref[0, pl.ds(start, SEL_TK), :]).reshape(GROUP, Q_BLOCK, SEL_TK)
        expand = jnp.where(jrow - kt * per_tile == cblk, 1.0, 0.0).astype(BF16)
        ok = _dot(sel, expand) > 0.5
        if causal:
            ok = ok & (start + kcol <= pos1)
        s = jnp.where(ok[None], s, NEG)
        m_new = jnp.maximum(m_i, jnp.max(s, axis=-1, keepdims=True))
        alpha = jnp.exp(m_i - m_new)
        p = jnp.exp(s - m_new)
        l_new = alpha * l_i + jnp.sum(p, axis=-1, keepdims=True)
        pv = _dot(p.reshape(rows, SEL_TK).astype(BF16), vs_ref[0, pl.ds(start, SEL_TK), :])
        return m_new, l_new, alpha * acc + pv.reshape(GROUP, Q_BLOCK, HEAD_DIM)

    init = (jnp.full((GROUP, Q_BLOCK, 1), NEG, F32), jnp.zeros((GROUP, Q_BLOCK, 1), F32),
            jnp.zeros((GROUP, Q_BLOCK, HEAD_DIM), F32))
    n_full = s0 // SEL_TK
    carry = lax.fori_loop(0, n_full, lambda kt, cr: sel_tile(kt, cr, False), init)
    _, l_s, acc_s = sel_tile(n_full, carry, True)
    o_s = (acc_s / l_s).reshape(rows, HEAD_DIM)

    gates = g_ref[...]
    outs = []
    for g in range(GROUP):
        r = slice(g * Q_BLOCK, (g + 1) * Q_BLOCK)
        outs.append(gates[:, 3 * g:3 * g + 1] * o_c[r] + gates[:, 3 * g + 1:3 * g + 2] * o_s[r]
                    + gates[:, 3 * g + 2:3 * g + 3] * o_w[r])
    o_ref[...] = jnp.concatenate(outs, axis=1).astype(o_ref.dtype)


def _nsa_prompt(q_hm, qr_hm, kc, vc, ksh, vsh, kwh, vwh, gates, b, t):
    nqb = t // Q_BLOCK
    n_sel_blocks = t // SEL_BLOCK
    ncp = kc.shape[2]
    assert N_SEL <= n_sel_blocks <= LANES and t % SEL_TK == 0 and t >= WINDOW + Q_BLOCK
    smap = _sel_map(ncp, ncp - 1, n_sel_blocks, LANES).T
    qspec = pl.BlockSpec((GROUP, Q_BLOCK, HEAD_DIM), lambda bi, h, i: (h, bi * nqb + i, 0))
    cspec = pl.BlockSpec((1, 1, ncp, HEAD_DIM), lambda bi, h, i: (bi, h, 0, 0))
    kspec = pl.BlockSpec((1, t, HEAD_DIM), lambda bi, h, i: (h, bi, 0))
    return pl.pallas_call(
        functools.partial(_nsa_prompt_kernel, n_sel_blocks=n_sel_blocks),
        out_shape=jax.ShapeDtypeStruct((b * t, Q_WIDTH), BF16),
        grid=(b, N_KV_HEADS, nqb),
        in_specs=[qspec, qspec, cspec, cspec, kspec, kspec, kspec, kspec,
                  pl.BlockSpec((Q_BLOCK, LANES), lambda bi, h, i: (bi * nqb + i, h)),
                  _full(smap.shape)],
        out_specs=pl.BlockSpec((Q_BLOCK, GROUP * HEAD_DIM), lambda bi, h, i: (bi * nqb + i, h)),
        compiler_params=_cparams(("parallel", "parallel", "arbitrary")),
        name="nsa_prompt",
    )(q_hm, qr_hm, kc, vc, ksh, vsh, kwh, vwh, gates, smap)


def _expm1(x):
    series = x * (1.0 + x * (1.0 / 2 + x * (1.0 / 6 + x * (1.0 / 24 + x * (1.0 / 120 + x * (1.0 / 720))))))
    return jnp.where(jnp.abs(x) < 0.1, series, jnp.exp(x) - 1.0)


def _log1p(z):
    series = z * (1.0 - z * (1.0 / 2 - z * (1.0 / 3 - z * (1.0 / 4 - z * (1.0 / 5 - z * (1.0 / 6 - z * (
        1.0 / 7 - z * (1.0 / 8))))))))
    return jnp.where(z < 0.1, series, jnp.log(1.0 + z))


def _rglru_gates(xc, ug, wa_ref, ba, wi_ref, bi, lam):
    xcb = xc.astype(BF16)
    bw = xc.shape[-1] // RNN_BLOCKS
    ra = jnp.concatenate([_dot(xcb[:, bw * n:bw * (n + 1)], wa_ref[n]) for n in range(RNN_BLOCKS)], axis=1)
    ri = jnp.concatenate([_dot(xcb[:, bw * n:bw * (n + 1)], wi_ref[n]) for n in range(RNN_BLOCKS)], axis=1)
    r = _sigmoid(ra + ba)
    i = _sigmoid(ri + bi)
    softplus = jnp.maximum(-lam, 0.0) + _log1p(jnp.exp(-jnp.abs(lam)))
    log_a = -RG_C * r * softplus
    a = jnp.exp(log_a)
    b = jnp.sqrt(-_expm1(2.0 * log_a)) * (i * xc)
    return a, b, _gelu(ug)


def _rglru_prompt_kernel(u_ref, ug_ref, cw_ref, cb_ref, wa_ref, ba_ref, wi_ref, bi_ref, lam_ref,
                         y_ref, h_ref, up_s, a_s, b_s, h_s):
    tt = u_ref.shape[1]
    t = pl.program_id(1)
    halo = CONV_W - 1

    @pl.when(t == 0)
    def _():
        up_s[0:8, :] = jnp.zeros((8, up_s.shape[1]), F32)
        h_s[...] = jnp.zeros(h_s.shape, F32)

    @pl.when(t > 0)
    def _():
        up_s[8 - halo:8, :] = up_s[8 + tt - halo:8 + tt, :]

    up_s[8:8 + tt, :] = u_ref[0]
    xc = cb_ref[...] + up_s[pl.ds(8 - halo, tt), :] * cw_ref[0:1, :]
    for k in range(1, CONV_W):
        xc = xc + up_s[pl.ds(8 - halo + k, tt), :] * cw_ref[k:k + 1, :]
    a, b, gate = _rglru_gates(xc, ug_ref[0], wa_ref, ba_ref[...], wi_ref, bi_ref[...], lam_ref[...])
    a_s[...] = a
    b_s[...] = b

    def step(i, h):
        h = a_s[pl.ds(i, 1), :] * h + b_s[pl.ds(i, 1), :]
        b_s[pl.ds(i, 1), :] = h
        return h

    h = lax.fori_loop(0, tt, step, h_s[...], unroll=8)
    h_s[...] = h
    h_ref[0] = h
    y_ref[0] = (b_s[...] * gate).astype(y_ref.dtype)


def _rglru_weights(p):
    row = lambda v: v.reshape(1, -1).astype(F32)
    return (p['conv_w'].astype(F32), row(p['conv_b']), p['w_rg_a'].astype(BF16), row(p['b_rg_a']),
            p['w_rg_i'].astype(BF16), row(p['b_rg_i']), row(p['lru_lambda']))


def _rglru_prompt(u, ug, rw, b, t, tt):
    d = u.shape[1]
    nt = t // tt
    seq = pl.BlockSpec((1, tt, d), lambda bi, i: (bi, i, 0))
    return pl.pallas_call(
        _rglru_prompt_kernel,
        out_shape=(jax.ShapeDtypeStruct((b, t, d), BF16), jax.ShapeDtypeStruct((b, 1, d), F32)),
        grid=(b, nt),
        in_specs=[seq, seq] + [_full(w.shape) for w in rw],
        out_specs=(seq, pl.BlockSpec((1, 1, d), lambda bi, i: (bi, 0, 0))),
        scratch_shapes=[pltpu.VMEM((tt + 8, d), F32), pltpu.VMEM((tt, d), F32), pltpu.VMEM((tt, d), F32),
                        pltpu.VMEM((1, d), F32)],
        compiler_params=_cparams(("parallel", "arbitrary")),
        name="rglru_prompt",
    )(u.reshape(b, t, d), ug.reshape(b, t, d), *rw)


def _rglru_step_kernel(u_ref, ug_ref, cs_ref, h0_ref, cw_ref, cb_ref, wa_ref, ba_ref, wi_ref, bi_ref, lam_ref,
                       y_ref, h_ref):
    xc = cb_ref[...] + u_ref[...] * cw_ref[CONV_W - 1:CONV_W, :]
    for k in range(CONV_W - 1):
        xc = xc + cs_ref[k] * cw_ref[k:k + 1, :]
    a, b, gate = _rglru_gates(xc, ug_ref[...], wa_ref, ba_ref[...], wi_ref, bi_ref[...], lam_ref[...])
    h = a * h0_ref[...] + b
    h_ref[...] = h
    y_ref[...] = (h * gate).astype(y_ref.dtype)


def _rglru_step(u, ug, conv_state, h0, rw):
    n, d = u.shape
    args = (u, ug, conv_state, h0) + tuple(rw)
    return pl.pallas_call(
        _rglru_step_kernel,
        out_shape=(jax.ShapeDtypeStruct((n, d), BF16), jax.ShapeDtypeStruct((n, d), F32)),
        grid=(1,),
        in_specs=[_full(a.shape) for a in args],
        out_specs=(_full((n, d)), _full((n, d))),
        compiler_params=_cparams(("arbitrary",)),
        name="rglru_step",
    )(*args)


_START, _WAIT = "start", "wait"


def _row_gather(rows, go, copy_fn, unroll=4):
    assert rows % 2 == 0

    def body(i, c):
        for lane in range(2):
            copy = copy_fn(2 * i + lane)
            if go == _START:
                copy.start(priority=lane)
            else:
                copy.wait()
        return c
    lax.fori_loop(0, rows // 2, body, 0, unroll=unroll)


def _nsa_sample_cmp_kernel(pt_ref, pool_ref, q_ref, w1_ref, pe_ref, w2_ref, smap_ref,
                           oc_ref, idx_ref, pbuf, rows_s, c_s, sem, *, n_seq, n_pages, q_pos, n_sel_blocks):
    b = pl.program_id(0)
    nb = n_seq
    n_slots, pg = pbuf.shape[0], pbuf.shape[1]
    groups = n_pages // pg
    ahead = n_slots - 1
    rows_pp = PAGE_SIZE // CMP_STRIDE

    def page_copy(seq, grp, slot, j):
        return pltpu.make_async_copy(pool_ref.at[pt_ref[seq * n_pages + grp * pg + j]], pbuf.at[slot, j],
                                     sem.at[slot])

    def issue(seq, grp, slot):
        _row_gather(pg, _START, lambda j: page_copy(seq, grp, slot, j))

    @pl.when(b == 0)
    def _():
        for g in range(ahead):
            if g // groups < n_seq:
                issue(g // groups, g % groups, g % n_slots)

    for k in range(groups):
        gi = b * groups + k
        slot = lax.rem(gi, n_slots)
        nxt = k + ahead
        seq_off, grp_nxt = nxt // groups, nxt % groups

        @pl.when(b + seq_off < nb)
        def _():
            issue(b + seq_off, grp_nxt, lax.rem(gi + ahead, n_slots))

        _row_gather(pg, _WAIT, lambda j: page_copy(b, k, slot, j))
        halves = KV_WIDTH // LANES
        for j in range(pg):
            for f in range(halves):
                rows_s[f, j * PAGE_SIZE:(j + 1) * PAGE_SIZE, :] = pbuf[slot, j, f * LANES:(f + 1) * LANES, :].T

        def chunk_rows(s):
            return jnp.concatenate([rows_s[f, pl.ds(s, pg * rows_pp, stride=CMP_STRIDE), :]
                                    for f in range(halves)], axis=1).astype(BF16)

        acc = _dot(chunk_rows(0), w1_ref[0])
        for s in range(1, CMP_STRIDE):
            acc = acc + _dot(chunk_rows(s), w1_ref[s])
        c_s[k * pg * rows_pp:(k + 1) * pg * rows_pp, :] = acc

    nch = n_pages * rows_pp
    pec = _dot(pe_ref[...], w1_ref[...].reshape(CHUNK_W, w1_ref.shape[2]))
    kv = _compress_rows(c_s[...], pec, w2_ref[...]).astype(BF16)

    q = q_ref[0]
    c_end = lax.broadcasted_iota(I32, (1, nch), 1) * CMP_STRIDE + (CMP_BLOCK - 1)
    row = lax.broadcasted_iota(I32, (N_HEADS, 1), 0)
    width = smap_ref.shape[1]
    blk_i = lax.broadcasted_iota(I32, (1, width), 1)
    cur = q_pos // SEL_BLOCK
    forced = (blk_i == 0) | (blk_i == cur) | (blk_i == cur - 1)
    o_c = jnp.zeros((N_HEADS, HEAD_DIM), F32)
    score = jnp.full((N_HEADS, width), -jnp.inf, F32)
    for h in range(N_KV_HEADS):
        kc = kv[:, HEAD_DIM * h:HEAD_DIM * (h + 1)]
        vc = kv[:, HEAD_DIM * (N_KV_HEADS + h):HEAD_DIM * (N_KV_HEADS + h + 1)]
        p = _masked_softmax_rows(_dot_nt(q, kc), c_end <= q_pos).astype(BF16)
        in_group = (row // GROUP) == h
        o_c = jnp.where(in_group, _dot(p, vc), o_c)
        imp = jnp.sum(jnp.where(in_group, _dot(p, smap_ref[...]), 0.0), axis=0, keepdims=True)
        sc = jnp.where(blk_i > cur, -jnp.inf, jnp.where(forced, SEL_FORCE, imp))
        score = jnp.where(row == h, sc, score)
    oc_ref[0] = o_c

    n = min(N_SEL, n_sel_blocks)
    lane = lax.broadcasted_iota(I32, (1, LANES), 1)
    idx_out = jnp.zeros((N_HEADS, LANES), I32)
    blk = blk_i.astype(F32)
    for r in range(n):
        m = jnp.max(score, axis=-1, keepdims=True)
        idx = jnp.min(jnp.where(score == m, blk, float(width)), axis=-1, keepdims=True)
        score = jnp.where(blk == idx, -jnp.inf, score)
        idx_out = jnp.where(lane == r, idx.astype(I32), idx_out)
    idx_ref[0] = idx_out


CMP_PAGE_GROUP = 32
CMP_GROUP_SLOTS = 3


def _pages_feature_major(pool):
    n_pool, rows = pool.shape[0], pool.shape[1]
    return jnp.transpose(pool, (0, 2, 3, 4, 1)).reshape(n_pool, KV_WIDTH, rows)


def _nsa_sample_cmp(page_table, pool_t, q_seq, cw, past_len):
    bd, n_pages = page_table.shape
    w1, pe, w2 = cw
    w1 = w1.reshape(CMP_STRIDE, KV_WIDTH, w1.shape[1])
    rows_pp = PAGE_SIZE // CMP_STRIDE
    nch = n_pages * rows_pp
    pg = min(CMP_PAGE_GROUP, n_pages)
    assert n_pages % pg == 0
    n_sel_blocks = past_len // SEL_BLOCK + 1
    width = -(-n_sel_blocks // LANES) * LANES
    smap = _sel_map(nch, nch - 1, n_sel_blocks, width)
    grid_spec = pltpu.PrefetchScalarGridSpec(
        num_scalar_prefetch=1,
        grid=(bd,),
        in_specs=[pl.BlockSpec(memory_space=pl.ANY),
                  pl.BlockSpec((1, N_HEADS, HEAD_DIM), lambda i, pt: (i, 0, 0)),
                  pl.BlockSpec(w1.shape, lambda i, pt: (0, 0, 0)),
                  pl.BlockSpec(pe.shape, lambda i, pt: (0, 0)),
                  pl.BlockSpec(w2.shape, lambda i, pt: (0, 0)),
                  pl.BlockSpec(smap.shape, lambda i, pt: (0, 0))],
        out_specs=(pl.BlockSpec((1, N_HEADS, HEAD_DIM), lambda i, pt: (i, 0, 0)),
                   pl.BlockSpec((1, N_HEADS, LANES), lambda i, pt: (i, 0, 0))),
        scratch_shapes=[pltpu.VMEM((CMP_GROUP_SLOTS, pg, KV_WIDTH, PAGE_SIZE), F32),
                        pltpu.VMEM((KV_WIDTH // LANES, pg * PAGE_SIZE, LANES), F32),
                        pltpu.VMEM((nch, w1.shape[2]), F32),
                        pltpu.SemaphoreType.DMA((CMP_GROUP_SLOTS,))],
    )
    return pl.pallas_call(
        functools.partial(_nsa_sample_cmp_kernel, n_seq=bd, n_pages=n_pages, q_pos=past_len,
                          n_sel_blocks=n_sel_blocks),
        out_shape=(jax.ShapeDtypeStruct((bd, N_HEADS, HEAD_DIM), F32),
                   jax.ShapeDtypeStruct((bd, N_HEADS, LANES), I32)),
        grid_spec=grid_spec,
        compiler_params=_cparams(("arbitrary",)),
        name="nsa_sample_cmp",
    )(page_table.reshape(-1), pool_t, q_seq, w1, pe, w2, smap)


def _nsa_sample_sel_kernel(pt_ref, idx_s_ref, pool_ref, qr_ref, idx_ref, kvs_ref, kvw_ref, win_ref, oc_ref, g_ref,
                           o_ref, wout_ref, sbuf, sem, *, n_seq, n_pages, n_sel, ns_past):
    b = pl.program_id(0)
    per_page = PAGE_SIZE // SEL_BLOCK
    nblk = N_KV_HEADS * n_sel
    nkeys = n_sel * PAGE_SIZE

    def slab_copy(seq, slot, j):
        blk, kv = j // 2, j % 2
        jp = jnp.minimum(idx_s_ref[seq * nblk + blk], ns_past - 1)
        page = pt_ref[seq * n_pages + jp // per_page]
        feat = pl.multiple_of((kv * N_KV_HEADS + blk // n_sel) * HEAD_DIM, HEAD_DIM)
        return pltpu.make_async_copy(pool_ref.at[page, pl.ds(feat, HEAD_DIM), :], sbuf.at[slot, blk, kv],
                                     sem.at[slot])

    @pl.when(b == 0)
    def _():
        _row_gather(2 * nblk, _START, lambda j: slab_copy(0, 0, j))

    @pl.when(b + 1 < n_seq)
    def _():
        _row_gather(2 * nblk, _START, lambda j: slab_copy(b + 1, (b + 1) % 2, j))

    slot = b % 2
    _row_gather(2 * nblk, _WAIT, lambda j: slab_copy(b, slot, j))

    qr = qr_ref[0]
    qf = qr.astype(F32)
    row = lax.broadcasted_iota(I32, (N_HEADS, 1), 0)
    key = lax.broadcasted_iota(I32, (1, nkeys), 1)
    key_blk = ((key % PAGE_SIZE) // SEL_BLOCK).astype(F32)
    first = (key % SEL_BLOCK) == 0
    expand = jnp.where(lax.broadcasted_iota(I32, (LANES, nkeys), 0)
                       == lax.broadcasted_iota(I32, (LANES, nkeys), 1) // PAGE_SIZE, 1.0, 0.0).astype(BF16)
    chosen = idx_ref[0]
    in_page = (jnp.minimum(chosen, ns_past - 1) % per_page).astype(F32).astype(BF16)
    want_blk = _dot(in_page, expand)
    is_new = _dot(jnp.where(chosen >= ns_past, 1.0, 0.0).astype(BF16), expand) > 0.5
    kvs_new = kvs_ref[0].astype(BF16).astype(F32)
    kvw_new = kvw_ref[0].astype(BF16).astype(F32)
    o_s = jnp.zeros((N_HEADS, HEAD_DIM), F32)
    o_w = jnp.zeros((N_HEADS, HEAD_DIM), F32)
    for h in range(N_KV_HEADS):
        in_group = (row // GROUP) == h
        ksl = slice(HEAD_DIM * h, HEAD_DIM * (h + 1))
        vsl = slice(HEAD_DIM * (N_KV_HEADS + h), HEAD_DIM * (N_KV_HEADS + h + 1))
        k_t = jnp.concatenate([sbuf[slot, h * n_sel + n, 0] for n in range(n_sel)], axis=1).astype(BF16)
        v_t = jnp.concatenate([sbuf[slot, h * n_sel + n, 1] for n in range(n_sel)], axis=1).astype(BF16)
        in_blk = want_blk[h:h + 1] == key_blk
        newblk = is_new[h:h + 1]
        newkey = in_blk & newblk & first
        s_new = jnp.sum(qf * kvs_new[:, ksl], axis=-1, keepdims=True)
        s = jnp.where(newkey, s_new, _dot(qr, k_t))
        p = _masked_softmax_rows(s, in_blk & (jnp.logical_not(newblk) | first))
        p_new = jnp.sum(jnp.where(newkey, p, 0.0), axis=-1, keepdims=True).astype(BF16).astype(F32)
        o_h = _dot_nt(jnp.where(newkey, 0.0, p).astype(BF16), v_t) + p_new * kvs_new[:, vsl]
        o_s = jnp.where(in_group, o_h, o_s)
        s_buf = _dot(qr, win_ref[0, ksl, :].astype(BF16))
        s_cur = jnp.sum(qf * kvw_new[:, ksl], axis=-1, keepdims=True)
        m = jnp.maximum(jnp.max(s_buf, axis=-1, keepdims=True), s_cur)
        e_buf = jnp.exp(s_buf - m)
        e_cur = jnp.exp(s_cur - m)
        den = jnp.maximum(jnp.sum(e_buf, axis=-1, keepdims=True) + e_cur, 1e-30)
        o_h = (_dot_nt((e_buf / den).astype(BF16), win_ref[0, vsl, :].astype(BF16))
               + (e_cur / den).astype(BF16).astype(F32) * kvw_new[:, vsl])
        o_w = jnp.where(in_group, o_h, o_w)

    gw = g_ref.shape[-1]
    gl = lax.broadcasted_iota(I32, (N_HEADS, gw), 1)
    gbase = (row // GROUP) * LANES + (row % GROUP) * 3
    gates = jnp.broadcast_to(g_ref[0], (N_HEADS, gw))
    gate = lambda j: jnp.sum(jnp.where(gl == gbase + j, gates, 0.0), axis=-1, keepdims=True)
    o_ref[0] = gate(0) * oc_ref[0] + gate(1) * o_s + gate(2) * o_w

    n_chunks = win_ref.shape[2] // LANES
    new_col = jnp.broadcast_to(kvw_ref[0], (LANES, KV_WIDTH)).T
    last_lane = lax.broadcasted_iota(I32, (1, LANES), 1) == LANES - 1
    for c in range(n_chunks):
        cur = win_ref[0, :, c * LANES:(c + 1) * LANES]
        nxt = win_ref[0, :, (c + 1) * LANES:(c + 2) * LANES] if c + 1 < n_chunks else new_col
        wout_ref[0, :, c * LANES:(c + 1) * LANES] = jnp.where(last_lane, pltpu.roll(nxt, LANES - 1, 1),
                                                             pltpu.roll(cur, LANES - 1, 1))


def _nsa_sample_sel(page_table, idx, pool_t, qr_seq, kvs_new, kvw_new, win_t, o_c, gates, past_len):
    bd, n_pages = page_table.shape
    n_sel = min(N_SEL, past_len // SEL_BLOCK + 1)
    n_buf = win_t.shape[2]
    assert n_buf % LANES == 0
    gw = gates.shape[-1]
    seq3 = lambda s1, s2: pl.BlockSpec((1, s1, s2), lambda i, pt, ix: (i, 0, 0))
    grid_spec = pltpu.PrefetchScalarGridSpec(
        num_scalar_prefetch=2,
        grid=(bd,),
        in_specs=[pl.BlockSpec(memory_space=pl.ANY), seq3(N_HEADS, HEAD_DIM), seq3(N_HEADS, LANES),
                  seq3(1, KV_WIDTH), seq3(1, KV_WIDTH), seq3(KV_WIDTH, n_buf), seq3(N_HEADS, HEAD_DIM),
                  seq3(1, gw)],
        out_specs=(seq3(N_HEADS, HEAD_DIM), seq3(KV_WIDTH, n_buf)),
        scratch_shapes=[pltpu.VMEM((2, N_KV_HEADS * n_sel, 2, HEAD_DIM, PAGE_SIZE), F32),
                        pltpu.SemaphoreType.DMA((2,))],
    )
    idx_flat = idx[:, :N_KV_HEADS, :n_sel].reshape(-1)
    return pl.pallas_call(
        functools.partial(_nsa_sample_sel_kernel, n_seq=bd, n_pages=n_pages, n_sel=n_sel,
                          ns_past=past_len // SEL_BLOCK),
        out_shape=(jax.ShapeDtypeStruct((bd, N_HEADS, HEAD_DIM), F32),
                   jax.ShapeDtypeStruct((bd, KV_WIDTH, n_buf), F32)),
        grid_spec=grid_spec,
        compiler_params=_cparams(("arbitrary",)),
        name="nsa_sample_sel",
    )(page_table.reshape(-1), idx_flat, pool_t, qr_seq, idx, kvs_new.reshape(bd, 1, KV_WIDTH),
      kvw_new.reshape(bd, 1, KV_WIDTH), win_t, o_c, gates.reshape(bd, 1, gw))


def _tail_kernel(x_ref, o_ref, y_ref, ga_ref, gb_ref, wa_ref, wb_ref, wo_ref, g1_ref, b1_ref, wr_ref, rb_ref,
                 cin_ref, x1_ref, idx_ref, w_ref, pos_ref, cnt_ref, carry_s, *, alpha):
    i = pl.program_id(0)
    tm = x_ref.shape[0]
    ne = wr_ref.shape[0]

    @pl.when(i == 0)
    def _():
        carry_s[...] = cin_ref[...]

    merged = (_sigmoid(ga_ref[...]) * _dot(o_ref[...], wa_ref[...])
              + _sigmoid(gb_ref[...]) * _dot(y_ref[...], wb_ref[...]))
    mix = _dot(merged.astype(BF16), wo_ref[...])
    x1 = _layer_norm(alpha * x_ref[...] + mix, g1_ref[...], b1_ref[...])
    x1_ref[...] = x1

    scores = _sigmoid(_dot_nt(wr_ref[...], x1.astype(BF16)))
    picked, hits = _top_rows(scores + rb_ref[...], TOP_K)
    expert = lax.broadcasted_iota(I32, (ne, tm), 0).astype(F32)
    chosen = [jnp.sum(jnp.where(expert == e_k, scores, 0.0), axis=0, keepdims=True) for e_k in picked]
    total = chosen[0]
    for c in chosen[1:]:
        total = total + c

    earlier = lax.broadcasted_iota(I32, (tm, tm), 0) < lax.broadcasted_iota(I32, (tm, tm), 1)
    prefix = _dot(hits.astype(BF16), jnp.where(earlier, 1.0, 0.0).astype(BF16)) + carry_s[...]
    out_row = lax.broadcasted_iota(I32, (idx_ref.shape[0], 1), 0)
    idx_out = jnp.zeros(idx_ref.shape, I32)
    w_out = jnp.zeros(w_ref.shape, F32)
    pos_out = jnp.zeros(pos_ref.shape, I32)
    for k in range(TOP_K):
        p_k = jnp.sum(jnp.where(expert == picked[k], prefix, 0.0), axis=0, keepdims=True)
        idx_out = jnp.where(out_row == k, picked[k].astype(I32), idx_out)
        pos_out = jnp.where(out_row == k, p_k.astype(I32), pos_out)
        w_out = jnp.where(out_row == k, chosen[k] / total * ROUTED_SCALE, w_out)
    idx_ref[...] = idx_out
    w_ref[...] = w_out
    pos_ref[...] = pos_out
    carry_s[...] = carry_s[...] + jnp.sum(hits, axis=1, keepdims=True)
    cnt_ref[...] = carry_s[...]


def _tail_weights(p):
    row = lambda v: v.reshape(1, -1).astype(F32)
    return (p['w_branch_attn'].astype(BF16), p['w_branch_rnn'].astype(BF16), p['w_out'].astype(BF16),
            row(p['ln1_g']), row(p['ln1_b']), p['w_router'].T.astype(BF16),
            p['router_bias'].reshape(-1, 1).astype(F32))


def _tail(x2d, o_attn, y_rnn, g_a, g_b, tw, counts_in, alpha, tm):
    m, d = x2d.shape
    ne = tw[5].shape[0]
    row = lambda w: pl.BlockSpec((tm, w), lambda i: (i, 0))
    slots = pl.BlockSpec((8, tm), lambda i: (0, i))
    small = jax.ShapeDtypeStruct((8, m), I32)
    return pl.pallas_call(
        functools.partial(_tail_kernel, alpha=alpha),
        out_shape=(jax.ShapeDtypeStruct((m, d), F32), small, jax.ShapeDtypeStruct((8, m), F32), small,
                   jax.ShapeDtypeStruct((ne, 1), F32)),
        grid=(m // tm,),
        in_specs=[row(d), row(o_attn.shape[1]), row(y_rnn.shape[1]), row(d), row(d)]
                 + [_full(w.shape) for w in tw] + [_full((ne, 1))],
        out_specs=(row(d), slots, slots, slots, _full((ne, 1))),
        scratch_shapes=[pltpu.VMEM((ne, 1), F32)],
        compiler_params=_cparams(("arbitrary",)),
        name="tail",
    )(x2d, o_attn, y_rnn, g_a, g_b, *tw, counts_in)


def _silu(x):
    return x * _sigmoid(x)


def _pack_bf16_pairs(x):
    half = x.shape[1] // 2
    bits = lax.bitcast_convert_type(x.astype(BF16).astype(F32), jnp.uint32)
    return bits[:, :half] | (bits[:, half:] >> 16)


def _dispatch_kernel(dest_ref, xp_ref, xs_ref, out_ref, wbuf, zbuf, sem, *, n_tok, tail):
    i = pl.program_id(0)
    n = pl.num_programs(0)
    tm = xp_ref.shape[0]
    n_sample = xs_ref.shape[0]
    slot = i % 2

    def scatter(tile, rows, slot_, go):
        def row_copy(k, t):
            dst = dest_ref[k * n_tok + tile * tm + t]
            return pltpu.make_async_copy(wbuf.at[slot_, pl.ds(t, 1)], out_ref.at[pl.ds(dst, 1)], sem.at[slot_])
        for k in range(TOP_K):
            _row_gather(rows, go, functools.partial(row_copy, k))

    if tail is not None:
        tail_copy = pltpu.make_async_copy(zbuf, out_ref.at[pl.ds(tail[0], tail[1])], sem.at[2])

        @pl.when(i == 0)
        def _():
            zbuf[...] = jnp.zeros(zbuf.shape, zbuf.dtype)
            tail_copy.start()

    @pl.when(i >= 2)
    def _():
        scatter(i - 2, tm, slot, _WAIT)

    @pl.when(i < n - 1)
    def _():
        wbuf[slot] = _pack_bf16_pairs(xp_ref[...])
        scatter(i, tm, slot, _START)

    @pl.when(i == n - 1)
    def _():
        wbuf[slot, 0:n_sample, :] = _pack_bf16_pairs(xs_ref[...])
        scatter(i, n_sample, slot, _START)
        scatter(i, n_sample, slot, _WAIT)
        if tail is not None:
            tail_copy.wait()

    @pl.when((i == n - 1) & (i >= 1))
    def _():
        scatter(i - 1, tm, 1 - slot, _WAIT)


def _dispatch(dest, x1_p, x1_s, n_rows_pad, tm):
    m, d = x1_p.shape
    n_rows = dest.shape[0]
    n_tiles = m // tm
    assert n_rows == (m + x1_s.shape[0]) * TOP_K and x1_s.shape[0] <= tm
    tail = None if n_rows_pad == n_rows else (n_rows, n_rows_pad - n_rows)
    grid_spec = pltpu.PrefetchScalarGridSpec(
        num_scalar_prefetch=1,
        grid=(n_tiles + 1,),
        in_specs=[pl.BlockSpec((tm, d), lambda i, ds: (jnp.minimum(i, n_tiles - 1), 0)),
                  pl.BlockSpec(x1_s.shape, lambda i, ds: (0, 0))],
        out_specs=pl.BlockSpec(memory_space=pl.ANY),
        scratch_shapes=[pltpu.VMEM((2, tm, d // 2), jnp.uint32),
                        pltpu.VMEM((8 if tail is None else tail[1], d // 2), jnp.uint32),
                        pltpu.SemaphoreType.DMA((3,))],
    )
    return pl.pallas_call(
        functools.partial(_dispatch_kernel, n_tok=n_rows // TOP_K, tail=tail),
        out_shape=jax.ShapeDtypeStruct((n_rows_pad, d // 2), jnp.uint32),
        grid_spec=grid_spec,
        compiler_params=_cparams(("arbitrary",)),
        name="dispatch",
    )(dest, x1_p, x1_s)


def _experts_kernel(vb_ref, ve_ref, lo_ref, hi_ref, xs_ref, wg_ref, wu_ref, wd_ref, y_ref, wgb, wub, wdb):
    v = pl.program_id(0)
    lo, hi = lo_ref[v], hi_ref[v]
    prev = jnp.maximum(v - 1, 0)
    first_of_block = (v == 0) | (vb_ref[v] != vb_ref[prev])

    @pl.when((v == 0) | (ve_ref[v] != ve_ref[prev]))
    def _():
        wgb[...] = wg_ref[0].astype(BF16)
        wub[...] = wu_ref[0].astype(BF16)
        wdb[...] = wd_ref[0].astype(BF16)

    @pl.when(hi > lo)
    def _():
        words = xs_ref[...]
        x = jnp.concatenate(
            [lax.bitcast_convert_type(words & jnp.uint32(0xFFFF0000), F32).astype(BF16),
             lax.bitcast_convert_type(words << 16, F32).astype(BF16)], axis=1)
        hid = _silu(_dot(x, wgb[...])) * _dot(x, wub[...])
        val = _dot(hid.astype(BF16), wdb[...])
        row = lax.broadcasted_iota(I32, (val.shape[0], 1), 0)
        mine = (row >= lo) & (row < hi)

        @pl.when(first_of_block)
        def _():
            y_ref[...] = jnp.where(mine, val, 0.0)

        @pl.when(jnp.logical_not(first_of_block))
        def _():
            y_ref[...] = jnp.where(mine, val, y_ref[...])


def _experts(xs, visits, w_gate, w_up, w_down):
    vblk, vexp, vlo, vhi = visits
    d, de = w_gate.shape[1], w_gate.shape[2]
    rb = EXPERT_ROWS
    grid_spec = pltpu.PrefetchScalarGridSpec(
        num_scalar_prefetch=4,
        grid=(vblk.shape[0],),
        in_specs=[pl.BlockSpec((rb, d // 2), lambda v, vb, ve, lo, hi: (vb[v], 0)),
                  pl.BlockSpec((1, d, de), lambda v, vb, ve, lo, hi: (ve[v], 0, 0)),
                  pl.BlockSpec((1, d, de), lambda v, vb, ve, lo, hi: (ve[v], 0, 0)),
                  pl.BlockSpec((1, de, d), lambda v, vb, ve, lo, hi: (ve[v], 0, 0))],
        out_specs=pl.BlockSpec((rb, d), lambda v, vb, ve, lo, hi: (vb[v], 0)),
        scratch_shapes=[pltpu.VMEM((d, de), BF16), pltpu.VMEM((d, de), BF16), pltpu.VMEM((de, d), BF16)],
    )
    return pl.pallas_call(
        _experts_kernel,
        out_shape=jax.ShapeDtypeStruct((xs.shape[0], d), F32),
        grid_spec=grid_spec,
        compiler_params=_cparams(("arbitrary",)),
        name="experts",
    )(vblk, vexp, vlo, vhi, xs, w_gate, w_up, w_down)


def _combine_kernel(dest_ref, x1_ref, w_ref, y_ref, wsg_ref, wsu_ref, wsd_ref, g2_ref, b2_ref, out_ref,
                    ybuf, sem, *, n_tok, tok_off, alpha):
    i = pl.program_id(0)
    n = pl.num_programs(0)
    tc = x1_ref.shape[0]

    def row_copy(tile, slot, k, t):
        src = dest_ref[k * n_tok + tok_off + tile * tc + t]
        return pltpu.make_async_copy(y_ref.at[pl.ds(src, 1)], ybuf.at[slot, k, pl.ds(t, 1)], sem.at[slot])

    def gather(tile, slot, go):
        for k in range(TOP_K):
            _row_gather(tc, go, lambda t: row_copy(tile, slot, k, t))

    @pl.when(i == 0)
    def _():
        gather(0, 0, _START)

    @pl.when(i + 1 < n)
    def _():
        gather(i + 1, (i + 1) % 2, _START)

    slot = i % 2
    gather(i, slot, _WAIT)
    x1 = x1_ref[...]
    w = w_ref[...]
    routed = w[:, 0:1] * ybuf[slot, 0]
    for k in range(1, TOP_K):
        routed = routed + w[:, k:k + 1] * ybuf[slot, k]
    xb = x1.astype(BF16)
    shared = _dot((_silu(_dot(xb, wsg_ref[...])) * _dot(xb, wsu_ref[...])).astype(BF16), wsd_ref[...])
    out_ref[...] = _layer_norm(alpha * x1 + (routed + shared), g2_ref[...], b2_ref[...])


def _combine_weights(p):
    row = lambda v: v.reshape(1, -1).astype(F32)
    return (p['w_sh_gate'].astype(BF16), p['w_sh_up'].astype(BF16), p['w_sh_down'].astype(BF16),
            row(p['ln2_g']), row(p['ln2_b']))


def _combine(dest, x1, w, y_rows, cw, tok_off, alpha, tc):
    m, d = x1.shape
    row = lambda width: pl.BlockSpec((tc, width), lambda i, ds: (i, 0))
    grid_spec = pltpu.PrefetchScalarGridSpec(
        num_scalar_prefetch=1,
        grid=(m // tc,),
        in_specs=[row(d), row(w.shape[1]), pl.BlockSpec(memory_space=pl.ANY)]
                 + [pl.BlockSpec(a.shape, lambda i, ds: (0, 0)) for a in cw],
        out_specs=row(d),
        scratch_shapes=[pltpu.VMEM((2, TOP_K, tc, d), F32), pltpu.SemaphoreType.DMA((2,))],
    )
    return pl.pallas_call(
        functools.partial(_combine_kernel, n_tok=dest.shape[0] // TOP_K, tok_off=tok_off, alpha=alpha),
        out_shape=jax.ShapeDtypeStruct((m, d), F32),
        grid_spec=grid_spec,
        compiler_params=_cparams(("arbitrary",)),
        name="combine",
    )(dest, x1, w, y_rows, *cw)


def _lookup(table, idx):
    hit = idx[..., None] == jnp.arange(table.shape[0], dtype=idx.dtype)
    return jnp.sum(jnp.where(hit, table, 0), axis=-1)


def _route(idx, pos, counts):
    n_tok = idx.shape[1]
    ne = counts.shape[0]
    rb = EXPERT_ROWS
    n_blocks = -(-n_tok * TOP_K // rb)
    end = jnp.cumsum(counts)
    start = end - counts
    dest = _lookup(start, idx) + pos
    first_blk = start // rb
    n_vis = jnp.where(counts > 0, (end - 1) // rb - first_blk + 1, 0)
    vend = jnp.cumsum(n_vis)
    vstart = vend - n_vis
    v = jnp.arange(n_blocks + ne - 1, dtype=I32)
    valid = v < vend[-1]
    v_c = jnp.minimum(v, vend[-1] - 1)
    vexp = jnp.minimum(jnp.sum((vend[None, :] <= v_c[:, None]).astype(I32), axis=1), ne - 1)
    of_visit = lambda a: _lookup(a, vexp)
    vblk = of_visit(first_blk) + (v_c - of_visit(vstart))
    vlo = jnp.where(valid, jnp.clip(of_visit(start) - vblk * rb, 0, rb), 0)
    vhi = jnp.where(valid, jnp.clip(of_visit(end) - vblk * rb, 0, rb), 0)
    visits = tuple(a.astype(I32) for a in (vblk, vexp, vlo, vhi))
    return dest.reshape(-1).astype(I32), visits, n_blocks * rb


def _layer(xp, xs, caches, page_table, p, depth):
    b, t, d = xp.shape
    bd, s, _ = xs.shape
    assert s == 1, "sample group is one new token per sequence"
    pool_c, pool_s, win_buf, state_conv, state_rnn = caches
    past_len = page_table.shape[1] * PAGE_SIZE
    alpha = (2.0 * depth) ** 0.25
    kv6 = lambda a, n, rows: a.reshape(n, rows, 2, N_KV_HEADS, HEAD_DIM)
    rows_major = lambda a: jnp.transpose(a.reshape(a.shape[0], 2, N_KV_HEADS, HEAD_DIM, a.shape[2]), (0, 4, 1, 2, 3))

    wparts = _split_w_in(p['w_in'])
    cw = _compress_weights(p)
    rw = _rglru_weights(p)
    tw = _tail_weights(p)
    mw = _combine_weights(p)

    pos_p = jnp.tile(jnp.arange(t, dtype=I32), b)
    (q, qr, kvc, kvc_t, kvs_t, kvw_t, ksh, vsh, kwh, vwh, gates, u_rnn, u_gate, g_a, g_b) = _proj(
        xp.reshape(b * t, d), pos_p, wparts, 256, seq_len=t)
    kc, vc = _compress_prompt(kvc, b, t, cw)
    o_attn = _nsa_prompt(q, qr, kc, vc, ksh, vsh, kwh, vwh, gates, b, t)
    y_rnn, h_p = _rglru_prompt(u_rnn, u_gate, rw, b, t, 256)
    ne = p['w_router'].shape[1]
    x1_p, idx_p, w_p, pos_r_p, counts_p = _tail(xp.reshape(b * t, d), o_attn, y_rnn.reshape(b * t, -1), g_a, g_b,
                                                tw, jnp.zeros((ne, 1), F32), alpha, 256)
    wn = min(WINDOW, t)
    outs_p = (rows_major(kvc_t), rows_major(kvs_t), rows_major(kvw_t[:, :, t - wn:]),
              u_rnn.reshape(b, t, -1)[:, t - (CONV_W - 1):], h_p.reshape(b, -1))

    pos_s = jnp.full((bd,), past_len, I32)
    (q, qr, kvc_s, kvs_s, kvw_s, _, _, _, _, gates_s, u_rnn_s, u_gate_s, g_a_s, g_b_s) = _proj(
        xs.reshape(bd, d), pos_s, wparts, bd)
    o_c, sel_idx = _nsa_sample_cmp(page_table, _pages_feature_major(pool_c), q.transpose(1, 0, 2), cw, past_len)
    o_s, win_new_t = _nsa_sample_sel(page_table, sel_idx, _pages_feature_major(pool_s), qr.transpose(1, 0, 2),
                                     kvs_s, kvw_s, _pages_feature_major(win_buf), o_c, gates_s, past_len)
    y_rnn_s, h_s = _rglru_step(u_rnn_s, u_gate_s, state_conv.transpose(1, 0, 2), state_rnn, rw)
    x1_s, idx_s, w_s, pos_r_s, counts = _tail(xs.reshape(bd, d), o_s.reshape(bd, Q_WIDTH).astype(BF16), y_rnn_s,
                                              g_a_s, g_b_s, tw, counts_p, alpha, bd)
    conv_s = jnp.concatenate([state_conv[:, 1:], u_rnn_s[:, None, :]], axis=1)
    outs_s = (kv6(kvc_s, bd, 1), kv6(kvs_s, bd, 1), rows_major(win_new_t), conv_s, h_s)

    idx_all = jnp.concatenate([idx_p, idx_s], axis=1)[:TOP_K]
    pos_all = jnp.concatenate([pos_r_p, pos_r_s], axis=1)[:TOP_K]
    dest, visits, n_rows_pad = _route(idx_all, pos_all, counts.reshape(-1).astype(I32))
    xs = _dispatch(dest, x1_p, x1_s, n_rows_pad, 256)
    y_rows = _experts(xs, visits, p['w_exp_gate'], p['w_exp_up'], p['w_exp_down'])
    yp = _combine(dest, x1_p, w_p.T, y_rows, mw, 0, alpha, 128)
    ys = _combine(dest, x1_s, w_s.T, y_rows, mw, b * t, alpha, bd)
    return yp.reshape(b, t, d), ys.reshape(bd, s, d), outs_p, outs_s


def kernel(x_prompt, x_sample, cache_cmp_kv, cache_sel_kv, cache_win_kv, state_conv, state_rnn, page_table,
           w_in, conv_w, conv_b, w_rg_a, b_rg_a, w_rg_i, b_rg_i, lru_lambda, cmp_pos_k, cmp_pos_v,
           w_cmp_k1, w_cmp_k2, w_cmp_v1, w_cmp_v2, w_branch_attn, w_branch_rnn, w_out, ln1_g, ln1_b,
           w_router, router_bias, w_exp_gate, w_exp_up, w_exp_down, w_sh_gate, w_sh_up, w_sh_down, ln2_g, ln2_b):
    weights = dict(w_in=w_in, conv_w=conv_w, conv_b=conv_b, w_rg_a=w_rg_a, b_rg_a=b_rg_a, w_rg_i=w_rg_i,
                   b_rg_i=b_rg_i, lru_lambda=lru_lambda, cmp_pos_k=cmp_pos_k, cmp_pos_v=cmp_pos_v,
                   w_cmp_k1=w_cmp_k1, w_cmp_k2=w_cmp_k2, w_cmp_v1=w_cmp_v1, w_cmp_v2=w_cmp_v2,
                   w_branch_attn=w_branch_attn, w_branch_rnn=w_branch_rnn, w_out=w_out, ln1_g=ln1_g, ln1_b=ln1_b,
                   w_router=w_router, router_bias=router_bias, w_exp_gate=w_exp_gate, w_exp_up=w_exp_up,
                   w_exp_down=w_exp_down, w_sh_gate=w_sh_gate, w_sh_up=w_sh_up, w_sh_down=w_sh_down,
                   ln2_g=ln2_g, ln2_b=ln2_b)
    depth = w_in.shape[0]
    xp, xs = x_prompt, x_sample
    per_layer_p, per_layer_s = [], []
    for l in range(depth):
        p = {k: v[l] for k, v in weights.items()}
        caches = (cache_cmp_kv[l], cache_sel_kv[l], cache_win_kv[l], state_conv[l], state_rnn[l])
        xp, xs, outs_p, outs_s = _layer(xp, xs, caches, page_table, p, depth)
        per_layer_p.append(outs_p)
        per_layer_s.append(outs_s)
    stack = lambda outs, i: jnp.stack([o[i] for o in outs])
    return (xp, xs, stack(per_layer_p, 0), stack(per_layer_s, 0), stack(per_layer_p, 1), stack(per_layer_s, 1),
            stack(per_layer_p, 2), stack(per_layer_s, 2), stack(per_layer_p, 3), stack(per_layer_s, 3),
            stack(per_layer_p, 4), stack(per_layer_s, 4))
```

```python
import functools

import numpy as np
import jax
import jax.numpy as jnp
from jax import lax
from jax.experimental import pallas as pl
from jax.experimental.pallas import tpu as pltpu

F32 = jnp.float32
BF16 = jnp.bfloat16
I32 = jnp.int32

N_HEADS = 8
HEAD_DIM = 64
N_KV_HEADS = 2
GROUP = N_HEADS // N_KV_HEADS
ROT_DIM = HEAD_DIM // 4
ROPE_THETA = 500000.0
CMP_BLOCK = 32
CMP_STRIDE = 16
CMP_HIDDEN = 64
SEL_BLOCK = 64
N_SEL = 16
SEL_FORCE = 1.0e6
WINDOW = 512
Q_BLOCK = 128
RNN_BLOCKS = 8
CONV_W = 4
RG_C = 8.0
TOP_K = 6
ROUTED_SCALE = 2.5
LN_EPS = 1e-5
PAGE_SIZE = 128

Q_WIDTH = N_HEADS * HEAD_DIM
KV_WIDTH = 2 * N_KV_HEADS * HEAD_DIM
N_GROUPS = 2 * N_KV_HEADS
CHUNK_W = CMP_STRIDE * KV_WIDTH
LANES = 128
NEG = -1.0e30
VMEM_LIMIT = 56 * 1024 * 1024

EXPERT_ROWS = 512
SEL_TK = 1024


def _cparams(sem):
    return pltpu.CompilerParams(dimension_semantics=sem, vmem_limit_bytes=VMEM_LIMIT)


def _full(shape):
    n = len(shape)
    return pl.BlockSpec(shape, lambda *a: (0,) * n)


def _dot(a, b):
    return jnp.dot(a, b, preferred_element_type=F32)


def _dot_nt(a, b):
    return lax.dot_general(a, b, (((1,), (1,)), ((), ())), preferred_element_type=F32)


def _gelu(x):
    return 0.5 * x * (1.0 + jnp.tanh(np.sqrt(2.0 / np.pi) * (x + 0.044715 * (x * x * x))))


def _sigmoid(x):
    return 1.0 / (1.0 + jnp.exp(-x))


def _layer_norm(z, g, b):
    mu = jnp.mean(z, axis=-1, keepdims=True)
    d = z - mu
    var = jnp.mean(d * d, axis=-1, keepdims=True)
    return d * lax.rsqrt(var + LN_EPS) * g + b


def _rope_tables(pos):
    half = ROT_DIM // 2
    inv = ROPE_THETA ** (-(jnp.arange(half, dtype=F32) * (2.0 / ROT_DIM)))
    ang = pos.astype(F32)[:, None] * inv[None, :]
    cos, sin = jnp.cos(ang), jnp.sin(ang)
    m = pos.shape[0]
    one = jnp.ones((m, HEAD_DIM - ROT_DIM), F32)
    zero = jnp.zeros((m, HEAD_DIM - ROT_DIM), F32)
    zh = jnp.zeros((m, half), F32)
    c = jnp.concatenate([cos, cos, one], axis=1)
    s1 = jnp.concatenate([-sin, zh, zero], axis=1)
    s2 = jnp.concatenate([zh, sin, zero], axis=1)
    rep = LANES // HEAD_DIM
    return jnp.tile(c, (1, rep)), jnp.tile(s1, (1, rep)), jnp.tile(s2, (1, rep))


def _rope128(x, c, s1, s2):
    half = ROT_DIM // 2
    return x * c + pltpu.roll(x, LANES - half, 1) * s1 + pltpu.roll(x, half, 1) * s2


def _proj_kernel(x_ref, c_ref, s1_ref, s2_ref, wq_ref, wkv_ref, wg_ref, ww_ref, *out_refs, transposed):
    if transposed:
        (q_ref, qr_ref, kvc_ref, kvct_ref, kvst_ref, kvwt_ref, ksh_ref, vsh_ref, kwh_ref, vwh_ref,
         gate_ref, urnn_ref, ugate_ref, ga_ref, gb_ref) = out_refs
        kvs_ref = kvw_ref = None
    else:
        (q_ref, qr_ref, kvc_ref, kvs_ref, kvw_ref, ksh_ref, vsh_ref, kwh_ref, vwh_ref,
         gate_ref, urnn_ref, ugate_ref, ga_ref, gb_ref) = out_refs
        kvct_ref = kvst_ref = kvwt_ref = None
    xb = x_ref[...].astype(BF16)
    c, s1, s2 = c_ref[...], s1_ref[...], s2_ref[...]
    scale = HEAD_DIM ** -0.5
    q = _dot(xb, wq_ref[...]) * scale
    for j in range(Q_WIDTH // LANES):
        ch = q[:, LANES * j:LANES * (j + 1)]
        rot = _rope128(ch, c, s1, s2)
        for hh in range(LANES // HEAD_DIM):
            head = j * (LANES // HEAD_DIM) + hh
            q_ref[head] = ch[:, HEAD_DIM * hh:HEAD_DIM * (hh + 1)].astype(BF16)
            qr_ref[head] = rot[:, HEAD_DIM * hh:HEAD_DIM * (hh + 1)].astype(BF16)
    kv = _dot(xb, wkv_ref[...])
    kvc_ref[...] = kv[:, :KV_WIDTH]
    if transposed:
        kvct_ref[0] = kv[:, :KV_WIDTH].T
    for base, full_ref, t_ref, kh_ref, vh_ref in ((KV_WIDTH, kvs_ref, kvst_ref, ksh_ref, vsh_ref),
                                                  (2 * KV_WIDTH, kvw_ref, kvwt_ref, kwh_ref, vwh_ref)):
        keys = _rope128(kv[:, base:base + LANES], c, s1, s2)
        vals = kv[:, base + LANES:base + 2 * LANES]
        if transposed:
            t_ref[0, :LANES, :] = keys.T
            t_ref[0, LANES:, :] = vals.T
        else:
            full_ref[:, :LANES] = keys
            full_ref[:, LANES:] = vals
        for h in range(N_KV_HEADS):
            kh_ref[h] = keys[:, HEAD_DIM * h:HEAD_DIM * (h + 1)].astype(BF16)
            vh_ref[h] = vals[:, HEAD_DIM * h:HEAD_DIM * (h + 1)].astype(BF16)
    gate_ref[...] = _sigmoid(_dot(xb, wg_ref[...]))
    d = urnn_ref.shape[-1]
    for k, ref in enumerate((urnn_ref, ugate_ref, ga_ref, gb_ref)):
        ref[...] = _dot(xb, ww_ref[:, d * k:d * (k + 1)])


def _split_w_in(w_in):
    d = w_in.shape[0]
    o = 0
    wq = w_in[:, o:o + Q_WIDTH]; o += Q_WIDTH
    wkv = w_in[:, o:o + 3 * KV_WIDTH]; o += 3 * KV_WIDTH
    wg = w_in[:, o:o + 3 * N_HEADS]; o += 3 * N_HEADS
    ww = w_in[:, o:]
    per = 3 * GROUP
    wg2 = jnp.zeros((d, N_KV_HEADS * LANES), w_in.dtype)
    for h in range(N_KV_HEADS):
        wg2 = wg2.at[:, h * LANES:h * LANES + per].set(wg[:, h * per:(h + 1) * per])
    return wq.astype(BF16), wkv.astype(BF16), wg2.astype(BF16), ww.astype(BF16)


def _proj(x2d, pos, wparts, tm, seq_len=None):
    m, d = x2d.shape
    wq, wkv, wg, ww = wparts
    c, s1, s2 = _rope_tables(pos)
    dw = ww.shape[1] // 4
    row = lambda w: pl.BlockSpec((tm, w), lambda i: (i, 0))
    hm = lambda n: pl.BlockSpec((n, tm, HEAD_DIM), lambda i: (0, i, 0))
    rows_f32 = jax.ShapeDtypeStruct((m, KV_WIDTH), F32)
    hm_q = jax.ShapeDtypeStruct((N_HEADS, m, HEAD_DIM), BF16)
    hm_kv = jax.ShapeDtypeStruct((N_KV_HEADS, m, HEAD_DIM), BF16)
    wide = jax.ShapeDtypeStruct((m, dw), F32)
    if seq_len is None:
        kv_shapes = (rows_f32, rows_f32, rows_f32)
        kv_specs = (row(KV_WIDTH),) * 3
    else:
        nt = seq_len // tm
        kv_t = jax.ShapeDtypeStruct((m // seq_len, KV_WIDTH, seq_len), F32)
        t_spec = pl.BlockSpec((1, KV_WIDTH, tm), lambda i: (i // nt, 0, i % nt))
        kv_shapes = (rows_f32, kv_t, kv_t, kv_t)
        kv_specs = (row(KV_WIDTH), t_spec, t_spec, t_spec)
    out_shape = (hm_q, hm_q) + kv_shapes + (hm_kv,) * 4 + (
        jax.ShapeDtypeStruct((m, N_KV_HEADS * LANES), F32), wide, wide, wide, wide)
    out_specs = (hm(N_HEADS), hm(N_HEADS)) + kv_specs + (hm(N_KV_HEADS),) * 4 + (
        row(N_KV_HEADS * LANES), row(dw), row(dw), row(dw), row(dw))
    return pl.pallas_call(
        functools.partial(_proj_kernel, transposed=seq_len is not None),
        out_shape=out_shape,
        grid=(m // tm,),
        in_specs=[row(d), row(LANES), row(LANES), row(LANES),
                  _full(wq.shape), _full(wkv.shape), _full(wg.shape), _full(ww.shape)],
        out_specs=out_specs,
        compiler_params=_cparams(("parallel",)),
        name="proj",
    )(x2d, c, s1, s2, wq, wkv, wg, ww)


def _compress_weights(p):
    parts = CMP_BLOCK // CMP_STRIDE
    gw = N_GROUPS * CMP_HIDDEN
    w1 = jnp.zeros((CMP_STRIDE, N_GROUPS, HEAD_DIM, parts, N_GROUPS, CMP_HIDDEN), F32)
    pe = jnp.zeros((8, CMP_STRIDE, N_GROUPS, HEAD_DIM), F32)
    w2 = jnp.zeros((N_GROUPS, CMP_HIDDEN, N_GROUPS, HEAD_DIM), F32)
    for g in range(N_GROUPS):
        kv = g // N_KV_HEADS
        w1_src = (p['w_cmp_k1'], p['w_cmp_v1'])[kv].reshape(parts, CMP_STRIDE, HEAD_DIM, CMP_HIDDEN)
        pe_src = (p['cmp_pos_k'], p['cmp_pos_v'])[kv].reshape(parts, CMP_STRIDE, HEAD_DIM)
        w2_src = (p['w_cmp_k2'], p['w_cmp_v2'])[kv]
        for m in range(parts):
            w1 = w1.at[:, g, :, m, g, :].set(w1_src[m])
            pe = pe.at[m, :, g, :].set(pe_src[m])
        w2 = w2.at[g, :, g, :].set(w2_src)
    return (w1.reshape(CHUNK_W, parts * gw).astype(BF16), pe.reshape(8, CHUNK_W).astype(BF16),
            w2.reshape(gw, N_GROUPS * HEAD_DIM).astype(BF16))


def _compress_rows(c, pec, w2):
    gw = N_GROUPS * CMP_HIDDEN
    n = c.shape[0]
    const = pec[0:1, :gw] + pec[1:2, gw:]
    hid = c[:, :gw] + pltpu.roll(c[:, gw:], n - 1, 0) + const
    return _dot(_gelu(hid).astype(BF16), w2)


def _compress_prompt_kernel(ch_ref, w1_ref, pe_ref, w2_ref, kc_ref, vc_ref):
    c = _dot(ch_ref[0].astype(BF16), w1_ref[...])
    pec = _dot(pe_ref[...], w1_ref[...])
    out = _compress_rows(c, pec, w2_ref[...])
    for h in range(N_KV_HEADS):
        kc_ref[0, h] = out[:, HEAD_DIM * h:HEAD_DIM * (h + 1)].astype(BF16)
        vc_ref[0, h] = out[:, HEAD_DIM * (N_KV_HEADS + h):HEAD_DIM * (N_KV_HEADS + h + 1)].astype(BF16)


def _compress_prompt(kvc, b, t, cw):
    w1, pe, w2 = cw
    nch = t // CMP_STRIDE
    chunks = kvc.reshape(b, nch, CHUNK_W)
    out = jax.ShapeDtypeStruct((b, N_KV_HEADS, nch, HEAD_DIM), BF16)
    ospec = pl.BlockSpec((1, N_KV_HEADS, nch, HEAD_DIM), lambda i: (i, 0, 0, 0))
    return pl.pallas_call(
        _compress_prompt_kernel,
        out_shape=(out, out),
        grid=(b,),
        in_specs=[pl.BlockSpec((1, nch, CHUNK_W), lambda i: (i, 0, 0)),
                  _full(w1.shape), _full(pe.shape), _full(w2.shape)],
        out_specs=(ospec, ospec),
        compiler_params=_cparams(("parallel",)),
        name="compress_prompt",
    )(chunks, w1, pe, w2)


def _sel_map(n_cmp_pad, n_cmp, n_sel, n_sel_pad):
    ratio = SEL_BLOCK // CMP_STRIDE
    i = np.arange(n_cmp_pad)[:, None]
    j = np.arange(n_sel_pad)[None, :]
    d = i - ratio * j
    m = sum(((d + n >= 0) & (d + n < ratio)).astype(np.float32) for n in range(CMP_BLOCK // CMP_STRIDE))
    m = m * (i < n_cmp) * (j < n_sel)
    return jnp.asarray(m, BF16)


def _masked_softmax_rows(s, valid):
    s = jnp.where(valid, s, -jnp.inf)
    m = jnp.max(s, axis=-1, keepdims=True)
    m = jnp.where(m == -jnp.inf, 0.0, m)
    e = jnp.exp(s - m)
    return e / jnp.maximum(jnp.sum(e, axis=-1, keepdims=True), 1e-30)


def _top_rows(score, n):
    rows = score.shape[0]
    row = lax.broadcasted_iota(I32, score.shape, 0).astype(F32)
    sel = jnp.zeros(score.shape, F32)
    picked = []
    for _ in range(n):
        m = jnp.max(score, axis=0, keepdims=True)
        idx = jnp.min(jnp.where(score == m, row, float(rows)), axis=0, keepdims=True)
        pick = row == idx
        picked.append(idx)
        sel = jnp.where(pick, 1.0, sel)
        score = jnp.where(pick, -jnp.inf, score)
    return picked, sel


def _nsa_prompt_kernel(q_ref, qr_ref, kc_ref, vc_ref, ks_ref, vs_ref, kw_ref, vw_ref, g_ref, smap_ref,
                       o_ref, *, n_sel_blocks):
    qb = pl.program_id(2)
    s0 = qb * Q_BLOCK
    rows = GROUP * Q_BLOCK
    q = q_ref[...].reshape(rows, HEAD_DIM)
    qr = qr_ref[...].reshape(rows, HEAD_DIM)
    pos1 = s0 + lax.broadcasted_iota(I32, (Q_BLOCK, 1), 0)
    pos = s0 + (lax.broadcasted_iota(I32, (rows, 1), 0) & (Q_BLOCK - 1))

    kc = kc_ref[0, 0]
    ncp = kc.shape[0]
    c_end = lax.broadcasted_iota(I32, (1, ncp), 1) * CMP_STRIDE + (CMP_BLOCK - 1)
    p_c = _masked_softmax_rows(_dot_nt(q, kc), c_end <= pos).astype(BF16)
    o_c = _dot(p_c, vc_ref[0, 0])
    imp_t = _dot_nt(smap_ref[...], p_c[0:Q_BLOCK])
    for g in range(1, GROUP):
        imp_t = imp_t + _dot_nt(smap_ref[...], p_c[g * Q_BLOCK:(g + 1) * Q_BLOCK])

    wlen = WINDOW + Q_BLOCK
    wstart = pl.multiple_of(jnp.maximum(s0 - WINDOW, 0), Q_BLOCK)
    kpos = wstart + lax.broadcasted_iota(I32, (1, wlen), 1)
    diff = pos - kpos
    p_w = _masked_softmax_rows(_dot_nt(qr, kw_ref[0, pl.ds(wstart, wlen), :]), (diff >= 0) & (diff <= WINDOW))
    o_w = _dot(p_w.astype(BF16), vw_ref[0, pl.ds(wstart, wlen), :])

    blk_t = lax.broadcasted_iota(I32, (LANES, Q_BLOCK), 0)
    cur_t = (s0 + lax.broadcasted_iota(I32, (1, Q_BLOCK), 1)) // SEL_BLOCK
    causal_t = blk_t <= cur_t
    forced_t = ((blk_t == 0) | (blk_t == cur_t) | (blk_t == cur_t - 1)) & causal_t
    n_forced = 3
    _, ranked_t = _top_rows(jnp.where(causal_t & jnp.logical_not(forced_t), imp_t, -jnp.inf),
                            min(N_SEL, n_sel_blocks) - n_forced)
    sel_t = jnp.where(forced_t | (causal_t & (ranked_t > 0.5)), 1.0, 0.0)
    sel = sel_t.T.astype(BF16)

    per_tile = SEL_TK // SEL_BLOCK
    jrow = lax.broadcasted_iota(I32, (LANES, SEL_TK), 0)
    cblk = lax.broadcasted_iota(I32, (LANES, SEL_TK), 1) // SEL_BLOCK
    kcol = lax.broadcasted_iota(I32, (1, SEL_TK), 1)

    def sel_tile(kt, carry, causal):
        m_i, l_i, acc = carry
        start = pl.multiple_of(kt * SEL_TK, SEL_TK)
        s = _dot_nt(qr, ks_ref[0, pl.ds(start, SEL_TK), :]).reshape(GROUP, Q_BLOCK, SEL_TK)
        expand = jnp.where(jrow - kt * per_tile == cblk, 1.0, 0.0).astype(BF16)
        ok = _dot(sel, expand) > 0.5
        if causal:
            ok = ok & (start + kcol <= pos1)
        s = jnp.where(ok[None], s, NEG)
        m_new = jnp.maximum(m_i, jnp.max(s, axis=-1, keepdims=True))
        alpha = jnp.exp(m_i - m_new)
        p = jnp.exp(s - m_new)
        l_new = alpha * l_i + jnp.sum(p, axis=-1, keepdims=True)
        pv = _dot(p.reshape(rows, SEL_TK).astype(BF16), vs_ref[0, pl.ds(start, SEL_TK), :])
        return m_new, l_new, alpha * acc + pv.reshape(GROUP, Q_BLOCK, HEAD_DIM)

    init = (jnp.full((GROUP, Q_BLOCK, 1), NEG, F32), jnp.zeros((GROUP, Q_BLOCK, 1), F32),
            jnp.zeros((GROUP, Q_BLOCK, HEAD_DIM), F32))
    n_full = s0 // SEL_TK
    carry = lax.fori_loop(0, n_full, lambda kt, cr: sel_tile(kt, cr, False), init)
    _, l_s, acc_s = sel_tile(n_full, carry, True)
    o_s = (acc_s / l_s).reshape(rows, HEAD_DIM)

    gates = g_ref[...]
    outs = []
    for g in range(GROUP):
        r = slice(g * Q_BLOCK, (g + 1) * Q_BLOCK)
        outs.append(gates[:, 3 * g:3 * g + 1] * o_c[r] + gates[:, 3 * g + 1:3 * g + 2] * o_s[r]
                    + gates[:, 3 * g + 2:3 * g + 3] * o_w[r])
    o_ref[...] = jnp.concatenate(outs, axis=1).astype(o_ref.dtype)


def _nsa_prompt(q_hm, qr_hm, kc, vc, ksh, vsh, kwh, vwh, gates, b, t):
    nqb = t // Q_BLOCK
    n_sel_blocks = t // SEL_BLOCK
    ncp = kc.shape[2]
    assert N_SEL <= n_sel_blocks <= LANES and t % SEL_TK == 0 and t >= WINDOW + Q_BLOCK
    smap = _sel_map(ncp, ncp - 1, n_sel_blocks, LANES).T
    qspec = pl.BlockSpec((GROUP, Q_BLOCK, HEAD_DIM), lambda bi, h, i: (h, bi * nqb + i, 0))
    cspec = pl.BlockSpec((1, 1, ncp, HEAD_DIM), lambda bi, h, i: (bi, h, 0, 0))
    kspec = pl.BlockSpec((1, t, HEAD_DIM), lambda bi, h, i: (h, bi, 0))
    return pl.pallas_call(
        functools.partial(_nsa_prompt_kernel, n_sel_blocks=n_sel_blocks),
        out_shape=jax.ShapeDtypeStruct((b * t, Q_WIDTH), BF16),
        grid=(b, N_KV_HEADS, nqb),
        in_specs=[qspec, qspec, cspec, cspec, kspec, kspec, kspec, kspec,
                  pl.BlockSpec((Q_BLOCK, LANES), lambda bi, h, i: (bi * nqb + i, h)),
                  _full(smap.shape)],
        out_specs=pl.BlockSpec((Q_BLOCK, GROUP * HEAD_DIM), lambda bi, h, i: (bi * nqb + i, h)),
        compiler_params=_cparams(("parallel", "parallel", "arbitrary")),
        name="nsa_prompt",
    )(q_hm, qr_hm, kc, vc, ksh, vsh, kwh, vwh, gates, smap)


def _expm1(x):
    series = x * (1.0 + x * (1.0 / 2 + x * (1.0 / 6 + x * (1.0 / 24 + x * (1.0 / 120 + x * (1.0 / 720))))))
    return jnp.where(jnp.abs(x) < 0.1, series, jnp.exp(x) - 1.0)


def _log1p(z):
    series = z * (1.0 - z * (1.0 / 2 - z * (1.0 / 3 - z * (1.0 / 4 - z * (1.0 / 5 - z * (1.0 / 6 - z * (
        1.0 / 7 - z * (1.0 / 8))))))))
    return jnp.where(z < 0.1, series, jnp.log(1.0 + z))


def _rglru_gates(xc, ug, wa_ref, ba, wi_ref, bi, lam):
    xcb = xc.astype(BF16)
    bw = xc.shape[-1] // RNN_BLOCKS
    ra = jnp.concatenate([_dot(xcb[:, bw * n:bw * (n + 1)], wa_ref[n]) for n in range(RNN_BLOCKS)], axis=1)
    ri = jnp.concatenate([_dot(xcb[:, bw * n:bw * (n + 1)], wi_ref[n]) for n in range(RNN_BLOCKS)], axis=1)
    r = _sigmoid(ra + ba)
    i = _sigmoid(ri + bi)
    softplus = jnp.maximum(-lam, 0.0) + _log1p(jnp.exp(-jnp.abs(lam)))
    log_a = -RG_C * r * softplus
    a = jnp.exp(log_a)
    b = jnp.sqrt(-_expm1(2.0 * log_a)) * (i * xc)
    return a, b, _gelu(ug)


def _rglru_prompt_kernel(u_ref, ug_ref, cw_ref, cb_ref, wa_ref, ba_ref, wi_ref, bi_ref, lam_ref,
                         y_ref, h_ref, up_s, a_s, b_s, h_s):
    tt = u_ref.shape[1]
    t = pl.program_id(1)
    halo = CONV_W - 1

    @pl.when(t == 0)
    def _():
        up_s[0:8, :] = jnp.zeros((8, up_s.shape[1]), F32)
        h_s[...] = jnp.zeros(h_s.shape, F32)

    @pl.when(t > 0)
    def _():
        up_s[8 - halo:8, :] = up_s[8 + tt - halo:8 + tt, :]

    up_s[8:8 + tt, :] = u_ref[0]
    xc = cb_ref[...] + up_s[pl.ds(8 - halo, tt), :] * cw_ref[0:1, :]
    for k in range(1, CONV_W):
        xc = xc + up_s[pl.ds(8 - halo + k, tt), :] * cw_ref[k:k + 1, :]
    a, b, gate = _rglru_gates(xc, ug_ref[0], wa_ref, ba_ref[...], wi_ref, bi_ref[...], lam_ref[...])
    a_s[...] = a
    b_s[...] = b

    def step(i, h):
        h = a_s[pl.ds(i, 1), :] * h + b_s[pl.ds(i, 1), :]
        b_s[pl.ds(i, 1), :] = h
        return h

    h = lax.fori_loop(0, tt, step, h_s[...], unroll=8)
    h_s[...] = h
    h_ref[0] = h
    y_ref[0] = (b_s[...] * gate).astype(y_ref.dtype)


def _rglru_weights(p):
    row = lambda v: v.reshape(1, -1).astype(F32)
    return (p['conv_w'].astype(F32), row(p['conv_b']), p['w_rg_a'].astype(BF16), row(p['b_rg_a']),
            p['w_rg_i'].astype(BF16), row(p['b_rg_i']), row(p['lru_lambda']))


def _rglru_prompt(u, ug, rw, b, t, tt):
    d = u.shape[1]
    nt = t // tt
    seq = pl.BlockSpec((1, tt, d), lambda bi, i: (bi, i, 0))
    return pl.pallas_call(
        _rglru_prompt_kernel,
        out_shape=(jax.ShapeDtypeStruct((b, t, d), BF16), jax.ShapeDtypeStruct((b, 1, d), F32)),
        grid=(b, nt),
        in_specs=[seq, seq] + [_full(w.shape) for w in rw],
        out_specs=(seq, pl.BlockSpec((1, 1, d), lambda bi, i: (bi, 0, 0))),
        scratch_shapes=[pltpu.VMEM((tt + 8, d), F32), pltpu.VMEM((tt, d), F32), pltpu.VMEM((tt, d), F32),
                        pltpu.VMEM((1, d), F32)],
        compiler_params=_cparams(("parallel", "arbitrary")),
        name="rglru_prompt",
    )(u.reshape(b, t, d), ug.reshape(b, t, d), *rw)


def _rglru_step_kernel(u_ref, ug_ref, cs_ref, h0_ref, cw_ref, cb_ref, wa_ref, ba_ref, wi_ref, bi_ref, lam_ref,
                       y_ref, h_ref):
    xc = cb_ref[...] + u_ref[...] * cw_ref[CONV_W - 1:CONV_W, :]
    for k in range(CONV_W - 1):
        xc = xc + cs_ref[k] * cw_ref[k:k + 1, :]
    a, b, gate = _rglru_gates(xc, ug_ref[...], wa_ref, ba_ref[...], wi_ref, bi_ref[...], lam_ref[...])
    h = a * h0_ref[...] + b
    h_ref[...] = h
    y_ref[...] = (h * gate).astype(y_ref.dtype)


def _rglru_step(u, ug, conv_state, h0, rw):
    n, d = u.shape
    args = (u, ug, conv_state, h0) + tuple(rw)
    return pl.pallas_call(
        _rglru_step_kernel,
        out_shape=(jax.ShapeDtypeStruct((n, d), BF16), jax.ShapeDtypeStruct((n, d), F32)),
        grid=(1,),
        in_specs=[_full(a.shape) for a in args],
        out_specs=(_full((n, d)), _full((n, d))),
        compiler_params=_cparams(("arbitrary",)),
        name="rglru_step",
    )(*args)


_START, _WAIT = "start", "wait"


def _row_gather(rows, go, copy_fn, unroll=4):
    assert rows % 2 == 0

    def body(i, c):
        for lane in range(2):
            copy = copy_fn(2 * i + lane)
            if go == _START:
                copy.start(priority=lane)
            else:
                copy.wait()
        return c
    lax.fori_loop(0, rows // 2, body, 0, unroll=unroll)


def _nsa_sample_cmp_kernel(pt_ref, pool_ref, q_ref, w1_ref, pe_ref, w2_ref, smap_ref,
                           oc_ref, idx_ref, pbuf, rows_s, c_s, sem, *, n_seq, n_pages, q_pos, n_sel_blocks):
    b = pl.program_id(0)
    nb = n_seq
    n_slots, pg = pbuf.shape[0], pbuf.shape[1]
    groups = n_pages // pg
    ahead = n_slots - 1
    rows_pp = PAGE_SIZE // CMP_STRIDE

    def page_copy(seq, grp, slot, j):
        return pltpu.make_async_copy(pool_ref.at[pt_ref[seq * n_pages + grp * pg + j]], pbuf.at[slot, j],
                                     sem.at[slot])

    def issue(seq, grp, slot):
        _row_gather(pg, _START, lambda j: page_copy(seq, grp, slot, j))

    @pl.when(b == 0)
    def _():
        for g in range(ahead):
            if g // groups < n_seq:
                issue(g // groups, g % groups, g % n_slots)

    for k in range(groups):
        gi = b * groups + k
        slot = lax.rem(gi, n_slots)
        nxt = k + ahead
        seq_off, grp_nxt = nxt // groups, nxt % groups

        @pl.when(b + seq_off < nb)
        def _():
            issue(b + seq_off, grp_nxt, lax.rem(gi + ahead, n_slots))

        _row_gather(pg, _WAIT, lambda j: page_copy(b, k, slot, j))
        halves = KV_WIDTH // LANES
        for j in range(pg):
            for f in range(halves):
                rows_s[f, j * PAGE_SIZE:(j + 1) * PAGE_SIZE, :] = pbuf[slot, j, f * LANES:(f + 1) * LANES, :].T

        def chunk_rows(s):
            return jnp.concatenate([rows_s[f, pl.ds(s, pg * rows_pp, stride=CMP_STRIDE), :]
                                    for f in range(halves)], axis=1).astype(BF16)

        acc = _dot(chunk_rows(0), w1_ref[0])
        for s in range(1, CMP_STRIDE):
            acc = acc + _dot(chunk_rows(s), w1_ref[s])
        c_s[k * pg * rows_pp:(k + 1) * pg * rows_pp, :] = acc

    nch = n_pages * rows_pp
    pec = _dot(pe_ref[...], w1_ref[...].reshape(CHUNK_W, w1_ref.shape[2]))
    kv = _compress_rows(c_s[...], pec, w2_ref[...]).astype(BF16)

    q = q_ref[0]
    c_end = lax.broadcasted_iota(I32, (1, nch), 1) * CMP_STRIDE + (CMP_BLOCK - 1)
    row = lax.broadcasted_iota(I32, (N_HEADS, 1), 0)
    width = smap_ref.shape[1]
    blk_i = lax.broadcasted_iota(I32, (1, width), 1)
    cur = q_pos // SEL_BLOCK
    forced = (blk_i == 0) | (blk_i == cur) | (blk_i == cur - 1)
    o_c = jnp.zeros((N_HEADS, HEAD_DIM), F32)
    score = jnp.full((N_HEADS, width), -jnp.inf, F32)
    for h in range(N_KV_HEADS):
        kc = kv[:, HEAD_DIM * h:HEAD_DIM * (h + 1)]
        vc = kv[:, HEAD_DIM * (N_KV_HEADS + h):HEAD_DIM * (N_KV_HEADS + h + 1)]
        p = _masked_softmax_rows(_dot_nt(q, kc), c_end <= q_pos).astype(BF16)
        in_group = (row // GROUP) == h
        o_c = jnp.where(in_group, _dot(p, vc), o_c)
        imp = jnp.sum(jnp.where(in_group, _dot(p, smap_ref[...]), 0.0), axis=0, keepdims=True)
        sc = jnp.where(blk_i > cur, -jnp.inf, jnp.where(forced, SEL_FORCE, imp))
        score = jnp.where(row == h, sc, score)
    oc_ref[0] = o_c

    n = min(N_SEL, n_sel_blocks)
    lane = lax.broadcasted_iota(I32, (1, LANES), 1)
    idx_out = jnp.zeros((N_HEADS, LANES), I32)
    blk = blk_i.astype(F32)
    for r in range(n):
        m = jnp.max(score, axis=-1, keepdims=True)
        idx = jnp.min(jnp.where(score == m, blk, float(width)), axis=-1, keepdims=True)
        score = jnp.where(blk == idx, -jnp.inf, score)
        idx_out = jnp.where(lane == r, idx.astype(I32), idx_out)
    idx_ref[0] = idx_out


CMP_PAGE_GROUP = 32
CMP_GROUP_SLOTS = 3


def _pages_feature_major(pool):
    n_pool, rows = pool.shape[0], pool.shape[1]
    return jnp.transpose(pool, (0, 2, 3, 4, 1)).reshape(n_pool, KV_WIDTH, rows)


def _nsa_sample_cmp(page_table, pool_t, q_seq, cw, past_len):
    bd, n_pages = page_table.shape
    w1, pe, w2 = cw
    w1 = w1.reshape(CMP_STRIDE, KV_WIDTH, w1.shape[1])
    rows_pp = PAGE_SIZE // CMP_STRIDE
    nch = n_pages * rows_pp
    pg = min(CMP_PAGE_GROUP, n_pages)
    assert n_pages % pg == 0
    n_sel_blocks = past_len // SEL_BLOCK + 1
    width = -(-n_sel_blocks // LANES) * LANES
    smap = _sel_map(nch, nch - 1, n_sel_blocks, width)
    grid_spec = pltpu.PrefetchScalarGridSpec(
        num_scalar_prefetch=1,
        grid=(bd,),
        in_specs=[pl.BlockSpec(memory_space=pl.ANY),
                  pl.BlockSpec((1, N_HEADS, HEAD_DIM), lambda i, pt: (i, 0, 0)),
                  pl.BlockSpec(w1.shape, lambda i, pt: (0, 0, 0)),
                  pl.BlockSpec(pe.shape, lambda i, pt: (0, 0)),
                  pl.BlockSpec(w2.shape, lambda i, pt: (0, 0)),
                  pl.BlockSpec(smap.shape, lambda i, pt: (0, 0))],
        out_specs=(pl.BlockSpec((1, N_HEADS, HEAD_DIM), lambda i, pt: (i, 0, 0)),
                   pl.BlockSpec((1, N_HEADS, LANES), lambda i, pt: (i, 0, 0))),
        scratch_shapes=[pltpu.VMEM((CMP_GROUP_SLOTS, pg, KV_WIDTH, PAGE_SIZE), F32),
                        pltpu.VMEM((KV_WIDTH // LANES, pg * PAGE_SIZE, LANES), F32),
                        pltpu.VMEM((nch, w1.shape[2]), F32),
                        pltpu.SemaphoreType.DMA((CMP_GROUP_SLOTS,))],
    )
    return pl.pallas_call(
        functools.partial(_nsa_sample_cmp_kernel, n_seq=bd, n_pages=n_pages, q_pos=past_len,
                          n_sel_blocks=n_sel_blocks),
        out_shape=(jax.ShapeDtypeStruct((bd, N_HEADS, HEAD_DIM), F32),
                   jax.ShapeDtypeStruct((bd, N_HEADS, LANES), I32)),
        grid_spec=grid_spec,
        compiler_params=_cparams(("arbitrary",)),
        name="nsa_sample_cmp",
    )(page_table.reshape(-1), pool_t, q_seq, w1, pe, w2, smap)


def _nsa_sample_sel_kernel(pt_ref, idx_s_ref, pool_ref, qr_ref, idx_ref, kvs_ref, kvw_ref, win_ref, oc_ref, g_ref,
                           o_ref, wout_ref, sbuf, sem, *, n_seq, n_pages, n_sel, ns_past):
    b = pl.program_id(0)
    per_page = PAGE_SIZE // SEL_BLOCK
    nblk = N_KV_HEADS * n_sel
    nkeys = n_sel * PAGE_SIZE

    def slab_copy(seq, slot, j):
        blk, kv = j // 2, j % 2
        jp = jnp.minimum(idx_s_ref[seq * nblk + blk], ns_past - 1)
        page = pt_ref[seq * n_pages + jp // per_page]
        feat = pl.multiple_of((kv * N_KV_HEADS + blk // n_sel) * HEAD_DIM, HEAD_DIM)
        return pltpu.make_async_copy(pool_ref.at[page, pl.ds(feat, HEAD_DIM), :], sbuf.at[slot, blk, kv],
                                     sem.at[slot])

    @pl.when(b == 0)
    def _():
        _row_gather(2 * nblk, _START, lambda j: slab_copy(0, 0, j))

    @pl.when(b + 1 < n_seq)
    def _():
        _row_gather(2 * nblk, _START, lambda j: slab_copy(b + 1, (b + 1) % 2, j))

    slot = b % 2
    _row_gather(2 * nblk, _WAIT, lambda j: slab_copy(b, slot, j))

    qr = qr_ref[0]
    qf = qr.astype(F32)
    row = lax.broadcasted_iota(I32, (N_HEADS, 1), 0)
    key = lax.broadcasted_iota(I32, (1, nkeys), 1)
    key_blk = ((key % PAGE_SIZE) // SEL_BLOCK).astype(F32)
    first = (key % SEL_BLOCK) == 0
    expand = jnp.where(lax.broadcasted_iota(I32, (LANES, nkeys), 0)
                       == lax.broadcasted_iota(I32, (LANES, nkeys), 1) // PAGE_SIZE, 1.0, 0.0).astype(BF16)
    chosen = idx_ref[0]
    in_page = (jnp.minimum(chosen, ns_past - 1) % per_page).astype(F32).astype(BF16)
    want_blk = _dot(in_page, expand)
    is_new = _dot(jnp.where(chosen >= ns_past, 1.0, 0.0).astype(BF16), expand) > 0.5
    kvs_new = kvs_ref[0].astype(BF16).astype(F32)
    kvw_new = kvw_ref[0].astype(BF16).astype(F32)
    o_s = jnp.zeros((N_HEADS, HEAD_DIM), F32)
    o_w = jnp.zeros((N_HEADS, HEAD_DIM), F32)
    for h in range(N_KV_HEADS):
        in_group = (row // GROUP) == h
        ksl = slice(HEAD_DIM * h, HEAD_DIM * (h + 1))
        vsl = slice(HEAD_DIM * (N_KV_HEADS + h), HEAD_DIM * (N_KV_HEADS + h + 1))
        k_t = jnp.concatenate([sbuf[slot, h * n_sel + n, 0] for n in range(n_sel)], axis=1).astype(BF16)
        v_t = jnp.concatenate([sbuf[slot, h * n_sel + n, 1] for n in range(n_sel)], axis=1).astype(BF16)
        in_blk = want_blk[h:h + 1] == key_blk
        newblk = is_new[h:h + 1]
        newkey = in_blk & newblk & first
        s_new = jnp.sum(qf * kvs_new[:, ksl], axis=-1, keepdims=True)
        s = jnp.where(newkey, s_new, _dot(qr, k_t))
        p = _masked_softmax_rows(s, in_blk & (jnp.logical_not(newblk) | first))
        p_new = jnp.sum(jnp.where(newkey, p, 0.0), axis=-1, keepdims=True).astype(BF16).astype(F32)
        o_h = _dot_nt(jnp.where(newkey, 0.0, p).astype(BF16), v_t) + p_new * kvs_new[:, vsl]
        o_s = jnp.where(in_group, o_h, o_s)
        s_buf = _dot(qr, win_ref[0, ksl, :].astype(BF16))
        s_cur = jnp.sum(qf * kvw_new[:, ksl], axis=-1, keepdims=True)
        m = jnp.maximum(jnp.max(s_buf, axis=-1, keepdims=True), s_cur)
        e_buf = jnp.exp(s_buf - m)
        e_cur = jnp.exp(s_cur - m)
        den = jnp.maximum(jnp.sum(e_buf, axis=-1, keepdims=True) + e_cur, 1e-30)
        o_h = (_dot_nt((e_buf / den).astype(BF16), win_ref[0, vsl, :].astype(BF16))
               + (e_cur / den).astype(BF16).astype(F32) * kvw_new[:, vsl])
        o_w = jnp.where(in_group, o_h, o_w)

    gw = g_ref.shape[-1]
    gl = lax.broadcasted_iota(I32, (N_HEADS, gw), 1)
    gbase = (row // GROUP) * LANES + (row % GROUP) * 3
    gates = jnp.broadcast_to(g_ref[0], (N_HEADS, gw))
    gate = lambda j: jnp.sum(jnp.where(gl == gbase + j, gates, 0.0), axis=-1, keepdims=True)
    o_ref[0] = gate(0) * oc_ref[0] + gate(1) * o_s + gate(2) * o_w

    n_chunks = win_ref.shape[2] // LANES
    new_col = jnp.broadcast_to(kvw_ref[0], (LANES, KV_WIDTH)).T
    last_lane = lax.broadcasted_iota(I32, (1, LANES), 1) == LANES - 1
    for c in range(n_chunks):
        cur = win_ref[0, :, c * LANES:(c + 1) * LANES]
        nxt = win_ref[0, :, (c + 1) * LANES:(c + 2) * LANES] if c + 1 < n_chunks else new_col
        wout_ref[0, :, c * LANES:(c + 1) * LANES] = jnp.where(last_lane, pltpu.roll(nxt, LANES - 1, 1),
                                                             pltpu.roll(cur, LANES - 1, 1))


def _nsa_sample_sel(page_table, idx, pool_t, qr_seq, kvs_new, kvw_new, win_t, o_c, gates, past_len):
    bd, n_pages = page_table.shape
    n_sel = min(N_SEL, past_len // SEL_BLOCK + 1)
    n_buf = win_t.shape[2]
    assert n_buf % LANES == 0
    gw = gates.shape[-1]
    seq3 = lambda s1, s2: pl.BlockSpec((1, s1, s2), lambda i, pt, ix: (i, 0, 0))
    grid_spec = pltpu.PrefetchScalarGridSpec(
        num_scalar_prefetch=2,
        grid=(bd,),
        in_specs=[pl.BlockSpec(memory_space=pl.ANY), seq3(N_HEADS, HEAD_DIM), seq3(N_HEADS, LANES),
                  seq3(1, KV_WIDTH), seq3(1, KV_WIDTH), seq3(KV_WIDTH, n_buf), seq3(N_HEADS, HEAD_DIM),
                  seq3(1, gw)],
        out_specs=(seq3(N_HEADS, HEAD_DIM), seq3(KV_WIDTH, n_buf)),
        scratch_shapes=[pltpu.VMEM((2, N_KV_HEADS * n_sel, 2, HEAD_DIM, PAGE_SIZE), F32),
                        pltpu.SemaphoreType.DMA((2,))],
    )
    idx_flat = idx[:, :N_KV_HEADS, :n_sel].reshape(-1)
    return pl.pallas_call(
        functools.partial(_nsa_sample_sel_kernel, n_seq=bd, n_pages=n_pages, n_sel=n_sel,
                          ns_past=past_len // SEL_BLOCK),
        out_shape=(jax.ShapeDtypeStruct((bd, N_HEADS, HEAD_DIM), F32),
                   jax.ShapeDtypeStruct((bd, KV_WIDTH, n_buf), F32)),
        grid_spec=grid_spec,
        compiler_params=_cparams(("arbitrary",)),
        name="nsa_sample_sel",
    )(page_table.reshape(-1), idx_flat, pool_t, qr_seq, idx, kvs_new.reshape(bd, 1, KV_WIDTH),
      kvw_new.reshape(bd, 1, KV_WIDTH), win_t, o_c, gates.reshape(bd, 1, gw))


def _tail_kernel(x_ref, o_ref, y_ref, ga_ref, gb_ref, wa_ref, wb_ref, wo_ref, g1_ref, b1_ref, wr_ref, rb_ref,
                 cin_ref, x1_ref, idx_ref, w_ref, pos_ref, cnt_ref, carry_s, *, alpha):
    i = pl.program_id(0)
    tm = x_ref.shape[0]
    ne = wr_ref.shape[0]

    @pl.when(i == 0)
    def _():
        carry_s[...] = cin_ref[...]

    merged = (_sigmoid(ga_ref[...]) * _dot(o_ref[...], wa_ref[...])
              + _sigmoid(gb_ref[...]) * _dot(y_ref[...], wb_ref[...]))
    mix = _dot(merged.astype(BF16), wo_ref[...])
    x1 = _layer_norm(alpha * x_ref[...] + mix, g1_ref[...], b1_ref[...])
    x1_ref[...] = x1

    scores = _sigmoid(_dot_nt(wr_ref[...], x1.astype(BF16)))
    picked, hits = _top_rows(scores + rb_ref[...], TOP_K)
    expert = lax.broadcasted_iota(I32, (ne, tm), 0).astype(F32)
    chosen = [jnp.sum(jnp.where(expert == e_k, scores, 0.0), axis=0, keepdims=True) for e_k in picked]
    total = chosen[0]
    for c in chosen[1:]:
        total = total + c

    earlier = lax.broadcasted_iota(I32, (tm, tm), 0) < lax.broadcasted_iota(I32, (tm, tm), 1)
    prefix = _dot(hits.astype(BF16), jnp.where(earlier, 1.0, 0.0).astype(BF16)) + carry_s[...]
    out_row = lax.broadcasted_iota(I32, (idx_ref.shape[0], 1), 0)
    idx_out = jnp.zeros(idx_ref.shape, I32)
    w_out = jnp.zeros(w_ref.shape, F32)
    pos_out = jnp.zeros(pos_ref.shape, I32)
    for k in range(TOP_K):
        p_k = jnp.sum(jnp.where(expert == picked[k], prefix, 0.0), axis=0, keepdims=True)
        idx_out = jnp.where(out_row == k, picked[k].astype(I32), idx_out)
        pos_out = jnp.where(out_row == k, p_k.astype(I32), pos_out)
        w_out = jnp.where(out_row == k, chosen[k] / total * ROUTED_SCALE, w_out)
    idx_ref[...] = idx_out
    w_ref[...] = w_out
    pos_ref[...] = pos_out
    carry_s[...] = carry_s[...] + jnp.sum(hits, axis=1, keepdims=True)
    cnt_ref[...] = carry_s[...]


def _tail_weights(p):
    row = lambda v: v.reshape(1, -1).astype(F32)
    return (p['w_branch_attn'].astype(BF16), p['w_branch_rnn'].astype(BF16), p['w_out'].astype(BF16),
            row(p['ln1_g']), row(p['ln1_b']), p['w_router'].T.astype(BF16),
            p['router_bias'].reshape(-1, 1).astype(F32))


def _tail(x2d, o_attn, y_rnn, g_a, g_b, tw, counts_in, alpha, tm):
    m, d = x2d.shape
    ne = tw[5].shape[0]
    row = lambda w: pl.BlockSpec((tm, w), lambda i: (i, 0))
    slots = pl.BlockSpec((8, tm), lambda i: (0, i))
    small = jax.ShapeDtypeStruct((8, m), I32)
    return pl.pallas_call(
        functools.partial(_tail_kernel, alpha=alpha),
        out_shape=(jax.ShapeDtypeStruct((m, d), F32), small, jax.ShapeDtypeStruct((8, m), F32), small,
                   jax.ShapeDtypeStruct((ne, 1), F32)),
        grid=(m // tm,),
        in_specs=[row(d), row(o_attn.shape[1]), row(y_rnn.shape[1]), row(d), row(d)]
                 + [_full(w.shape) for w in tw] + [_full((ne, 1))],
        out_specs=(row(d), slots, slots, slots, _full((ne, 1))),
        scratch_shapes=[pltpu.VMEM((ne, 1), F32)],
        compiler_params=_cparams(("arbitrary",)),
        name="tail",
    )(x2d, o_attn, y_rnn, g_a, g_b, *tw, counts_in)


def _silu(x):
    return x * _sigmoid(x)


def _pack_bf16_pairs(x):
    half = x.shape[1] // 2
    bits = lax.bitcast_convert_type(x.astype(BF16).astype(F32), jnp.uint32)
    return bits[:, :half] | (bits[:, half:] >> 16)


def _dispatch_kernel(dest_ref, xp_ref, xs_ref, out_ref, wbuf, zbuf, sem, *, n_tok, tail):
    i = pl.program_id(0)
    n = pl.num_programs(0)
    tm = xp_ref.shape[0]
    n_sample = xs_ref.shape[0]
    slot = i % 2

    def scatter(tile, rows, slot_, go):
        def row_copy(k, t):
            dst = dest_ref[k * n_tok + tile * tm + t]
            return pltpu.make_async_copy(wbuf.at[slot_, pl.ds(t, 1)], out_ref.at[pl.ds(dst, 1)], sem.at[slot_])
        for k in range(TOP_K):
            _row_gather(rows, go, functools.partial(row_copy, k))

    if tail is not None:
        tail_copy = pltpu.make_async_copy(zbuf, out_ref.at[pl.ds(tail[0], tail[1])], sem.at[2])

        @pl.when(i == 0)
        def _():
            zbuf[...] = jnp.zeros(zbuf.shape, zbuf.dtype)
            tail_copy.start()

    @pl.when(i >= 2)
    def _():
        scatter(i - 2, tm, slot, _WAIT)

    @pl.when(i < n - 1)
    def _():
        wbuf[slot] = _pack_bf16_pairs(xp_ref[...])
        scatter(i, tm, slot, _START)

    @pl.when(i == n - 1)
    def _():
        wbuf[slot, 0:n_sample, :] = _pack_bf16_pairs(xs_ref[...])
        scatter(i, n_sample, slot, _START)
        scatter(i, n_sample, slot, _WAIT)
        if tail is not None:
            tail_copy.wait()

    @pl.when((i == n - 1) & (i >= 1))
    def _():
        scatter(i - 1, tm, 1 - slot, _WAIT)


def _dispatch(dest, x1_p, x1_s, n_rows_pad, tm):
    m, d = x1_p.shape
    n_rows = dest.shape[0]
    n_tiles = m // tm
    assert n_rows == (m + x1_s.shape[0]) * TOP_K and x1_s.shape[0] <= tm
    tail = None if n_rows_pad == n_rows else (n_rows, n_rows_pad - n_rows)
    grid_spec = pltpu.PrefetchScalarGridSpec(
        num_scalar_prefetch=1,
        grid=(n_tiles + 1,),
        in_specs=[pl.BlockSpec((tm, d), lambda i, ds: (jnp.minimum(i, n_tiles - 1), 0)),
                  pl.BlockSpec(x1_s.shape, lambda i, ds: (0, 0))],
        out_specs=pl.BlockSpec(memory_space=pl.ANY),
        scratch_shapes=[pltpu.VMEM((2, tm, d // 2), jnp.uint32),
                        pltpu.VMEM((8 if tail is None else tail[1], d // 2), jnp.uint32),
                        pltpu.SemaphoreType.DMA((3,))],
    )
    return pl.pallas_call(
        functools.partial(_dispatch_kernel, n_tok=n_rows // TOP_K, tail=tail),
        out_shape=jax.ShapeDtypeStruct((n_rows_pad, d // 2), jnp.uint32),
        grid_spec=grid_spec,
        compiler_params=_cparams(("arbitrary",)),
        name="dispatch",
    )(dest, x1_p, x1_s)


def _experts_kernel(vb_ref, ve_ref, lo_ref, hi_ref, xs_ref, wg_ref, wu_ref, wd_ref, y_ref, wgb, wub, wdb):
    v = pl.program_id(0)
    lo, hi = lo_ref[v], hi_ref[v]
    prev = jnp.maximum(v - 1, 0)
    first_of_block = (v == 0) | (vb_ref[v] != vb_ref[prev])

    @pl.when((v == 0) | (ve_ref[v] != ve_ref[prev]))
    def _():
        wgb[...] = wg_ref[0].astype(BF16)
        wub[...] = wu_ref[0].astype(BF16)
        wdb[...] = wd_ref[0].astype(BF16)

    @pl.when(hi > lo)
    def _():
        words = xs_ref[...]
        x = jnp.concatenate(
            [lax.bitcast_convert_type(words & jnp.uint32(0xFFFF0000), F32).astype(BF16),
             lax.bitcast_convert_type(words << 16, F32).astype(BF16)], axis=1)
        hid = _silu(_dot(x, wgb[...])) * _dot(x, wub[...])
        val = _dot(hid.astype(BF16), wdb[...])
        row = lax.broadcasted_iota(I32, (val.shape[0], 1), 0)
        mine = (row >= lo) & (row < hi)

        @pl.when(first_of_block)
        def _():
            y_ref[...] = jnp.where(mine, val, 0.0)

        @pl.when(jnp.logical_not(first_of_block))
        def _():
            y_ref[...] = jnp.where(mine, val, y_ref[...])


def _experts(xs, visits, w_gate, w_up, w_down):
    vblk, vexp, vlo, vhi = visits
    d, de = w_gate.shape[1], w_gate.shape[2]
    rb = EXPERT_ROWS
    grid_spec = pltpu.PrefetchScalarGridSpec(
        num_scalar_prefetch=4,
        grid=(vblk.shape[0],),
        in_specs=[pl.BlockSpec((rb, d // 2), lambda v, vb, ve, lo, hi: (vb[v], 0)),
                  pl.BlockSpec((1, d, de), lambda v, vb, ve, lo, hi: (ve[v], 0, 0)),
                  pl.BlockSpec((1, d, de), lambda v, vb, ve, lo, hi: (ve[v], 0, 0)),
                  pl.BlockSpec((1, de, d), lambda v, vb, ve, lo, hi: (ve[v], 0, 0))],
        out_specs=pl.BlockSpec((rb, d), lambda v, vb, ve, lo, hi: (vb[v], 0)),
        scratch_shapes=[pltpu.VMEM((d, de), BF16), pltpu.VMEM((d, de), BF16), pltpu.VMEM((de, d), BF16)],
    )
    return pl.pallas_call(
        _experts_kernel,
        out_shape=jax.ShapeDtypeStruct((xs.shape[0], d), F32),
        grid_spec=grid_spec,
        compiler_params=_cparams(("arbitrary",)),
        name="experts",
    )(vblk, vexp, vlo, vhi, xs, w_gate, w_up, w_down)


def _combine_kernel(dest_ref, x1_ref, w_ref, y_ref, wsg_ref, wsu_ref, wsd_ref, g2_ref, b2_ref, out_ref,
                    ybuf, sem, *, n_tok, tok_off, alpha):
    i = pl.program_id(0)
    n = pl.num_programs(0)
    tc = x1_ref.shape[0]

    def row_copy(tile, slot, k, t):
        src = dest_ref[k * n_tok + tok_off + tile * tc + t]
        return pltpu.make_async_copy(y_ref.at[pl.ds(src, 1)], ybuf.at[slot, k, pl.ds(t, 1)], sem.at[slot])

    def gather(tile, slot, go):
        for k in range(TOP_K):
            _row_gather(tc, go, lambda t: row_copy(tile, slot, k, t))

    @pl.when(i == 0)
    def _():
        gather(0, 0, _START)

    @pl.when(i + 1 < n)
    def _():
        gather(i + 1, (i + 1) % 2, _START)

    slot = i % 2
    gather(i, slot, _WAIT)
    x1 = x1_ref[...]
    w = w_ref[...]
    routed = w[:, 0:1] * ybuf[slot, 0]
    for k in range(1, TOP_K):
        routed = routed + w[:, k:k + 1] * ybuf[slot, k]
    xb = x1.astype(BF16)
    shared = _dot((_silu(_dot(xb, wsg_ref[...])) * _dot(xb, wsu_ref[...])).astype(BF16), wsd_ref[...])
    out_ref[...] = _layer_norm(alpha * x1 + (routed + shared), g2_ref[...], b2_ref[...])


def _combine_weights(p):
    row = lambda v: v.reshape(1, -1).astype(F32)
    return (p['w_sh_gate'].astype(BF16), p['w_sh_up'].astype(BF16), p['w_sh_down'].astype(BF16),
            row(p['ln2_g']), row(p['ln2_b']))


def _combine(dest, x1, w, y_rows, cw, tok_off, alpha, tc):
    m, d = x1.shape
    row = lambda width: pl.BlockSpec((tc, width), lambda i, ds: (i, 0))
    grid_spec = pltpu.PrefetchScalarGridSpec(
        num_scalar_prefetch=1,
        grid=(m // tc,),
        in_specs=[row(d), row(w.shape[1]), pl.BlockSpec(memory_space=pl.ANY)]
                 + [pl.BlockSpec(a.shape, lambda i, ds: (0, 0)) for a in cw],
        out_specs=row(d),
        scratch_shapes=[pltpu.VMEM((2, TOP_K, tc, d), F32), pltpu.SemaphoreType.DMA((2,))],
    )
    return pl.pallas_call(
        functools.partial(_combine_kernel, n_tok=dest.shape[0] // TOP_K, tok_off=tok_off, alpha=alpha),
        out_shape=jax.ShapeDtypeStruct((m, d), F32),
        grid_spec=grid_spec,
        compiler_params=_cparams(("arbitrary",)),
        name="combine",
    )(dest, x1, w, y_rows, *cw)


def _lookup(table, idx):
    hit = idx[..., None] == jnp.arange(table.shape[0], dtype=idx.dtype)
    return jnp.sum(jnp.where(hit, table, 0), axis=-1)


def _route(idx, pos, counts):
    n_tok = idx.shape[1]
    ne = counts.shape[0]
    rb = EXPERT_ROWS
    n_blocks = -(-n_tok * TOP_K // rb)
    end = jnp.cumsum(counts)
    start = end - counts
    dest = _lookup(start, idx) + pos
    first_blk = start // rb
    n_vis = jnp.where(counts > 0, (end - 1) // rb - first_blk + 1, 0)
    vend = jnp.cumsum(n_vis)
    vstart = vend - n_vis
    v = jnp.arange(n_blocks + ne - 1, dtype=I32)
    valid = v < vend[-1]
    v_c = jnp.minimum(v, vend[-1] - 1)
    vexp = jnp.minimum(jnp.sum((vend[None, :] <= v_c[:, None]).astype(I32), axis=1), ne - 1)
    of_visit = lambda a: _lookup(a, vexp)
    vblk = of_visit(first_blk) + (v_c - of_visit(vstart))
    vlo = jnp.where(valid, jnp.clip(of_visit(start) - vblk * rb, 0, rb), 0)
    vhi = jnp.where(valid, jnp.clip(of_visit(end) - vblk * rb, 0, rb), 0)
    visits = tuple(a.astype(I32) for a in (vblk, vexp, vlo, vhi))
    return dest.reshape(-1).astype(I32), visits, n_blocks * rb


def _layer(xp, xs, caches, page_table, p, depth):
    b, t, d = xp.shape
    bd, s, _ = xs.shape
    assert s == 1, "sample group is one new token per sequence"
    pool_c, pool_s, win_buf, state_conv, state_rnn = caches
    past_len = page_table.shape[1] * PAGE_SIZE
    alpha = (2.0 * depth) ** 0.25
    kv6 = lambda a, n, rows: a.reshape(n, rows, 2, N_KV_HEADS, HEAD_DIM)
    rows_major = lambda a: jnp.transpose(a.reshape(a.shape[0], 2, N_KV_HEADS, HEAD_DIM, a.shape[2]), (0, 4, 1, 2, 3))

    wparts = _split_w_in(p['w_in'])
    cw = _compress_weights(p)
    rw = _rglru_weights(p)
    tw = _tail_weights(p)
    mw = _combine_weights(p)

    pos_p = jnp.tile(jnp.arange(t, dtype=I32), b)
    (q, qr, kvc, kvc_t, kvs_t, kvw_t, ksh, vsh, kwh, vwh, gates, u_rnn, u_gate, g_a, g_b) = _proj(
        xp.reshape(b * t, d), pos_p, wparts, 256, seq_len=t)
    kc, vc = _compress_prompt(kvc, b, t, cw)
    o_attn = _nsa_prompt(q, qr, kc, vc, ksh, vsh, kwh, vwh, gates, b, t)
    y_rnn, h_p = _rglru_prompt(u_rnn, u_gate, rw, b, t, 256)
    ne = p['w_router'].shape[1]
    x1_p, idx_p, w_p, pos_r_p, counts_p = _tail(xp.reshape(b * t, d), o_attn, y_rnn.reshape(b * t, -1), g_a, g_b,
                                                tw, jnp.zeros((ne, 1), F32), alpha, 256)
    wn = min(WINDOW, t)
    outs_p = (rows_major(kvc_t), rows_major(kvs_t), rows_major(kvw_t[:, :, t - wn:]),
              u_rnn.reshape(b, t, -1)[:, t - (CONV_W - 1):], h_p.reshape(b, -1))

    pos_s = jnp.full((bd,), past_len, I32)
    (q, qr, kvc_s, kvs_s, kvw_s, _, _, _, _, gates_s, u_rnn_s, u_gate_s, g_a_s, g_b_s) = _proj(
        xs.reshape(bd, d), pos_s, wparts, bd)
    o_c, sel_idx = _nsa_sample_cmp(page_table, _pages_feature_major(pool_c), q.transpose(1, 0, 2), cw, past_len)
    o_s, win_new_t = _nsa_sample_sel(page_table, sel_idx, _pages_feature_major(pool_s), qr.transpose(1, 0, 2),
                                     kvs_s, kvw_s, _pages_feature_major(win_buf), o_c, gates_s, past_len)
    y_rnn_s, h_s = _rglru_step(u_rnn_s, u_gate_s, state_conv.transpose(1, 0, 2), state_rnn, rw)
    x1_s, idx_s, w_s, pos_r_s, counts = _tail(xs.reshape(bd, d), o_s.reshape(bd, Q_WIDTH).astype(BF16), y_rnn_s,
                                              g_a_s, g_b_s, tw, counts_p, alpha, bd)
    conv_s = jnp.concatenate([state_conv[:, 1:], u_rnn_s[:, None, :]], axis=1)
    outs_s = (kv6(kvc_s, bd, 1), kv6(kvs_s, bd, 1), rows_major(win_new_t), conv_s, h_s)

    idx_all = jnp.concatenate([idx_p, idx_s], axis=1)[:TOP_K]
    pos_all = jnp.concatenate([pos_r_p, pos_r_s], axis=1)[:TOP_K]
    dest, visits, n_rows_pad = _route(idx_all, pos_all, counts.reshape(-1).astype(I32))
    xs = _dispatch(dest, x1_p, x1_s, n_rows_pad, 256)
    y_rows = _experts(xs, visits, p['w_exp_gate'], p['w_exp_up'], p['w_exp_down'])
    yp = _combine(dest, x1_p, w_p.T, y_rows, mw, 0, alpha, 128)
    ys = _combine(dest, x1_s, w_s.T, y_rows, mw, b * t, alpha, bd)
    return yp.reshape(b, t, d), ys.reshape(bd, s, d), outs_p, outs_s


def kernel(x_prompt, x_sample, cache_cmp_kv, cache_sel_kv, cache_win_kv, state_conv, state_rnn, page_table,
           w_in, conv_w, conv_b, w_rg_a, b_rg_a, w_rg_i, b_rg_i, lru_lambda, cmp_pos_k, cmp_pos_v,
           w_cmp_k1, w_cmp_k2, w_cmp_v1, w_cmp_v2, w_branch_attn, w_branch_rnn, w_out, ln1_g, ln1_b,
           w_router, router_bias, w_exp_gate, w_exp_up, w_exp_down, w_sh_gate, w_sh_up, w_sh_down, ln2_g, ln2_b):
    weights = dict(w_in=w_in, conv_w=conv_w, conv_b=conv_b, w_rg_a=w_rg_a, b_rg_a=b_rg_a, w_rg_i=w_rg_i,
                   b_rg_i=b_rg_i, lru_lambda=lru_lambda, cmp_pos_k=cmp_pos_k, cmp_pos_v=cmp_pos_v,
                   w_cmp_k1=w_cmp_k1, w_cmp_k2=w_cmp_k2, w_cmp_v1=w_cmp_v1, w_cmp_v2=w_cmp_v2,
                   w_branch_attn=w_branch_attn, w_branch_rnn=w_branch_rnn, w_out=w_out, ln1_g=ln1_g, ln1_b=ln1_b,
                   w_router=w_router, router_bias=router_bias, w_exp_gate=w_exp_gate, w_exp_up=w_exp_up,
                   w_exp_down=w_exp_down, w_sh_gate=w_sh_gate, w_sh_up=w_sh_up, w_sh_down=w_sh_down,
                   ln2_g=ln2_g, ln2_b=ln2_b)
    depth = w_in.shape[0]
    xp, xs = x_prompt, x_sample
    per_layer_p, per_layer_s = [], []
    for l in range(depth):
        p = {k: v[l] for k, v in weights.items()}
        caches = (cache_cmp_kv[l], cache_sel_kv[l], cache_win_kv[l], state_conv[l], state_rnn[l])
        xp, xs, outs_p, outs_s = _layer(xp, xs, caches, page_table, p, depth)
        per_layer_p.append(outs_p)
        per_layer_s.append(outs_s)
    stack = lambda outs, i: jnp.stack([o[i] for o in outs])
    return (xp, xs, stack(per_layer_p, 0), stack(per_layer_s, 0), stack(per_layer_p, 1), stack(per_layer_s, 1),
            stack(per_layer_p, 2), stack(per_layer_s, 2), stack(per_layer_p, 3), stack(per_layer_s, 3),
            stack(per_layer_p, 4), stack(per_layer_s, 4))
```

```python
import functools

import numpy as np
import jax
import jax.numpy as jnp
from jax import lax
from jax.experimental import pallas as pl
from jax.experimental.pallas import tpu as pltpu

F32 = jnp.float32
BF16 = jnp.bfloat16
I32 = jnp.int32

N_HEADS = 8
HEAD_DIM = 64
N_KV_HEADS = 2
GROUP = N_HEADS // N_KV_HEADS
ROT_DIM = HEAD_DIM // 4
ROPE_THETA = 500000.0
CMP_BLOCK = 32
CMP_STRIDE = 16
CMP_HIDDEN = 64
SEL_BLOCK = 64
N_SEL = 16
SEL_FORCE = 1.0e6
WINDOW = 512
Q_BLOCK = 128
RNN_BLOCKS = 8
CONV_W = 4
RG_C = 8.0
TOP_K = 6
ROUTED_SCALE = 2.5
LN_EPS = 1e-5
PAGE_SIZE = 128

Q_WIDTH = N_HEADS * HEAD_DIM
KV_WIDTH = 2 * N_KV_HEADS * HEAD_DIM
N_GROUPS = 2 * N_KV_HEADS
CHUNK_W = CMP_STRIDE * KV_WIDTH
LANES = 128
NEG = -1.0e30
VMEM_LIMIT = 56 * 1024 * 1024

EXPERT_ROWS = 512
SEL_TK = 1024


def _cparams(sem):
    return pltpu.CompilerParams(dimension_semantics=sem, vmem_limit_bytes=VMEM_LIMIT)


def _full(shape):
    n = len(shape)
    return pl.BlockSpec(shape, lambda *a: (0,) * n)


def _dot(a, b):
    return jnp.dot(a, b, preferred_element_type=F32)


def _dot_nt(a, b):
    return lax.dot_general(a, b, (((1,), (1,)), ((), ())), preferred_element_type=F32)


def _gelu(x):
    return 0.5 * x * (1.0 + jnp.tanh(np.sqrt(2.0 / np.pi) * (x + 0.044715 * (x * x * x))))


def _sigmoid(x):
    return 1.0 / (1.0 + jnp.exp(-x))


def _layer_norm(z, g, b):
    mu = jnp.mean(z, axis=-1, keepdims=True)
    d = z - mu
    var = jnp.mean(d * d, axis=-1, keepdims=True)
    return d * lax.rsqrt(var + LN_EPS) * g + b


def _rope_tables(pos):
    half = ROT_DIM // 2
    inv = ROPE_THETA ** (-(jnp.arange(half, dtype=F32) * (2.0 / ROT_DIM)))
    ang = pos.astype(F32)[:, None] * inv[None, :]
    cos, sin = jnp.cos(ang), jnp.sin(ang)
    m = pos.shape[0]
    one = jnp.ones((m, HEAD_DIM - ROT_DIM), F32)
    zero = jnp.zeros((m, HEAD_DIM - ROT_DIM), F32)
    zh = jnp.zeros((m, half), F32)
    c = jnp.concatenate([cos, cos, one], axis=1)
    s1 = jnp.concatenate([-sin, zh, zero], axis=1)
    s2 = jnp.concatenate([zh, sin, zero], axis=1)
    rep = LANES // HEAD_DIM
    return jnp.tile(c, (1, rep)), jnp.tile(s1, (1, rep)), jnp.tile(s2, (1, rep))


def _rope128(x, c, s1, s2):
    half = ROT_DIM // 2
    return x * c + pltpu.roll(x, LANES - half, 1) * s1 + pltpu.roll(x, half, 1) * s2


def _proj_kernel(x_ref, c_ref, s1_ref, s2_ref, wq_ref, wkv_ref, wg_ref, ww_ref, *out_refs, transposed):
    if transposed:
        (q_ref, qr_ref, kvc_ref, kvct_ref, kvst_ref, kvwt_ref, ksh_ref, vsh_ref, kwh_ref, vwh_ref,
         gate_ref, urnn_ref, ugate_ref, ga_ref, gb_ref) = out_refs
        kvs_ref = kvw_ref = None
    else:
        (q_ref, qr_ref, kvc_ref, kvs_ref, kvw_ref, ksh_ref, vsh_ref, kwh_ref, vwh_ref,
         gate_ref, urnn_ref, ugate_ref, ga_ref, gb_ref) = out_refs
        kvct_ref = kvst_ref = kvwt_ref = None
    xb = x_ref[...].astype(BF16)
    c, s1, s2 = c_ref[...], s1_ref[...], s2_ref[...]
    scale = HEAD_DIM ** -0.5
    q = _dot(xb, wq_ref[...]) * scale
    for j in range(Q_WIDTH // LANES):
        ch = q[:, LANES * j:LANES * (j + 1)]
        rot = _rope128(ch, c, s1, s2)
        for hh in range(LANES // HEAD_DIM):
            head = j * (LANES // HEAD_DIM) + hh
            q_ref[head] = ch[:, HEAD_DIM * hh:HEAD_DIM * (hh + 1)].astype(BF16)
            qr_ref[head] = rot[:, HEAD_DIM * hh:HEAD_DIM * (hh + 1)].astype(BF16)
    kv = _dot(xb, wkv_ref[...])
    kvc_ref[...] = kv[:, :KV_WIDTH]
    if transposed:
        kvct_ref[0] = kv[:, :KV_WIDTH].T
    for base, full_ref, t_ref, kh_ref, vh_ref in ((KV_WIDTH, kvs_ref, kvst_ref, ksh_ref, vsh_ref),
                                                  (2 * KV_WIDTH, kvw_ref, kvwt_ref, kwh_ref, vwh_ref)):
        keys = _rope128(kv[:, base:base + LANES], c, s1, s2)
        vals = kv[:, base + LANES:base + 2 * LANES]
        if transposed:
            t_ref[0, :LANES, :] = keys.T
            t_ref[0, LANES:, :] = vals.T
        else:
            full_ref[:, :LANES] = keys
            full_ref[:, LANES:] = vals
        for h in range(N_KV_HEADS):
            kh_ref[h] = keys[:, HEAD_DIM * h:HEAD_DIM * (h + 1)].astype(BF16)
            vh_ref[h] = vals[:, HEAD_DIM * h:HEAD_DIM * (h + 1)].astype(BF16)
    gate_ref[...] = _sigmoid(_dot(xb, wg_ref[...]))
    d = urnn_ref.shape[-1]
    for k, ref in enumerate((urnn_ref, ugate_ref, ga_ref, gb_ref)):
        ref[...] = _dot(xb, ww_ref[:, d * k:d * (k + 1)])


def _split_w_in(w_in):
    d = w_in.shape[0]
    o = 0
    wq = w_in[:, o:o + Q_WIDTH]; o += Q_WIDTH
    wkv = w_in[:, o:o + 3 * KV_WIDTH]; o += 3 * KV_WIDTH
    wg = w_in[:, o:o + 3 * N_HEADS]; o += 3 * N_HEADS
    ww = w_in[:, o:]
    per = 3 * GROUP
    wg2 = jnp.zeros((d, N_KV_HEADS * LANES), w_in.dtype)
    for h in range(N_KV_HEADS):
        wg2 = wg2.at[:, h * LANES:h * LANES + per].set(wg[:, h * per:(h + 1) * per])
    return wq.astype(BF16), wkv.astype(BF16), wg2.astype(BF16), ww.astype(BF16)


def _proj(x2d, pos, wparts, tm, seq_len=None):
    m, d = x2d.shape
    wq, wkv, wg, ww = wparts
    c, s1, s2 = _rope_tables(pos)
    dw = ww.shape[1] // 4
    row = lambda w: pl.BlockSpec((tm, w), lambda i: (i, 0))
    hm = lambda n: pl.BlockSpec((n, tm, HEAD_DIM), lambda i: (0, i, 0))
    rows_f32 = jax.ShapeDtypeStruct((m, KV_WIDTH), F32)
    hm_q = jax.ShapeDtypeStruct((N_HEADS, m, HEAD_DIM), BF16)
    hm_kv = jax.ShapeDtypeStruct((N_KV_HEADS, m, HEAD_DIM), BF16)
    wide = jax.ShapeDtypeStruct((m, dw), F32)
    if seq_len is None:
        kv_shapes = (rows_f32, rows_f32, rows_f32)
        kv_specs = (row(KV_WIDTH),) * 3
    else:
        nt = seq_len // tm
        kv_t = jax.ShapeDtypeStruct((m // seq_len, KV_WIDTH, seq_len), F32)
        t_spec = pl.BlockSpec((1, KV_WIDTH, tm), lambda i: (i // nt, 0, i % nt))
        kv_shapes = (rows_f32, kv_t, kv_t, kv_t)
        kv_specs = (row(KV_WIDTH), t_spec, t_spec, t_spec)
    out_shape = (hm_q, hm_q) + kv_shapes + (hm_kv,) * 4 + (
        jax.ShapeDtypeStruct((m, N_KV_HEADS * LANES), F32), wide, wide, wide, wide)
    out_specs = (hm(N_HEADS), hm(N_HEADS)) + kv_specs + (hm(N_KV_HEADS),) * 4 + (
        row(N_KV_HEADS * LANES), row(dw), row(dw), row(dw), row(dw))
    return pl.pallas_call(
        functools.partial(_proj_kernel, transposed=seq_len is not None),
        out_shape=out_shape,
        grid=(m // tm,),
        in_specs=[row(d), row(LANES), row(LANES), row(LANES),
                  _full(wq.shape), _full(wkv.shape), _full(wg.shape), _full(ww.shape)],
        out_specs=out_specs,
        compiler_params=_cparams(("parallel",)),
        name="proj",
    )(x2d, c, s1, s2, wq, wkv, wg, ww)


def _compress_weights(p):
    parts = CMP_BLOCK // CMP_STRIDE
    gw = N_GROUPS * CMP_HIDDEN
    w1 = jnp.zeros((CMP_STRIDE, N_GROUPS, HEAD_DIM, parts, N_GROUPS, CMP_HIDDEN), F32)
    pe = jnp.zeros((8, CMP_STRIDE, N_GROUPS, HEAD_DIM), F32)
    w2 = jnp.zeros((N_GROUPS, CMP_HIDDEN, N_GROUPS, HEAD_DIM), F32)
    for g in range(N_GROUPS):
        kv = g // N_KV_HEADS
        w1_src = (p['w_cmp_k1'], p['w_cmp_v1'])[kv].reshape(parts, CMP_STRIDE, HEAD_DIM, CMP_HIDDEN)
        pe_src = (p['cmp_pos_k'], p['cmp_pos_v'])[kv].reshape(parts, CMP_STRIDE, HEAD_DIM)
        w2_src = (p['w_cmp_k2'], p['w_cmp_v2'])[kv]
        for m in range(parts):
            w1 = w1.at[:, g, :, m, g, :].set(w1_src[m])
            pe = pe.at[m, :, g, :].set(pe_src[m])
        w2 = w2.at[g, :, g, :].set(w2_src)
    return (w1.reshape(CHUNK_W, parts * gw).astype(BF16), pe.reshape(8, CHUNK_W).astype(BF16),
            w2.reshape(gw, N_GROUPS * HEAD_DIM).astype(BF16))


def _compress_rows(c, pec, w2):
    gw = N_GROUPS * CMP_HIDDEN
    n = c.shape[0]
    const = pec[0:1, :gw] + pec[1:2, gw:]
    hid = c[:, :gw] + pltpu.roll(c[:, gw:], n - 1, 0) + const
    return _dot(_gelu(hid).astype(BF16), w2)


def _compress_prompt_kernel(ch_ref, w1_ref, pe_ref, w2_ref, kc_ref, vc_ref):
    c = _dot(ch_ref[0].astype(BF16), w1_ref[...])
    pec = _dot(pe_ref[...], w1_ref[...])
    out = _compress_rows(c, pec, w2_ref[...])
    for h in range(N_KV_HEADS):
        kc_ref[0, h] = out[:, HEAD_DIM * h:HEAD_DIM * (h + 1)].astype(BF16)
        vc_ref[0, h] = out[:, HEAD_DIM * (N_KV_HEADS + h):HEAD_DIM * (N_KV_HEADS + h + 1)].astype(BF16)


def _compress_prompt(kvc, b, t, cw):
    w1, pe, w2 = cw
    nch = t // CMP_STRIDE
    chunks = kvc.reshape(b, nch, CHUNK_W)
    out = jax.ShapeDtypeStruct((b, N_KV_HEADS, nch, HEAD_DIM), BF16)
    ospec = pl.BlockSpec((1, N_KV_HEADS, nch, HEAD_DIM), lambda i: (i, 0, 0, 0))
    return pl.pallas_call(
        _compress_prompt_kernel,
        out_shape=(out, out),
        grid=(b,),
        in_specs=[pl.BlockSpec((1, nch, CHUNK_W), lambda i: (i, 0, 0)),
                  _full(w1.shape), _full(pe.shape), _full(w2.shape)],
        out_specs=(ospec, ospec),
        compiler_params=_cparams(("parallel",)),
        name="compress_prompt",
    )(chunks, w1, pe, w2)


def _sel_map(n_cmp_pad, n_cmp, n_sel, n_sel_pad):
    ratio = SEL_BLOCK // CMP_STRIDE
    i = np.arange(n_cmp_pad)[:, None]
    j = np.arange(n_sel_pad)[None, :]
    d = i - ratio * j
    m = sum(((d + n >= 0) & (d + n < ratio)).astype(np.float32) for n in range(CMP_BLOCK // CMP_STRIDE))
    m = m * (i < n_cmp) * (j < n_sel)
    return jnp.asarray(m, BF16)


def _masked_softmax_rows(s, valid):
    s = jnp.where(valid, s, -jnp.inf)
    m = jnp.max(s, axis=-1, keepdims=True)
    m = jnp.where(m == -jnp.inf, 0.0, m)
    e = jnp.exp(s - m)
    return e / jnp.maximum(jnp.sum(e, axis=-1, keepdims=True), 1e-30)


def _top_rows(score, n):
    rows = score.shape[0]
    row = lax.broadcasted_iota(I32, score.shape, 0).astype(F32)
    sel = jnp.zeros(score.shape, F32)
    picked = []
    for _ in range(n):
        m = jnp.max(score, axis=0, keepdims=True)
        idx = jnp.min(jnp.where(score == m, row, float(rows)), axis=0, keepdims=True)
        pick = row == idx
        picked.append(idx)
        sel = jnp.where(pick, 1.0, sel)
        score = jnp.where(pick, -jnp.inf, score)
    return picked, sel


def _nsa_prompt_kernel(q_ref, qr_ref, kc_ref, vc_ref, ks_ref, vs_ref, kw_ref, vw_ref, g_ref, smap_ref,
                       o_ref, *, n_sel_blocks):
    qb = pl.program_id(2)
    s0 = qb * Q_BLOCK
    rows = GROUP * Q_BLOCK
    q = q_ref[...].reshape(rows, HEAD_DIM)
    qr = qr_ref[...].reshape(rows, HEAD_DIM)
    pos1 = s0 + lax.broadcasted_iota(I32, (Q_BLOCK, 1), 0)
    pos = s0 + (lax.broadcasted_iota(I32, (rows, 1), 0) & (Q_BLOCK - 1))

    kc = kc_ref[0, 0]
    ncp = kc.shape[0]
    c_end = lax.broadcasted_iota(I32, (1, ncp), 1) * CMP_STRIDE + (CMP_BLOCK - 1)
    p_c = _masked_softmax_rows(_dot_nt(q, kc), c_end <= pos).astype(BF16)
    o_c = _dot(p_c, vc_ref[0, 0])
    imp_t = _dot_nt(smap_ref[...], p_c[0:Q_BLOCK])
    for g in range(1, GROUP):
        imp_t = imp_t + _dot_nt(smap_ref[...], p_c[g * Q_BLOCK:(g + 1) * Q_BLOCK])

    wlen = WINDOW + Q_BLOCK
    wstart = pl.multiple_of(jnp.maximum(s0 - WINDOW, 0), Q_BLOCK)
    kpos = wstart + lax.broadcasted_iota(I32, (1, wlen), 1)
    diff = pos - kpos
    p_w = _masked_softmax_rows(_dot_nt(qr, kw_ref[0, pl.ds(wstart, wlen), :]), (diff >= 0) & (diff <= WINDOW))
    o_w = _dot(p_w.astype(BF16), vw_ref[0, pl.ds(wstart, wlen), :])

    blk_t = lax.broadcasted_iota(I32, (LANES, Q_BLOCK), 0)
    cur_t = (s0 + lax.broadcasted_iota(I32, (1, Q_BLOCK), 1)) // SEL_BLOCK
    causal_t = blk_t <= cur_t
    forced_t = ((blk_t == 0) | (blk_t == cur_t) | (blk_t == cur_t - 1)) & causal_t
    n_forced = 3
    _, ranked_t = _top_rows(jnp.where(causal_t & jnp.logical_not(forced_t), imp_t, -jnp.inf),
                            min(N_SEL, n_sel_blocks) - n_forced)
    sel_t = jnp.where(forced_t | (causal_t & (ranked_t > 0.5)), 1.0, 0.0)
    sel = sel_t.T.astype(BF16)

    per_tile = SEL_TK // SEL_BLOCK
    jrow = lax.broadcasted_iota(I32, (LANES, SEL_TK), 0)
    cblk = lax.broadcasted_iota(I32, (LANES, SEL_TK), 1) // SEL_BLOCK
    kcol = lax.broadcasted_iota(I32, (1, SEL_TK), 1)

    def sel_tile(kt, carry, causal):
        m_i, l_i, acc = carry
        start = pl.multiple_of(kt * SEL_TK, SEL_TK)
        s = _dot_nt(qr, ks_ref[0, pl.ds(start, SEL_TK), :]).reshape(GROUP, Q_BLOCK, SEL_TK)
        expand = jnp.where(jrow - kt * per_tile == cblk, 1.0, 0.0).astype(BF16)
        ok = _dot(sel, expand) > 0.5
        if causal:
            ok = ok & (start + kcol <= pos1)
        s = jnp.where(ok[None], s, NEG)
        m_new = jnp.maximum(m_i, jnp.max(s, axis=-1, keepdims=True))
        alpha = jnp.exp(m_i - m_new)
        p = jnp.exp(s - m_new)
        l_new = alpha * l_i + jnp.sum(p, axis=-1, keepdims=True)
        pv = _dot(p.reshape(rows, SEL_TK).astype(BF16), vs_ref[0, pl.ds(start, SEL_TK), :])
        return m_new, l_new, alpha * acc + pv.reshape(GROUP, Q_BLOCK, HEAD_DIM)

    init = (jnp.full((GROUP, Q_BLOCK, 1), NEG, F32), jnp.zeros((GROUP, Q_BLOCK, 1), F32),
            jnp.zeros((GROUP, Q_BLOCK, HEAD_DIM), F32))
    n_full = s0 // SEL_TK
    carry = lax.fori_loop(0, n_full, lambda kt, cr: sel_tile(kt, cr, False), init)
    _, l_s, acc_s = sel_tile(n_full, carry, True)
    o_s = (acc_s / l_s).reshape(rows, HEAD_DIM)

    gates = g_ref[...]
    outs = []
    for g in range(GROUP):
        r = slice(g * Q_BLOCK, (g + 1) * Q_BLOCK)
        outs.append(gates[:, 3 * g:3 * g + 1] * o_c[r] + gates[:, 3 * g + 1:3 * g + 2] * o_s[r]
                    + gates[:, 3 * g + 2:3 * g + 3] * o_w[r])
    o_ref[...] = jnp.concatenate(outs, axis=1).astype(o_ref.dtype)


def _nsa_prompt(q_hm, qr_hm, kc, vc, ksh, vsh, kwh, vwh, gates, b, t):
    nqb = t // Q_BLOCK
    n_sel_blocks = t // SEL_BLOCK
    ncp = kc.shape[2]
    assert N_SEL <= n_sel_blocks <= LANES and t % SEL_TK == 0 and t >= WINDOW + Q_BLOCK
    smap = _sel_map(ncp, ncp - 1, n_sel_blocks, LANES).T
    qspec = pl.BlockSpec((GROUP, Q_BLOCK, HEAD_DIM), lambda bi, h, i: (h, bi * nqb + i, 0))
    cspec = pl.BlockSpec((1, 1, ncp, HEAD_DIM), lambda bi, h, i: (bi, h, 0, 0))
    kspec = pl.BlockSpec((1, t, HEAD_DIM), lambda bi, h, i: (h, bi, 0))
    return pl.pallas_call(
        functools.partial(_nsa_prompt_kernel, n_sel_blocks=n_sel_blocks),
        out_shape=jax.ShapeDtypeStruct((b * t, Q_WIDTH), BF16),
        grid=(b, N_KV_HEADS, nqb),
        in_specs=[qspec, qspec, cspec, cspec, kspec, kspec, kspec, kspec,
                  pl.BlockSpec((Q_BLOCK, LANES), lambda bi, h, i: (bi * nqb + i, h)),
                  _full(smap.shape)],
        out_specs=pl.BlockSpec((Q_BLOCK, GROUP * HEAD_DIM), lambda bi, h, i: (bi * nqb + i, h)),
        compiler_params=_cparams(("parallel", "parallel", "arbitrary")),
        name="nsa_prompt",
    )(q_hm, qr_hm, kc, vc, ksh, vsh, kwh, vwh, gates, smap)


def _expm1(x):
    series = x * (1.0 + x * (1.0 / 2 + x * (1.0 / 6 + x * (1.0 / 24 + x * (1.0 / 120 + x * (1.0 / 720))))))
    return jnp.where(jnp.abs(x) < 0.1, series, jnp.exp(x) - 1.0)


def _log1p(z):
    series = z * (1.0 - z * (1.0 / 2 - z * (1.0 / 3 - z * (1.0 / 4 - z * (1.0 / 5 - z * (1.0 / 6 - z * (
        1.0 / 7 - z * (1.0 / 8))))))))
    return jnp.where(z < 0.1, series, jnp.log(1.0 + z))


def _rglru_gates(xc, ug, wa_ref, ba, wi_ref, bi, lam):
    xcb = xc.astype(BF16)
    bw = xc.shape[-1] // RNN_BLOCKS
    ra = jnp.concatenate([_dot(xcb[:, bw * n:bw * (n + 1)], wa_ref[n]) for n in range(RNN_BLOCKS)], axis=1)
    ri = jnp.concatenate([_dot(xcb[:, bw * n:bw * (n + 1)], wi_ref[n]) for n in range(RNN_BLOCKS)], axis=1)
    r = _sigmoid(ra + ba)
    i = _sigmoid(ri + bi)
    softplus = jnp.maximum(-lam, 0.0) + _log1p(jnp.exp(-jnp.abs(lam)))
    log_a = -RG_C * r * softplus
    a = jnp.exp(log_a)
    b = jnp.sqrt(-_expm1(2.0 * log_a)) * (i * xc)
    return a, b, _gelu(ug)


def _rglru_prompt_kernel(u_ref, ug_ref, cw_ref, cb_ref, wa_ref, ba_ref, wi_ref, bi_ref, lam_ref,
                         y_ref, h_ref, up_s, a_s, b_s, h_s):
    tt = u_ref.shape[1]
    t = pl.program_id(1)
    halo = CONV_W - 1

    @pl.when(t == 0)
    def _():
        up_s[0:8, :] = jnp.zeros((8, up_s.shape[1]), F32)
        h_s[...] = jnp.zeros(h_s.shape, F32)

    @pl.when(t > 0)
    def _():
        up_s[8 - halo:8, :] = up_s[8 + tt - halo:8 + tt, :]

    up_s[8:8 + tt, :] = u_ref[0]
    taps = up_s[pl.ds(8 - halo, tt), :] * cw_ref[0:1, :]
    for k in range(1, CONV_W):
        taps = taps + up_s[pl.ds(8 - halo + k, tt), :] * cw_ref[k:k + 1, :]
    xc = cb_ref[...] + taps
    a, b, gate = _rglru_gates(xc, ug_ref[0], wa_ref, ba_ref[...], wi_ref, bi_ref[...], lam_ref[...])
    a_s[...] = a
    b_s[...] = b

    def step(i, h):
        h = a_s[pl.ds(i, 1), :] * h + b_s[pl.ds(i, 1), :]
        b_s[pl.ds(i, 1), :] = h
        return h

    h = lax.fori_loop(0, tt, step, h_s[...], unroll=8)
    h_s[...] = h
    h_ref[0] = h
    y_ref[0] = (b_s[...] * gate).astype(y_ref.dtype)


def _rglru_weights(p):
    row = lambda v: v.reshape(1, -1).astype(F32)
    return (p['conv_w'].astype(F32), row(p['conv_b']), p['w_rg_a'].astype(BF16), row(p['b_rg_a']),
            p['w_rg_i'].astype(BF16), row(p['b_rg_i']), row(p['lru_lambda']))


def _rglru_prompt(u, ug, rw, b, t, tt):
    d = u.shape[1]
    nt = t // tt
    seq = pl.BlockSpec((1, tt, d), lambda bi, i: (bi, i, 0))
    return pl.pallas_call(
        _rglru_prompt_kernel,
        out_shape=(jax.ShapeDtypeStruct((b, t, d), BF16), jax.ShapeDtypeStruct((b, 1, d), F32)),
        grid=(b, nt),
        in_specs=[seq, seq] + [_full(w.shape) for w in rw],
        out_specs=(seq, pl.BlockSpec((1, 1, d), lambda bi, i: (bi, 0, 0))),
        scratch_shapes=[pltpu.VMEM((tt + 8, d), F32), pltpu.VMEM((tt, d), F32), pltpu.VMEM((tt, d), F32),
                        pltpu.VMEM((1, d), F32)],
        compiler_params=_cparams(("parallel", "arbitrary")),
        name="rglru_prompt",
    )(u.reshape(b, t, d), ug.reshape(b, t, d), *rw)


def _rglru_step_kernel(u_ref, ug_ref, cs_ref, h0_ref, cw_ref, cb_ref, wa_ref, ba_ref, wi_ref, bi_ref, lam_ref,
                       y_ref, h_ref):
    taps = cs_ref[0] * cw_ref[0:1, :]
    for k in range(1, CONV_W - 1):
        taps = taps + cs_ref[k] * cw_ref[k:k + 1, :]
    xc = cb_ref[...] + (taps + u_ref[...] * cw_ref[CONV_W - 1:CONV_W, :])
    a, b, gate = _rglru_gates(xc, ug_ref[...], wa_ref, ba_ref[...], wi_ref, bi_ref[...], lam_ref[...])
    h = a * h0_ref[...] + b
    h_ref[...] = h
    y_ref[...] = (h * gate).astype(y_ref.dtype)


def _rglru_step(u, ug, conv_state, h0, rw):
    n, d = u.shape
    args = (u, ug, conv_state, h0) + tuple(rw)
    return pl.pallas_call(
        _rglru_step_kernel,
        out_shape=(jax.ShapeDtypeStruct((n, d), BF16), jax.ShapeDtypeStruct((n, d), F32)),
        grid=(1,),
        in_specs=[_full(a.shape) for a in args],
        out_specs=(_full((n, d)), _full((n, d))),
        compiler_params=_cparams(("arbitrary",)),
        name="rglru_step",
    )(*args)


_START, _WAIT = "start", "wait"


def _row_gather(rows, go, copy_fn, unroll=4):
    assert rows % 2 == 0

    def body(i, c):
        for lane in range(2):
            copy = copy_fn(2 * i + lane)
            if go == _START:
                copy.start(priority=lane)
            else:
                copy.wait()
        return c
    lax.fori_loop(0, rows // 2, body, 0, unroll=unroll)


def _nsa_sample_cmp_kernel(pt_ref, pool_ref, q_ref, w1_ref, pe_ref, w2_ref, smap_ref,
                           oc_ref, idx_ref, pbuf, rows_s, c_s, sem, *, n_seq, n_pages, q_pos, n_sel_blocks):
    b = pl.program_id(0)
    nb = n_seq
    n_slots, pg = pbuf.shape[0], pbuf.shape[1]
    groups = n_pages // pg
    ahead = n_slots - 1
    rows_pp = PAGE_SIZE // CMP_STRIDE

    def page_copy(seq, grp, slot, j):
        return pltpu.make_async_copy(pool_ref.at[pt_ref[seq * n_pages + grp * pg + j]], pbuf.at[slot, j],
                                     sem.at[slot])

    def issue(seq, grp, slot):
        _row_gather(pg, _START, lambda j: page_copy(seq, grp, slot, j))

    @pl.when(b == 0)
    def _():
        for g in range(ahead):
            if g // groups < n_seq:
                issue(g // groups, g % groups, g % n_slots)

    for k in range(groups):
        gi = b * groups + k
        slot = lax.rem(gi, n_slots)
        nxt = k + ahead
        seq_off, grp_nxt = nxt // groups, nxt % groups

        @pl.when(b + seq_off < nb)
        def _():
            issue(b + seq_off, grp_nxt, lax.rem(gi + ahead, n_slots))

        _row_gather(pg, _WAIT, lambda j: page_copy(b, k, slot, j))
        halves = KV_WIDTH // LANES
        for j in range(pg):
            for f in range(halves):
                rows_s[f, j * PAGE_SIZE:(j + 1) * PAGE_SIZE, :] = pbuf[slot, j, f * LANES:(f + 1) * LANES, :].T

        def chunk_rows(s):
            return jnp.concatenate([rows_s[f, pl.ds(s, pg * rows_pp, stride=CMP_STRIDE), :]
                                    for f in range(halves)], axis=1).astype(BF16)

        acc = _dot(chunk_rows(0), w1_ref[0])
        for s in range(1, CMP_STRIDE):
            acc = acc + _dot(chunk_rows(s), w1_ref[s])
        c_s[k * pg * rows_pp:(k + 1) * pg * rows_pp, :] = acc

    nch = n_pages * rows_pp
    pec = _dot(pe_ref[...], w1_ref[...].reshape(CHUNK_W, w1_ref.shape[2]))
    kv = _compress_rows(c_s[...], pec, w2_ref[...]).astype(BF16)

    q = q_ref[0]
    c_end = lax.broadcasted_iota(I32, (1, nch), 1) * CMP_STRIDE + (CMP_BLOCK - 1)
    row = lax.broadcasted_iota(I32, (N_HEADS, 1), 0)
    width = smap_ref.shape[1]
    blk_i = lax.broadcasted_iota(I32, (1, width), 1)
    cur = q_pos // SEL_BLOCK
    forced = (blk_i == 0) | (blk_i == cur) | (blk_i == cur - 1)
    o_c = jnp.zeros((N_HEADS, HEAD_DIM), F32)
    score = jnp.full((N_HEADS, width), -jnp.inf, F32)
    for h in range(N_KV_HEADS):
        kc = kv[:, HEAD_DIM * h:HEAD_DIM * (h + 1)]
        vc = kv[:, HEAD_DIM * (N_KV_HEADS + h):HEAD_DIM * (N_KV_HEADS + h + 1)]
        p = _masked_softmax_rows(_dot_nt(q, kc), c_end <= q_pos).astype(BF16)
        in_group = (row // GROUP) == h
        o_c = jnp.where(in_group, _dot(p, vc), o_c)
        imp = jnp.sum(jnp.where(in_group, _dot(p, smap_ref[...]), 0.0), axis=0, keepdims=True)
        sc = jnp.where(blk_i > cur, -jnp.inf, jnp.where(forced, SEL_FORCE, imp))
        score = jnp.where(row == h, sc, score)
    oc_ref[0] = o_c

    n = min(N_SEL, n_sel_blocks)
    lane = lax.broadcasted_iota(I32, (1, LANES), 1)
    idx_out = jnp.zeros((N_HEADS, LANES), I32)
    blk = blk_i.astype(F32)
    for r in range(n):
        m = jnp.max(score, axis=-1, keepdims=True)
        idx = jnp.min(jnp.where(score == m, blk, float(width)), axis=-1, keepdims=True)
        score = jnp.where(blk == idx, -jnp.inf, score)
        idx_out = jnp.where(lane == r, idx.astype(I32), idx_out)
    idx_ref[0] = idx_out


CMP_PAGE_GROUP = 32
CMP_GROUP_SLOTS = 3


def _pages_feature_major(pool):
    n_pool, rows = pool.shape[0], pool.shape[1]
    return jnp.transpose(pool, (0, 2, 3, 4, 1)).reshape(n_pool, KV_WIDTH, rows)


def _nsa_sample_cmp(page_table, pool_t, q_seq, cw, past_len):
    bd, n_pages = page_table.shape
    w1, pe, w2 = cw
    w1 = w1.reshape(CMP_STRIDE, KV_WIDTH, w1.shape[1])
    rows_pp = PAGE_SIZE // CMP_STRIDE
    nch = n_pages * rows_pp
    pg = min(CMP_PAGE_GROUP, n_pages)
    assert n_pages % pg == 0
    n_sel_blocks = past_len // SEL_BLOCK + 1
    width = -(-n_sel_blocks // LANES) * LANES
    smap = _sel_map(nch, nch - 1, n_sel_blocks, width)
    grid_spec = pltpu.PrefetchScalarGridSpec(
        num_scalar_prefetch=1,
        grid=(bd,),
        in_specs=[pl.BlockSpec(memory_space=pl.ANY),
                  pl.BlockSpec((1, N_HEADS, HEAD_DIM), lambda i, pt: (i, 0, 0)),
                  pl.BlockSpec(w1.shape, lambda i, pt: (0, 0, 0)),
                  pl.BlockSpec(pe.shape, lambda i, pt: (0, 0)),
                  pl.BlockSpec(w2.shape, lambda i, pt: (0, 0)),
                  pl.BlockSpec(smap.shape, lambda i, pt: (0, 0))],
        out_specs=(pl.BlockSpec((1, N_HEADS, HEAD_DIM), lambda i, pt: (i, 0, 0)),
                   pl.BlockSpec((1, N_HEADS, LANES), lambda i, pt: (i, 0, 0))),
        scratch_shapes=[pltpu.VMEM((CMP_GROUP_SLOTS, pg, KV_WIDTH, PAGE_SIZE), F32),
                        pltpu.VMEM((KV_WIDTH // LANES, pg * PAGE_SIZE, LANES), F32),
                        pltpu.VMEM((nch, w1.shape[2]), F32),
                        pltpu.SemaphoreType.DMA((CMP_GROUP_SLOTS,))],
    )
    return pl.pallas_call(
        functools.partial(_nsa_sample_cmp_kernel, n_seq=bd, n_pages=n_pages, q_pos=past_len,
                          n_sel_blocks=n_sel_blocks),
        out_shape=(jax.ShapeDtypeStruct((bd, N_HEADS, HEAD_DIM), F32),
                   jax.ShapeDtypeStruct((bd, N_HEADS, LANES), I32)),
        grid_spec=grid_spec,
        compiler_params=_cparams(("arbitrary",)),
        name="nsa_sample_cmp",
    )(page_table.reshape(-1), pool_t, q_seq, w1, pe, w2, smap)


def _nsa_sample_sel_kernel(pt_ref, idx_s_ref, pool_ref, qr_ref, idx_ref, kvs_ref, kvw_ref, win_ref, oc_ref, g_ref,
                           o_ref, wout_ref, sbuf, sem, *, n_seq, n_pages, n_sel, ns_past):
    b = pl.program_id(0)
    per_page = PAGE_SIZE // SEL_BLOCK
    nblk = N_KV_HEADS * n_sel
    nkeys = n_sel * PAGE_SIZE

    def slab_copy(seq, slot, j):
        blk, kv = j // 2, j % 2
        jp = jnp.minimum(idx_s_ref[seq * nblk + blk], ns_past - 1)
        page = pt_ref[seq * n_pages + jp // per_page]
        feat = pl.multiple_of((kv * N_KV_HEADS + blk // n_sel) * HEAD_DIM, HEAD_DIM)
        return pltpu.make_async_copy(pool_ref.at[page, pl.ds(feat, HEAD_DIM), :], sbuf.at[slot, blk, kv],
                                     sem.at[slot])

    @pl.when(b == 0)
    def _():
        _row_gather(2 * nblk, _START, lambda j: slab_copy(0, 0, j))

    @pl.when(b + 1 < n_seq)
    def _():
        _row_gather(2 * nblk, _START, lambda j: slab_copy(b + 1, (b + 1) % 2, j))

    slot = b % 2
    _row_gather(2 * nblk, _WAIT, lambda j: slab_copy(b, slot, j))

    qr = qr_ref[0]
    qf = qr.astype(F32)
    row = lax.broadcasted_iota(I32, (N_HEADS, 1), 0)
    key = lax.broadcasted_iota(I32, (1, nkeys), 1)
    key_blk = ((key % PAGE_SIZE) // SEL_BLOCK).astype(F32)
    first = (key % SEL_BLOCK) == 0
    expand = jnp.where(lax.broadcasted_iota(I32, (LANES, nkeys), 0)
                       == lax.broadcasted_iota(I32, (LANES, nkeys), 1) // PAGE_SIZE, 1.0, 0.0).astype(BF16)
    chosen = idx_ref[0]
    in_page = (jnp.minimum(chosen, ns_past - 1) % per_page).astype(F32).astype(BF16)
    want_blk = _dot(in_page, expand)
    is_new = _dot(jnp.where(chosen >= ns_past, 1.0, 0.0).astype(BF16), expand) > 0.5
    kvs_new = kvs_ref[0].astype(BF16).astype(F32)
    kvw_new = kvw_ref[0].astype(BF16).astype(F32)
    o_s = jnp.zeros((N_HEADS, HEAD_DIM), F32)
    o_w = jnp.zeros((N_HEADS, HEAD_DIM), F32)
    for h in range(N_KV_HEADS):
        in_group = (row // GROUP) == h
        ksl = slice(HEAD_DIM * h, HEAD_DIM * (h + 1))
        vsl = slice(HEAD_DIM * (N_KV_HEADS + h), HEAD_DIM * (N_KV_HEADS + h + 1))
        k_t = jnp.concatenate([sbuf[slot, h * n_sel + n, 0] for n in range(n_sel)], axis=1).astype(BF16)
        v_t = jnp.concatenate([sbuf[slot, h * n_sel + n, 1] for n in range(n_sel)], axis=1).astype(BF16)
        in_blk = want_blk[h:h + 1] == key_blk
        newblk = is_new[h:h + 1]
        newkey = in_blk & newblk & first
        s_new = jnp.sum(qf * kvs_new[:, ksl], axis=-1, keepdims=True)
        s = jnp.where(newkey, s_new, _dot(qr, k_t))
        p = _masked_softmax_rows(s, in_blk & (jnp.logical_not(newblk) | first))
        p_new = jnp.sum(jnp.where(newkey, p, 0.0), axis=-1, keepdims=True).astype(BF16).astype(F32)
        o_h = _dot_nt(jnp.where(newkey, 0.0, p).astype(BF16), v_t) + p_new * kvs_new[:, vsl]
        o_s = jnp.where(in_group, o_h, o_s)
        s_buf = _dot(qr, win_ref[0, ksl, :].astype(BF16))
        s_cur = jnp.sum(qf * kvw_new[:, ksl], axis=-1, keepdims=True)
        m = jnp.maximum(jnp.max(s_buf, axis=-1, keepdims=True), s_cur)
        e_buf = jnp.exp(s_buf - m)
        e_cur = jnp.exp(s_cur - m)
        den = jnp.maximum(jnp.sum(e_buf, axis=-1, keepdims=True) + e_cur, 1e-30)
        o_h = (_dot_nt((e_buf / den).astype(BF16), win_ref[0, vsl, :].astype(BF16))
               + (e_cur / den).astype(BF16).astype(F32) * kvw_new[:, vsl])
        o_w = jnp.where(in_group, o_h, o_w)

    gw = g_ref.shape[-1]
    gl = lax.broadcasted_iota(I32, (N_HEADS, gw), 1)
    gbase = (row // GROUP) * LANES + (row % GROUP) * 3
    gates = jnp.broadcast_to(g_ref[0], (N_HEADS, gw))
    gate = lambda j: jnp.sum(jnp.where(gl == gbase + j, gates, 0.0), axis=-1, keepdims=True)
    o_ref[0] = gate(0) * oc_ref[0] + gate(1) * o_s + gate(2) * o_w

    n_chunks = win_ref.shape[2] // LANES
    new_col = jnp.broadcast_to(kvw_ref[0], (LANES, KV_WIDTH)).T
    last_lane = lax.broadcasted_iota(I32, (1, LANES), 1) == LANES - 1
    for c in range(n_chunks):
        cur = win_ref[0, :, c * LANES:(c + 1) * LANES]
        nxt = win_ref[0, :, (c + 1) * LANES:(c + 2) * LANES] if c + 1 < n_chunks else new_col
        wout_ref[0, :, c * LANES:(c + 1) * LANES] = jnp.where(last_lane, pltpu.roll(nxt, LANES - 1, 1),
                                                             pltpu.roll(cur, LANES - 1, 1))


def _nsa_sample_sel(page_table, idx, pool_t, qr_seq, kvs_new, kvw_new, win_t, o_c, gates, past_len):
    bd, n_pages = page_table.shape
    n_sel = min(N_SEL, past_len // SEL_BLOCK + 1)
    n_buf = win_t.shape[2]
    assert n_buf % LANES == 0
    gw = gates.shape[-1]
    seq3 = lambda s1, s2: pl.BlockSpec((1, s1, s2), lambda i, pt, ix: (i, 0, 0))
    grid_spec = pltpu.PrefetchScalarGridSpec(
        num_scalar_prefetch=2,
        grid=(bd,),
        in_specs=[pl.BlockSpec(memory_space=pl.ANY), seq3(N_HEADS, HEAD_DIM), seq3(N_HEADS, LANES),
                  seq3(1, KV_WIDTH), seq3(1, KV_WIDTH), seq3(KV_WIDTH, n_buf), seq3(N_HEADS, HEAD_DIM),
                  seq3(1, gw)],
        out_specs=(seq3(N_HEADS, HEAD_DIM), seq3(KV_WIDTH, n_buf)),
        scratch_shapes=[pltpu.VMEM((2, N_KV_HEADS * n_sel, 2, HEAD_DIM, PAGE_SIZE), F32),
                        pltpu.SemaphoreType.DMA((2,))],
    )
    idx_flat = idx[:, :N_KV_HEADS, :n_sel].reshape(-1)
    return pl.pallas_call(
        functools.partial(_nsa_sample_sel_kernel, n_seq=bd, n_pages=n_pages, n_sel=n_sel,
                          ns_past=past_len // SEL_BLOCK),
        out_shape=(jax.ShapeDtypeStruct((bd, N_HEADS, HEAD_DIM), F32),
                   jax.ShapeDtypeStruct((bd, KV_WIDTH, n_buf), F32)),
        grid_spec=grid_spec,
        compiler_params=_cparams(("arbitrary",)),
        name="nsa_sample_sel",
    )(page_table.reshape(-1), idx_flat, pool_t, qr_seq, idx, kvs_new.reshape(bd, 1, KV_WIDTH),
      kvw_new.reshape(bd, 1, KV_WIDTH), win_t, o_c, gates.reshape(bd, 1, gw))


def _tail_kernel(x_ref, o_ref, y_ref, ga_ref, gb_ref, wa_ref, wb_ref, wo_ref, g1_ref, b1_ref, wr_ref, rb_ref,
                 cin_ref, x1_ref, idx_ref, w_ref, pos_ref, cnt_ref, carry_s, *, alpha):
    i = pl.program_id(0)
    tm = x_ref.shape[0]
    ne = wr_ref.shape[0]

    @pl.when(i == 0)
    def _():
        carry_s[...] = cin_ref[...]

    merged = (_sigmoid(ga_ref[...]) * _dot(o_ref[...], wa_ref[...])
              + _sigmoid(gb_ref[...]) * _dot(y_ref[...], wb_ref[...]))
    mix = _dot(merged.astype(BF16), wo_ref[...])
    x1 = _layer_norm(alpha * x_ref[...] + mix, g1_ref[...], b1_ref[...])
    x1_ref[...] = x1

    scores = _sigmoid(_dot_nt(wr_ref[...], x1.astype(BF16)))
    picked, hits = _top_rows(scores + rb_ref[...], TOP_K)
    expert = lax.broadcasted_iota(I32, (ne, tm), 0).astype(F32)
    chosen = [jnp.sum(jnp.where(expert == e_k, scores, 0.0), axis=0, keepdims=True) for e_k in picked]
    total = chosen[0]
    for c in chosen[1:]:
        total = total + c

    earlier = lax.broadcasted_iota(I32, (tm, tm), 0) < lax.broadcasted_iota(I32, (tm, tm), 1)
    prefix = _dot(hits.astype(BF16), jnp.where(earlier, 1.0, 0.0).astype(BF16)) + carry_s[...]
    out_row = lax.broadcasted_iota(I32, (idx_ref.shape[0], 1), 0)
    idx_out = jnp.zeros(idx_ref.shape, I32)
    w_out = jnp.zeros(w_ref.shape, F32)
    pos_out = jnp.zeros(pos_ref.shape, I32)
    for k in range(TOP_K):
        p_k = jnp.sum(jnp.where(expert == picked[k], prefix, 0.0), axis=0, keepdims=True)
        idx_out = jnp.where(out_row == k, picked[k].astype(I32), idx_out)
        pos_out = jnp.where(out_row == k, p_k.astype(I32), pos_out)
        w_out = jnp.where(out_row == k, chosen[k] / total * ROUTED_SCALE, w_out)
    idx_ref[...] = idx_out
    w_ref[...] = w_out
    pos_ref[...] = pos_out
    carry_s[...] = carry_s[...] + jnp.sum(hits, axis=1, keepdims=True)
    cnt_ref[...] = carry_s[...]


def _tail_weights(p):
    row = lambda v: v.reshape(1, -1).astype(F32)
    return (p['w_branch_attn'].astype(BF16), p['w_branch_rnn'].astype(BF16), p['w_out'].astype(BF16),
            row(p['ln1_g']), row(p['ln1_b']), p['w_router'].T.astype(BF16),
            p['router_bias'].reshape(-1, 1).astype(F32))


def _tail(x2d, o_attn, y_rnn, g_a, g_b, tw, counts_in, alpha, tm):
    m, d = x2d.shape
    ne = tw[5].shape[0]
    row = lambda w: pl.BlockSpec((tm, w), lambda i: (i, 0))
    slots = pl.BlockSpec((8, tm), lambda i: (0, i))
    small = jax.ShapeDtypeStruct((8, m), I32)
    return pl.pallas_call(
        functools.partial(_tail_kernel, alpha=alpha),
        out_shape=(jax.ShapeDtypeStruct((m, d), F32), small, jax.ShapeDtypeStruct((8, m), F32), small,
                   jax.ShapeDtypeStruct((ne, 1), F32)),
        grid=(m // tm,),
        in_specs=[row(d), row(o_attn.shape[1]), row(y_rnn.shape[1]), row(d), row(d)]
                 + [_full(w.shape) for w in tw] + [_full((ne, 1))],
        out_specs=(row(d), slots, slots, slots, _full((ne, 1))),
        scratch_shapes=[pltpu.VMEM((ne, 1), F32)],
        compiler_params=_cparams(("arbitrary",)),
        name="tail",
    )(x2d, o_attn, y_rnn, g_a, g_b, *tw, counts_in)


def _silu(x):
    return x * _sigmoid(x)


def _pack_bf16_pairs(x):
    half = x.shape[1] // 2
    bits = lax.bitcast_convert_type(x.astype(BF16).astype(F32), jnp.uint32)
    return bits[:, :half] | (bits[:, half:] >> 16)


def _dispatch_kernel(dest_ref, xp_ref, xs_ref, out_ref, wbuf, zbuf, sem, *, n_tok, tail):
    i = pl.program_id(0)
    n = pl.num_programs(0)
    tm = xp_ref.shape[0]
    n_sample = xs_ref.shape[0]
    slot = i % 2

    def scatter(tile, rows, slot_, go):
        def row_copy(k, t):
            dst = dest_ref[k * n_tok + tile * tm + t]
            return pltpu.make_async_copy(wbuf.at[slot_, pl.ds(t, 1)], out_ref.at[pl.ds(dst, 1)], sem.at[slot_])
        for k in range(TOP_K):
            _row_gather(rows, go, functools.partial(row_copy, k))

    if tail is not None:
        tail_copy = pltpu.make_async_copy(zbuf, out_ref.at[pl.ds(tail[0], tail[1])], sem.at[2])

        @pl.when(i == 0)
        def _():
            zbuf[...] = jnp.zeros(zbuf.shape, zbuf.dtype)
            tail_copy.start()

    @pl.when(i >= 2)
    def _():
        scatter(i - 2, tm, slot, _WAIT)

    @pl.when(i < n - 1)
    def _():
        wbuf[slot] = _pack_bf16_pairs(xp_ref[...])
        scatter(i, tm, slot, _START)

    @pl.when(i == n - 1)
    def _():
        wbuf[slot, 0:n_sample, :] = _pack_bf16_pairs(xs_ref[...])
        scatter(i, n_sample, slot, _START)
        scatter(i, n_sample, slot, _WAIT)
        if tail is not None:
            tail_copy.wait()

    @pl.when((i == n - 1) & (i >= 1))
    def _():
        scatter(i - 1, tm, 1 - slot, _WAIT)


def _dispatch(dest, x1_p, x1_s, n_rows_pad, tm):
    m, d = x1_p.shape
    n_rows = dest.shape[0]
    n_tiles = m // tm
    assert n_rows == (m + x1_s.shape[0]) * TOP_K and x1_s.shape[0] <= tm
    tail = None if n_rows_pad == n_rows else (n_rows, n_rows_pad - n_rows)
    grid_spec = pltpu.PrefetchScalarGridSpec(
        num_scalar_prefetch=1,
        grid=(n_tiles + 1,),
        in_specs=[pl.BlockSpec((tm, d), lambda i, ds: (jnp.minimum(i, n_tiles - 1), 0)),
                  pl.BlockSpec(x1_s.shape, lambda i, ds: (0, 0))],
        out_specs=pl.BlockSpec(memory_space=pl.ANY),
        scratch_shapes=[pltpu.VMEM((2, tm, d // 2), jnp.uint32),
                        pltpu.VMEM((8 if tail is None else tail[1], d // 2), jnp.uint32),
                        pltpu.SemaphoreType.DMA((3,))],
    )
    return pl.pallas_call(
        functools.partial(_dispatch_kernel, n_tok=n_rows // TOP_K, tail=tail),
        out_shape=jax.ShapeDtypeStruct((n_rows_pad, d // 2), jnp.uint32),
        grid_spec=grid_spec,
        compiler_params=_cparams(("arbitrary",)),
        name="dispatch",
    )(dest, x1_p, x1_s)


def _experts_kernel(vb_ref, ve_ref, lo_ref, hi_ref, xs_ref, wg_ref, wu_ref, wd_ref, y_ref, wgb, wub, wdb):
    v = pl.program_id(0)
    lo, hi = lo_ref[v], hi_ref[v]
    prev = jnp.maximum(v - 1, 0)
    first_of_block = (v == 0) | (vb_ref[v] != vb_ref[prev])

    @pl.when((v == 0) | (ve_ref[v] != ve_ref[prev]))
    def _():
        wgb[...] = wg_ref[0].astype(BF16)
        wub[...] = wu_ref[0].astype(BF16)
        wdb[...] = wd_ref[0].astype(BF16)

    @pl.when(hi > lo)
    def _():
        words = xs_ref[...]
        x = jnp.concatenate(
            [lax.bitcast_convert_type(words & jnp.uint32(0xFFFF0000), F32).astype(BF16),
             lax.bitcast_convert_type(words << 16, F32).astype(BF16)], axis=1)
        hid = _silu(_dot(x, wgb[...])) * _dot(x, wub[...])
        val = _dot(hid.astype(BF16), wdb[...])
        row = lax.broadcasted_iota(I32, (val.shape[0], 1), 0)
        mine = (row >= lo) & (row < hi)

        @pl.when(first_of_block)
        def _():
            y_ref[...] = jnp.where(mine, val, 0.0)

        @pl.when(jnp.logical_not(first_of_block))
        def _():
            y_ref[...] = jnp.where(mine, val, y_ref[...])


def _experts(xs, visits, w_gate, w_up, w_down):
    vblk, vexp, vlo, vhi = visits
    d, de = w_gate.shape[1], w_gate.shape[2]
    rb = EXPERT_ROWS
    grid_spec = pltpu.PrefetchScalarGridSpec(
        num_scalar_prefetch=4,
        grid=(vblk.shape[0],),
        in_specs=[pl.BlockSpec((rb, d // 2), lambda v, vb, ve, lo, hi: (vb[v], 0)),
                  pl.BlockSpec((1, d, de), lambda v, vb, ve, lo, hi: (ve[v], 0, 0)),
                  pl.BlockSpec((1, d, de), lambda v, vb, ve, lo, hi: (ve[v], 0, 0)),
                  pl.BlockSpec((1, de, d), lambda v, vb, ve, lo, hi: (ve[v], 0, 0))],
        out_specs=pl.BlockSpec((rb, d), lambda v, vb, ve, lo, hi: (vb[v], 0)),
        scratch_shapes=[pltpu.VMEM((d, de), BF16), pltpu.VMEM((d, de), BF16), pltpu.VMEM((de, d), BF16)],
    )
    return pl.pallas_call(
        _experts_kernel,
        out_shape=jax.ShapeDtypeStruct((xs.shape[0], d), F32),
        grid_spec=grid_spec,
        compiler_params=_cparams(("arbitrary",)),
        name="experts",
    )(vblk, vexp, vlo, vhi, xs, w_gate, w_up, w_down)


def _combine_kernel(dest_ref, x1_ref, w_ref, y_ref, wsg_ref, wsu_ref, wsd_ref, g2_ref, b2_ref, out_ref,
                    ybuf, sem, *, n_tok, tok_off, alpha):
    i = pl.program_id(0)
    n = pl.num_programs(0)
    tc = x1_ref.shape[0]

    def row_copy(tile, slot, k, t):
        src = dest_ref[k * n_tok + tok_off + tile * tc + t]
        return pltpu.make_async_copy(y_ref.at[pl.ds(src, 1)], ybuf.at[slot, k, pl.ds(t, 1)], sem.at[slot])

    def gather(tile, slot, go):
        for k in range(TOP_K):
            _row_gather(tc, go, lambda t: row_copy(tile, slot, k, t))

    @pl.when(i == 0)
    def _():
        gather(0, 0, _START)

    @pl.when(i + 1 < n)
    def _():
        gather(i + 1, (i + 1) % 2, _START)

    slot = i % 2
    gather(i, slot, _WAIT)
    x1 = x1_ref[...]
    w = w_ref[...]
    routed = w[:, 0:1] * ybuf[slot, 0]
    for k in range(1, TOP_K):
        routed = routed + w[:, k:k + 1] * ybuf[slot, k]
    xb = x1.astype(BF16)
    shared = _dot((_silu(_dot(xb, wsg_ref[...])) * _dot(xb, wsu_ref[...])).astype(BF16), wsd_ref[...])
    out_ref[...] = _layer_norm(alpha * x1 + (routed + shared), g2_ref[...], b2_ref[...])


def _combine_weights(p):
    row = lambda v: v.reshape(1, -1).astype(F32)
    return (p['w_sh_gate'].astype(BF16), p['w_sh_up'].astype(BF16), p['w_sh_down'].astype(BF16),
            row(p['ln2_g']), row(p['ln2_b']))


def _combine(dest, x1, w, y_rows, cw, tok_off, alpha, tc):
    m, d = x1.shape
    row = lambda width: pl.BlockSpec((tc, width), lambda i, ds: (i, 0))
    grid_spec = pltpu.PrefetchScalarGridSpec(
        num_scalar_prefetch=1,
        grid=(m // tc,),
        in_specs=[row(d), row(w.shape[1]), pl.BlockSpec(memory_space=pl.ANY)]
                 + [pl.BlockSpec(a.shape, lambda i, ds: (0, 0)) for a in cw],
        out_specs=row(d),
        scratch_shapes=[pltpu.VMEM((2, TOP_K, tc, d), F32), pltpu.SemaphoreType.DMA((2,))],
    )
    return pl.pallas_call(
        functools.partial(_combine_kernel, n_tok=dest.shape[0] // TOP_K, tok_off=tok_off, alpha=alpha),
        out_shape=jax.ShapeDtypeStruct((m, d), F32),
        grid_spec=grid_spec,
        compiler_params=_cparams(("arbitrary",)),
        name="combine",
    )(dest, x1, w, y_rows, *cw)


def _lookup(table, idx):
    hit = idx[..., None] == jnp.arange(table.shape[0], dtype=idx.dtype)
    return jnp.sum(jnp.where(hit, table, 0), axis=-1)


def _route(idx, pos, counts):
    n_tok = idx.shape[1]
    ne = counts.shape[0]
    rb = EXPERT_ROWS
    n_blocks = -(-n_tok * TOP_K // rb)
    end = jnp.cumsum(counts)
    start = end - counts
    dest = _lookup(start, idx) + pos
    first_blk = start // rb
    n_vis = jnp.where(counts > 0, (end - 1) // rb - first_blk + 1, 0)
    vend = jnp.cumsum(n_vis)
    vstart = vend - n_vis
    v = jnp.arange(n_blocks + ne - 1, dtype=I32)
    valid = v < vend[-1]
    v_c = jnp.minimum(v, vend[-1] - 1)
    vexp = jnp.minimum(jnp.sum((vend[None, :] <= v_c[:, None]).astype(I32), axis=1), ne - 1)
    of_visit = lambda a: _lookup(a, vexp)
    vblk = of_visit(first_blk) + (v_c - of_visit(vstart))
    vlo = jnp.where(valid, jnp.clip(of_visit(start) - vblk * rb, 0, rb), 0)
    vhi = jnp.where(valid, jnp.clip(of_visit(end) - vblk * rb, 0, rb), 0)
    visits = tuple(a.astype(I32) for a in (vblk, vexp, vlo, vhi))
    return dest.reshape(-1).astype(I32), visits, n_blocks * rb


def _layer(xp, xs, caches, page_table, p, depth):
    b, t, d = xp.shape
    bd, s, _ = xs.shape
    assert s == 1, "sample group is one new token per sequence"
    pool_c, pool_s, win_buf, state_conv, state_rnn = caches
    past_len = page_table.shape[1] * PAGE_SIZE
    alpha = (2.0 * depth) ** 0.25
    kv6 = lambda a, n, rows: a.reshape(n, rows, 2, N_KV_HEADS, HEAD_DIM)
    rows_major = lambda a: jnp.transpose(a.reshape(a.shape[0], 2, N_KV_HEADS, HEAD_DIM, a.shape[2]), (0, 4, 1, 2, 3))

    wparts = _split_w_in(p['w_in'])
    cw = _compress_weights(p)
    rw = _rglru_weights(p)
    tw = _tail_weights(p)
    mw = _combine_weights(p)

    pos_p = jnp.tile(jnp.arange(t, dtype=I32), b)
    (q, qr, kvc, kvc_t, kvs_t, kvw_t, ksh, vsh, kwh, vwh, gates, u_rnn, u_gate, g_a, g_b) = _proj(
        xp.reshape(b * t, d), pos_p, wparts, 256, seq_len=t)
    kc, vc = _compress_prompt(kvc, b, t, cw)
    o_attn = _nsa_prompt(q, qr, kc, vc, ksh, vsh, kwh, vwh, gates, b, t)
    y_rnn, h_p = _rglru_prompt(u_rnn, u_gate, rw, b, t, 256)
    ne = p['w_router'].shape[1]
    x1_p, idx_p, w_p, pos_r_p, counts_p = _tail(xp.reshape(b * t, d), o_attn, y_rnn.reshape(b * t, -1), g_a, g_b,
                                                tw, jnp.zeros((ne, 1), F32), alpha, 256)
    wn = min(WINDOW, t)
    outs_p = (rows_major(kvc_t), rows_major(kvs_t), rows_major(kvw_t[:, :, t - wn:]),
              u_rnn.reshape(b, t, -1)[:, t - (CONV_W - 1):], h_p.reshape(b, -1))

    pos_s = jnp.full((bd,), past_len, I32)
    (q, qr, kvc_s, kvs_s, kvw_s, _, _, _, _, gates_s, u_rnn_s, u_gate_s, g_a_s, g_b_s) = _proj(
        xs.reshape(bd, d), pos_s, wparts, bd)
    o_c, sel_idx = _nsa_sample_cmp(page_table, _pages_feature_major(pool_c), q.transpose(1, 0, 2), cw, past_len)
    o_s, win_new_t = _nsa_sample_sel(page_table, sel_idx, _pages_feature_major(pool_s), qr.transpose(1, 0, 2),
                                     kvs_s, kvw_s, _pages_feature_major(win_buf), o_c, gates_s, past_len)
    y_rnn_s, h_s = _rglru_step(u_rnn_s, u_gate_s, state_conv.transpose(1, 0, 2), state_rnn, rw)
    x1_s, idx_s, w_s, pos_r_s, counts = _tail(xs.reshape(bd, d), o_s.reshape(bd, Q_WIDTH).astype(BF16), y_rnn_s,
                                              g_a_s, g_b_s, tw, counts_p, alpha, bd)
    conv_s = jnp.concatenate([state_conv[:, 1:], u_rnn_s[:, None, :]], axis=1)
    outs_s = (kv6(kvc_s, bd, 1), kv6(kvs_s, bd, 1), rows_major(win_new_t), conv_s, h_s)

    idx_all = jnp.concatenate([idx_p, idx_s], axis=1)[:TOP_K]
    pos_all = jnp.concatenate([pos_r_p, pos_r_s], axis=1)[:TOP_K]
    dest, visits, n_rows_pad = _route(idx_all, pos_all, counts.reshape(-1).astype(I32))
    xs = _dispatch(dest, x1_p, x1_s, n_rows_pad, 256)
    y_rows = _experts(xs, visits, p['w_exp_gate'], p['w_exp_up'], p['w_exp_down'])
    yp = _combine(dest, x1_p, w_p.T, y_rows, mw, 0, alpha, 128)
    ys = _combine(dest, x1_s, w_s.T, y_rows, mw, b * t, alpha, bd)
    return yp.reshape(b, t, d), ys.reshape(bd, s, d), outs_p, outs_s


def kernel(x_prompt, x_sample, cache_cmp_kv, cache_sel_kv, cache_win_kv, state_conv, state_rnn, page_table,
           w_in, conv_w, conv_b, w_rg_a, b_rg_a, w_rg_i, b_rg_i, lru_lambda, cmp_pos_k, cmp_pos_v,
           w_cmp_k1, w_cmp_k2, w_cmp_v1, w_cmp_v2, w_branch_attn, w_branch_rnn, w_out, ln1_g, ln1_b,
           w_router, router_bias, w_exp_gate, w_exp_up, w_exp_down, w_sh_gate, w_sh_up, w_sh_down, ln2_g, ln2_b):
    weights = dict(w_in=w_in, conv_w=conv_w, conv_b=conv_b, w_rg_a=w_rg_a, b_rg_a=b_rg_a, w_rg_i=w_rg_i,
                   b_rg_i=b_rg_i, lru_lambda=lru_lambda, cmp_pos_k=cmp_pos_k, cmp_pos_v=cmp_pos_v,
                   w_cmp_k1=w_cmp_k1, w_cmp_k2=w_cmp_k2, w_cmp_v1=w_cmp_v1, w_cmp_v2=w_cmp_v2,
                   w_branch_attn=w_branch_attn, w_branch_rnn=w_branch_rnn, w_out=w_out, ln1_g=ln1_g, ln1_b=ln1_b,
                   w_router=w_router, router_bias=router_bias, w_exp_gate=w_exp_gate, w_exp_up=w_exp_up,
                   w_exp_down=w_exp_down, w_sh_gate=w_sh_gate, w_sh_up=w_sh_up, w_sh_down=w_sh_down,
                   ln2_g=ln2_g, ln2_b=ln2_b)
    depth = w_in.shape[0]
    xp, xs = x_prompt, x_sample
    per_layer_p, per_layer_s = [], []
    for l in range(depth):
        p = {k: v[l] for k, v in weights.items()}
        caches = (cache_cmp_kv[l], cache_sel_kv[l], cache_win_kv[l], state_conv[l], state_rnn[l])
        xp, xs, outs_p, outs_s = _layer(xp, xs, caches, page_table, p, depth)
        per_layer_p.append(outs_p)
        per_layer_s.append(outs_s)
    stack = lambda outs, i: jnp.stack([o[i] for o in outs])
    return (xp, xs, stack(per_layer_p, 0), stack(per_layer_s, 0), stack(per_layer_p, 1), stack(per_layer_s, 1),
            stack(per_layer_p, 2), stack(per_layer_s, 2), stack(per_layer_p, 3), stack(per_layer_s, 3),
            stack(per_layer_p, 4), stack(per_layer_s, 4))
```
